```python
import jax, jax.numpy as jnp
from jax import lax
import numpy as np

D_MODEL = 2048
BATCH = 16
SEQ = 256
DEPTH = 2
DEC_BATCH = 8
DEC_SEQ = 4096
PAST_LEN = 256

GRID_W = 64
N_EVEN = (DEPTH + 1) // 2
N_ODD = DEPTH // 2
ALPHA = (2 * DEPTH) ** 0.25
LN_EPS = 1e-5
RMS_EPS = 1e-6
Q_BLOCK = 128
N_MOD = 9
D_FF = 5632
NA_HEADS = 8
NA_HEAD_DIM = 128
NA_WIDTH = NA_HEADS * NA_HEAD_DIM
NA_ROWS = 8
NA_COLS = 16
POOL_WINDOWS = (2, 4, 8, 16)
POOL_GROUPS = 4
POOL_CH = D_MODEL // 8
POOL_WIDTH = POOL_GROUPS * POOL_CH
MIX0_IN = 3 * NA_WIDTH + POOL_WIDTH
MIX0_OUT = NA_WIDTH + POOL_WIDTH
MLA_HEADS = 16
Q_LORA = 512
KV_LORA = 512
QK_NOPE = 128
QK_ROPE = 64
V_DIM = 128
MLA_DOWN = Q_LORA + KV_LORA + QK_ROPE
ROPE_AXIS = QK_ROPE // 2
ROPE_BASE = 10000.0

kernel_name = 'hybrid_diffusion_na_pool_mla_step'


def layer_norm(x, g, b):
    xf = x.astype(jnp.float32)
    mu = jnp.mean(xf, axis=-1, keepdims=True)
    var = jnp.mean(jnp.square(xf - mu), axis=-1, keepdims=True)
    y = (xf - mu) * lax.rsqrt(var + LN_EPS) * g.astype(jnp.float32) + b.astype(jnp.float32)
    return y.astype(x.dtype)


def rms_norm(x, g):
    xf = x.astype(jnp.float32)
    y = xf * lax.rsqrt(jnp.mean(jnp.square(xf), axis=-1, keepdims=True) + RMS_EPS) * g.astype(jnp.float32)
    return y.astype(x.dtype)


def swiglu(h, w1, w3, w2):
    return (jax.nn.silu(h @ w1) * (h @ w3)) @ w2


def modulate(x, mod, j):
    shift = mod[:, None, 3 * j]
    scale = mod[:, None, 3 * j + 1]
    gate = mod[:, None, 3 * j + 2]
    return x * (1 + scale) + shift, gate


def post_norm(x, update, g, b):
    return layer_norm(ALPHA * x + update, g, b)


def dense_attention(q, k, v, scale):
    B_, Lq, H, Dq = q.shape
    nb = Lq // Q_BLOCK
    qb = q.reshape(B_, nb, Q_BLOCK, H, Dq).transpose(1, 0, 2, 3, 4)

    def block(qi):
        s = jnp.einsum('bqhd,bkhd->bhqk', qi, k).astype(jnp.float32) * scale
        p = jax.nn.softmax(s, axis=-1).astype(v.dtype)
        return jnp.einsum('bhqk,bkhd->bqhd', p, v)

    o = lax.map(block, qb)
    return o.transpose(1, 0, 2, 3, 4).reshape(B_, Lq, H, v.shape[-1])


def axial_rope_angles(n_tokens):
    t = jnp.arange(n_tokens)
    inv = ROPE_BASE ** (-jnp.arange(0, ROPE_AXIS, 2, dtype=jnp.float32) / ROPE_AXIS)
    ang_row = (t // GRID_W).astype(jnp.float32)[:, None] * inv[None, :]
    ang_col = (t % GRID_W).astype(jnp.float32)[:, None] * inv[None, :]
    return ang_row, ang_col


def _rotate(x, ang):
    x1, x2 = jnp.split(x, 2, axis=-1)
    cos, sin = jnp.cos(ang), jnp.sin(ang)
    return jnp.concatenate([x1 * cos - x2 * sin, x2 * cos + x1 * sin], axis=-1)


def axial_rope(x, ang_row, ang_col):
    bshape = (x.shape[1],) + (1,) * (x.ndim - 3) + (ang_row.shape[-1],)
    xr, xc = jnp.split(x.astype(jnp.float32), 2, axis=-1)
    out = jnp.concatenate([_rotate(xr, ang_row.reshape(bshape)),
                           _rotate(xc, ang_col.reshape(bshape))], axis=-1)
    return out.astype(x.dtype)


def multiscale_pool(u, w_pool, pool_scale):
    B_, L, _ = u.shape
    ug = u.reshape(B_, L, POOL_GROUPS, POOL_CH)
    cs = jnp.cumsum(ug.astype(jnp.float32), axis=1)
    cs = jnp.concatenate([jnp.zeros_like(cs[:, :1]), cs], axis=1)
    t = jnp.arange(L)
    means = []
    for g, w in enumerate(POOL_WINDOWS):
        lo = jnp.clip(t - w // 2, 0, L)
        hi = jnp.clip(t + w // 2, 0, L)
        csg = cs[:, :, g]
        s = jnp.take(csg, hi, axis=1) - jnp.take(csg, lo, axis=1)
        means.append(s / (hi - lo).astype(jnp.float32)[None, :, None])
    pooled = jnp.stack(means, axis=2)
    d = (pooled - ug.astype(jnp.float32)).astype(u.dtype)
    y = jnp.einsum('blgc,gcd->blgd', d, w_pool).reshape(B_, L, POOL_WIDTH)
    return y * pool_scale


def neighbourhood_attention(q, k, v, k_ctx, v_ctx, rpb):
    B_, L, H, Dh = q.shape
    rows = L // GRID_W
    kr = min(NA_ROWS, rows)
    kc = NA_COLS
    scale = Dh ** -0.5
    col = jnp.arange(GRID_W)
    col_start = jnp.clip(col - kc // 2, 0, GRID_W - kc)
    col_idx = col_start[:, None] + jnp.arange(kc)[None, :]
    dc = col_idx - col[:, None]
    kg = k.reshape(B_, rows, GRID_W, H, Dh)
    vg = v.reshape(B_, rows, GRID_W, H, Dh)
    q_rows = q.reshape(B_, rows, GRID_W, H, Dh).transpose(1, 0, 2, 3, 4)

    def row_block(args):
        r, qb = args
        rs = jnp.clip(r - kr // 2, 0, rows - kr)
        kb = lax.dynamic_slice_in_dim(kg, rs, kr, axis=1)
        vb = lax.dynamic_slice_in_dim(vg, rs, kr, axis=1)
        kq = kb[:, :, col_idx]
        vq = vb[:, :, col_idx]
        dr = rs + jnp.arange(kr) - r
        bias = rpb[:, dr[:, None, None] + NA_ROWS - 1, dc[None] + NA_COLS - 1]
        bias = bias.transpose(0, 2, 1, 3).reshape(H, GRID_W, kr * kc).astype(jnp.float32)
        s_loc = jnp.einsum('bqhd,biqjhd->bhqij', qb, kq).astype(jnp.float32)
        s_loc = s_loc.reshape(B_, H, GRID_W, kr * kc) * scale + bias[None]
        s_ctx = jnp.einsum('bqhd,bkhd->bhqk', qb, k_ctx).astype(jnp.float32) * scale
        p = jax.nn.softmax(jnp.concatenate([s_loc, s_ctx], axis=-1), axis=-1).astype(v.dtype)
        p_loc = p[..., :kr * kc].reshape(B_, H, GRID_W, kr, kc)
        p_ctx = p[..., kr * kc:]
        return (jnp.einsum('bhqij,biqjhd->bqhd', p_loc, vq)
                + jnp.einsum('bhqk,bkhd->bqhd', p_ctx, v_ctx))

    o = lax.map(row_block, (jnp.arange(rows), q_rows))
    return o.transpose(1, 0, 2, 3, 4).reshape(B_, L, H, Dh)


def even_mixer(h, ctx, w_in, w_out, rpb, w_pool, pool_scale):
    B_, L, _ = h.shape
    proj = h @ w_in
    q = proj[..., :NA_WIDTH].reshape(B_, L, NA_HEADS, NA_HEAD_DIM)
    k = proj[..., NA_WIDTH:2 * NA_WIDTH].reshape(B_, L, NA_HEADS, NA_HEAD_DIM)
    v = proj[..., 2 * NA_WIDTH:3 * NA_WIDTH].reshape(B_, L, NA_HEADS, NA_HEAD_DIM)
    u = proj[..., 3 * NA_WIDTH:]
    if ctx is None:
        a = dense_attention(q, k, v, NA_HEAD_DIM ** -0.5)
        st = (k, v)
    else:
        a = neighbourhood_attention(q, k, v, ctx[0], ctx[1], rpb)
        st = None
    pooled = multiscale_pool(u, w_pool, pool_scale)
    y = jnp.concatenate([a.reshape(B_, L, NA_WIDTH), pooled], axis=-1) @ w_out
    return y, st


def mla_expand(ckv, kpe, w_ukv):
    B_, L, _ = ckv.shape
    kv = (ckv @ w_ukv).reshape(B_, L, MLA_HEADS, QK_NOPE + V_DIM)
    k_pe = jnp.broadcast_to(kpe[:, :, None, :], (B_, L, MLA_HEADS, QK_ROPE))
    k = jnp.concatenate([kv[..., :QK_NOPE], k_pe], axis=-1)
    return k, kv[..., QK_NOPE:]


def odd_mixer(h, ctx, w_down, q_norm, w_uq, kv_norm, w_ukv, w_out):
    B_, L, _ = h.shape
    down = h @ w_down
    cq = rms_norm(down[..., :Q_LORA], q_norm)
    ckv = rms_norm(down[..., Q_LORA:Q_LORA + KV_LORA], kv_norm)
    kpe = down[..., Q_LORA + KV_LORA:]
    q = (cq @ w_uq).reshape(B_, L, MLA_HEADS, QK_NOPE + QK_ROPE)
    if ctx is None:
        k, v = mla_expand(ckv, kpe, w_ukv)
        st = (ckv, kpe)
    else:
        ang_r, ang_c = axial_rope_angles(L)
        q = jnp.concatenate([q[..., :QK_NOPE], axial_rope(q[..., QK_NOPE:], ang_r, ang_c)], axis=-1)
        k_lat, v_lat = mla_expand(ckv, axial_rope(kpe, ang_r, ang_c), w_ukv)
        k_ctx, v_ctx = mla_expand(ctx[0], ctx[1], w_ukv)
        k = jnp.concatenate([k_lat, k_ctx], axis=1)
        v = jnp.concatenate([v_lat, v_ctx], axis=1)
        st = None
    o = dense_attention(q, k, v, (QK_NOPE + QK_ROPE) ** -0.5)
    return o.reshape(B_, L, MLA_HEADS * V_DIM) @ w_out, st


def run_trunk(x, cond, caches, w_mod, b_mod, ln_g, ln_b, ffn_w1, ffn_w3, ffn_w2,
              na_w_in, mix0_w_out, na_rpb, pool_w, pool_scale,
              mla_w_down, mla_q_norm, mla_w_uq, mla_kv_norm, mla_w_ukv, mla_w_out):
    ctx_even, ctx_odd = [], []
    for l in range(DEPTH):
        mod = (jax.nn.silu(cond) @ w_mod[l] + b_mod[l]).reshape(cond.shape[0], N_MOD, D_MODEL)
        h, gate = modulate(x, mod, 0)
        x = post_norm(x, 0.5 * gate * swiglu(h, ffn_w1[l, 0], ffn_w3[l, 0], ffn_w2[l, 0]),
                      ln_g[l, 0], ln_b[l, 0])
        h, gate = modulate(x, mod, 1)
        i = l // 2
        if l % 2 == 0:
            ctx = None if caches is None else (caches[0][:, i], caches[1][:, i])
            y, st = even_mixer(h, ctx, na_w_in[i], mix0_w_out[i], na_rpb[i], pool_w[i], pool_scale[i])
            ctx_even.append(st)
        else:
            ctx = None if caches is None else (caches[2][:, i], caches[3][:, i])
            y, st = odd_mixer(h, ctx, mla_w_down[i], mla_q_norm[i], mla_w_uq[i],
                              mla_kv_norm[i], mla_w_ukv[i], mla_w_out[i])
            ctx_odd.append(st)
        x = post_norm(x, gate * y, ln_g[l, 1], ln_b[l, 1])
        h, gate = modulate(x, mod, 2)
        x = post_norm(x, 0.5 * gate * swiglu(h, ffn_w1[l, 1], ffn_w3[l, 1], ffn_w2[l, 1]),
                      ln_g[l, 2], ln_b[l, 2])
    return x, ctx_even, ctx_odd


def setup_inputs(seed: int = 0) -> dict:
    key = jax.random.key(seed)
    ks = iter(jax.random.split(key, 40))

    def nrm(shape, scale):
        return jax.random.normal(next(ks), shape, jnp.float32) * scale

    beta = (8 * DEPTH) ** -0.25
    D = D_MODEL
    return {
        'x_prompt': nrm((BATCH, SEQ, D), 1.0),
        'x_sample': nrm((DEC_BATCH, DEC_SEQ, D), 1.0),
        'cache_na_k': nrm((DEC_BATCH, N_EVEN, PAST_LEN, NA_HEADS, NA_HEAD_DIM), 1.0),
        'cache_na_v': nrm((DEC_BATCH, N_EVEN, PAST_LEN, NA_HEADS, NA_HEAD_DIM), 1.0),
        'cache_mla_ckv': nrm((DEC_BATCH, N_ODD, PAST_LEN, KV_LORA), 1.0),
        'cache_mla_kpe': nrm((DEC_BATCH, N_ODD, PAST_LEN, QK_ROPE), 1.0),
        'c': nrm((DEC_BATCH, D), 1.0),
        'c_ctx': nrm((D,), 1.0),
        'w_mod': nrm((DEPTH, D, N_MOD * D), 0.5 * D ** -0.5),
        'b_mod': nrm((DEPTH, N_MOD * D), 0.02),
        'ln_g': 1.0 + nrm((DEPTH, 3, D), 0.02),
        'ln_b': nrm((DEPTH, 3, D), 0.02),
        'ffn_w1': nrm((DEPTH, 2, D, D_FF), D ** -0.5),
        'ffn_w3': nrm((DEPTH, 2, D, D_FF), D ** -0.5),
        'ffn_w2': nrm((DEPTH, 2, D_FF, D), beta * D_FF ** -0.5),
        'na_w_in': nrm((N_EVEN, D, MIX0_IN), D ** -0.5),
        'mix0_w_out': nrm((N_EVEN, MIX0_OUT, D), beta * MIX0_OUT ** -0.5),
        'na_rpb': nrm((N_EVEN, NA_HEADS, 2 * NA_ROWS - 1, 2 * NA_COLS - 1), 0.1),
        'pool_w': nrm((N_EVEN, POOL_GROUPS, POOL_CH, POOL_CH), POOL_CH ** -0.5),
        'pool_scale': 1.0 + nrm((N_EVEN, POOL_WIDTH), 0.02),
        'mla_w_down': nrm((N_ODD, D, MLA_DOWN), D ** -0.5),
        'mla_q_norm': 1.0 + nrm((N_ODD, Q_LORA), 0.02),
        'mla_w_uq': nrm((N_ODD, Q_LORA, MLA_HEADS * (QK_NOPE + QK_ROPE)), Q_LORA ** -0.5),
        'mla_kv_norm': 1.0 + nrm((N_ODD, KV_LORA), 0.02),
        'mla_w_ukv': nrm((N_ODD, KV_LORA, MLA_HEADS * (QK_NOPE + V_DIM)), KV_LORA ** -0.5),
        'mla_w_out': nrm((N_ODD, MLA_HEADS * V_DIM, D), beta * (MLA_HEADS * V_DIM) ** -0.5),
    }


def reference(x_prompt, x_sample, cache_na_k, cache_na_v, cache_mla_ckv, cache_mla_kpe, c, c_ctx,
              w_mod, b_mod, ln_g, ln_b, ffn_w1, ffn_w3, ffn_w2,
              na_w_in, mix0_w_out, na_rpb, pool_w, pool_scale,
              mla_w_down, mla_q_norm, mla_w_uq, mla_kv_norm, mla_w_ukv, mla_w_out):
    y_prompt, ctx_even, ctx_odd = run_trunk(
        x_prompt, c_ctx[None, :], None, w_mod, b_mod, ln_g, ln_b, ffn_w1, ffn_w3, ffn_w2,
        na_w_in, mix0_w_out, na_rpb, pool_w, pool_scale,
        mla_w_down, mla_q_norm, mla_w_uq, mla_kv_norm, mla_w_ukv, mla_w_out)
    y_sample, _, _ = run_trunk(
        x_sample, c, (cache_na_k, cache_na_v, cache_mla_ckv, cache_mla_kpe),
        w_mod, b_mod, ln_g, ln_b, ffn_w1, ffn_w3, ffn_w2,
        na_w_in, mix0_w_out, na_rpb, pool_w, pool_scale,
        mla_w_down, mla_q_norm, mla_w_uq, mla_kv_norm, mla_w_ukv, mla_w_out)
    new_na_k = jnp.stack([s[0] for s in ctx_even], axis=1)
    new_na_v = jnp.stack([s[1] for s in ctx_even], axis=1)
    new_mla_ckv = jnp.stack([s[0] for s in ctx_odd], axis=1)
    new_mla_kpe = jnp.stack([s[1] for s in ctx_odd], axis=1)
    return (y_prompt, y_sample, new_na_k, new_na_v, new_mla_ckv, new_mla_kpe)
```

```python
import functools
import math

import numpy as np
import jax
import jax.numpy as jnp
from jax import lax
from jax.experimental import pallas as pl
from jax.experimental.pallas import tpu as pltpu

GRID_W = 64
LN_EPS = 1e-5
RMS_EPS = 1e-6
POOL_WINDOWS = (2, 4, 8, 16)
ROPE_BASE = 10000.0
MASKED = -1e30
NA_GROUP_ROWS = 4
NA_WINDOW_ROWS = 12
V7X_VMEM_LIMIT_BYTES = 56 * 1024 * 1024

F32 = jnp.float32
BF16 = jnp.bfloat16
_NT = (((1,), (1,)), ((), ()))


def _params(*semantics):
    return pltpu.CompilerParams(dimension_semantics=semantics,
                                vmem_limit_bytes=V7X_VMEM_LIMIT_BYTES)


def _dot(a, b):
    return jnp.dot(a, b, preferred_element_type=F32)


def _post_norm(x, update, g, b, alpha):
    z = alpha * x + update
    mu = jnp.mean(z, axis=-1, keepdims=True)
    zc = z - mu
    var = jnp.mean(zc * zc, axis=-1, keepdims=True)
    return zc * lax.rsqrt(var + LN_EPS) * g + b


def _modulated(x, m, j):
    return x * (1.0 + m[3 * j + 1:3 * j + 2]) + m[3 * j:3 * j + 1]


def _mod_kernel(c_ref, w_ref, b_ref, o_ref):
    c = c_ref[...]
    h = (c * jax.nn.sigmoid(c)).astype(BF16)
    o_ref[...] = _dot(h, w_ref[...].astype(BF16)) + b_ref[...]


def _mod_call(cond, w_mod, b_mod, tn=1024):
    depth, d, n = w_mod.shape
    r = cond.shape[0]
    return pl.pallas_call(
        _mod_kernel,
        out_shape=jax.ShapeDtypeStruct((depth, r, n), F32),
        grid=(depth, n // tn),
        in_specs=[pl.BlockSpec((r, d), lambda l, j: (0, 0)),
                  pl.BlockSpec((None, d, tn), lambda l, j: (l, 0, j)),
                  pl.BlockSpec((None, 1, tn), lambda l, j: (l, 0, j))],
        out_specs=pl.BlockSpec((None, r, tn), lambda l, j: (l, 0, j)),
        compiler_params=_params("parallel", "parallel"),
        name="mod_proj",
    )(cond, w_mod, b_mod.reshape(depth, 1, n))


def _ffn_kernel(x_ref, mod_ref, w1_ref, w3_ref, w2_ref, g_ref, b_ref, o_ref, h_ref, *, j, alpha):
    f = pl.program_id(1)

    @pl.when(f == 0)
    def _():
        h_ref[...] = _modulated(x_ref[...], mod_ref[...], j).astype(BF16)
        o_ref[...] = jnp.zeros_like(o_ref)

    h = h_ref[...]
    a = _dot(h, w1_ref[...])
    b = _dot(h, w3_ref[...])
    act = (a * jax.nn.sigmoid(a) * b).astype(BF16)
    o_ref[...] += _dot(act, w2_ref[...])

    @pl.when(f == pl.num_programs(1) - 1)
    def _():
        gate = mod_ref[...][3 * j + 2:3 * j + 3]
        o_ref[...] = _post_norm(x_ref[...], 0.5 * gate * o_ref[...], g_ref[...], b_ref[...], alpha)


def _ffn_call(x, mod, tokens_per_cond, w1, w3, w2, l, s, ln_g, ln_b, j, alpha, tm=512, tf=512):
    t, d = x.shape
    ff = w1.shape[-1]
    tm = min(tm, tokens_per_cond)
    tf = min(tf, ff)
    assert tokens_per_cond % tm == 0 and ff % tf == 0
    per = tokens_per_cond // tm
    return pl.pallas_call(
        functools.partial(_ffn_kernel, j=j, alpha=alpha),
        out_shape=jax.ShapeDtypeStruct((t, d), F32),
        grid=(t // tm, ff // tf),
        in_specs=[pl.BlockSpec((tm, d), lambda i, f: (i, 0)),
                  pl.BlockSpec((None, 9, d), lambda i, f: (i // per, 0, 0)),
                  pl.BlockSpec((None, None, d, tf), lambda i, f: (l, s, 0, f)),
                  pl.BlockSpec((None, None, d, tf), lambda i, f: (l, s, 0, f)),
                  pl.BlockSpec((None, None, tf, d), lambda i, f: (l, s, f, 0)),
                  pl.BlockSpec((1, d), lambda i, f: (0, 0)),
                  pl.BlockSpec((1, d), lambda i, f: (0, 0))],
        out_specs=pl.BlockSpec((tm, d), lambda i, f: (i, 0)),
        scratch_shapes=[pltpu.VMEM((tm, d), BF16)],
        compiler_params=_params("parallel", "arbitrary"),
        name="ffn",
    )(x, mod, w1, w3, w2, ln_g, ln_b)


def _na_proj_kernel(x_ref, mod_ref, w_ref, o_ref, *rest, q_scale, with_kv):
    h_ref = rest[-1]
    n = pl.program_id(1)

    @pl.when(n == 0)
    def _():
        h_ref[...] = _modulated(x_ref[...], mod_ref[...], 1).astype(BF16)

    y = _dot(h_ref[...], w_ref[...])
    o_ref[...] = jnp.where(n == 0, y * q_scale, y).astype(BF16)
    if with_kv:
        k_ref, v_ref = rest[0], rest[1]

        @pl.when(n == 1)
        def _():
            k_ref[...] = y

        @pl.when(n == 2)
        def _():
            v_ref[...] = y


def _na_proj_call(x, mod, tokens_per_cond, w_in, i_even, na_width, q_scale, with_kv, tm=512):
    t, d = x.shape
    n_in = w_in.shape[-1]
    tn = na_width
    assert n_in % tn == 0, "q, k, v and pooled widths must share one tile width"
    tm = min(tm, tokens_per_cond)
    per = tokens_per_cond // tm
    out_shape = [jax.ShapeDtypeStruct((t, n_in), BF16)]
    out_specs = [pl.BlockSpec((tm, tn), lambda i, n: (i, n))]
    if with_kv:
        out_shape += [jax.ShapeDtypeStruct((t, tn), F32)] * 2
        out_specs += [pl.BlockSpec((tm, tn), lambda i, n: (i, 0))] * 2
    return pl.pallas_call(
        functools.partial(_na_proj_kernel, q_scale=q_scale, with_kv=with_kv),
        out_shape=out_shape,
        grid=(t // tm, n_in // tn),
        in_specs=[pl.BlockSpec((tm, d), lambda i, n: (i, 0)),
                  pl.BlockSpec((None, 9, d), lambda i, n: (i // per, 0, 0)),
                  pl.BlockSpec((None, d, tn), lambda i, n: (i_even, 0, n))],
        out_specs=out_specs,
        scratch_shapes=[pltpu.VMEM((tm, d), BF16)],
        compiler_params=_params("parallel", "arbitrary"),
        name="na_proj",
    )(x, mod, w_in)


def _attn_kernel(*refs, n_seg):
    q_ref, o_ref = refs[0], refs[-1]
    q = q_ref[...]
    s = [lax.dot_general(q, refs[1 + 2 * i][...], _NT, preferred_element_type=F32)
         for i in range(n_seg)]
    m = functools.reduce(jnp.maximum, [jnp.max(si, axis=-1, keepdims=True) for si in s])
    p = [jnp.exp(si - m) for si in s]
    den = functools.reduce(jnp.add, [jnp.sum(pi, axis=-1, keepdims=True) for pi in p])
    o = functools.reduce(jnp.add, [_dot(p[i].astype(BF16), refs[2 + 2 * i][...])
                                   for i in range(n_seg)])
    o_ref[...] = (o / den).astype(o_ref.dtype)


def _attn_call(q, q_col0, dq, segments, n_heads, dv, tq, name):
    b, l, _ = q.shape
    tq = min(tq, l)
    in_specs = [pl.BlockSpec((None, tq, dq), lambda bi, h, qi: (bi, qi, q_col0 + h))]
    args = [q]
    for k, kc, v, vc in segments:
        lk = k.shape[1]
        in_specs.append(pl.BlockSpec((None, lk, dq), lambda bi, h, qi, kc=kc: (bi, 0, kc + h)))
        in_specs.append(pl.BlockSpec((None, lk, dv), lambda bi, h, qi, vc=vc: (bi, 0, vc + h)))
        args += [k, v]
    return pl.pallas_call(
        functools.partial(_attn_kernel, n_seg=len(segments)),
        out_shape=jax.ShapeDtypeStruct((b, l, n_heads * dv), BF16),
        grid=(b, n_heads, l // tq),
        in_specs=in_specs,
        out_specs=pl.BlockSpec((None, tq, dv), lambda bi, h, qi: (bi, qi, h)),
        compiler_params=_params("parallel", "parallel", "arbitrary"),
        name=name,
    )(*args)


def _na_kernel(q_ref, k_ref, v_ref, kc_ref, vc_ref, bias_ref, o_ref, *, n_rows, half_rows):
    g = pl.program_id(2)
    base_row = jnp.clip(g * NA_GROUP_ROWS - half_rows, 0, n_rows - NA_WINDOW_ROWS)
    base = pl.multiple_of(base_row * GRID_W, GRID_W)
    win = NA_WINDOW_ROWS * GRID_W
    q = q_ref[...]
    kw = k_ref[pl.ds(base, win), :]
    vw = v_ref[pl.ds(base, win), :]
    s_loc = lax.dot_general(q, kw, _NT, preferred_element_type=F32) + bias_ref[...]
    s_ctx = lax.dot_general(q, kc_ref[...], _NT, preferred_element_type=F32)
    m = jnp.maximum(jnp.max(s_loc, axis=-1, keepdims=True), jnp.max(s_ctx, axis=-1, keepdims=True))
    p_loc = jnp.exp(s_loc - m)
    p_ctx = jnp.exp(s_ctx - m)
    den = jnp.sum(p_loc, axis=-1, keepdims=True) + jnp.sum(p_ctx, axis=-1, keepdims=True)
    o = _dot(p_loc.astype(BF16), vw) + _dot(p_ctx.astype(BF16), vc_ref[...])
    o_ref[...] = (o / den).astype(o_ref.dtype)


def _na_group_indices(g, n_rows, na_rows, na_cols):
    kr, kc = min(na_rows, n_rows), na_cols
    base = int(np.clip(g * NA_GROUP_ROWS - kr // 2, 0, n_rows - NA_WINDOW_ROWS))
    rq = (g * NA_GROUP_ROWS + np.arange(NA_GROUP_ROWS))[:, None, None, None]
    cq = np.arange(GRID_W)[None, :, None, None]
    rk = (base + np.arange(NA_WINDOW_ROWS))[None, None, :, None]
    ck = np.arange(GRID_W)[None, None, None, :]
    rs = np.clip(rq - kr // 2, 0, n_rows - kr)
    cs = np.clip(cq - kc // 2, 0, GRID_W - kc)
    ok = (rk >= rs) & (rk < rs + kr) & (ck >= cs) & (ck < cs + kc)
    assert (ok.sum(axis=(2, 3)) == kr * kc).all(), "key window misses part of a neighbourhood"
    shape = (NA_GROUP_ROWS * GRID_W, NA_WINDOW_ROWS * GRID_W)
    idx_r = np.broadcast_to(np.clip(rk - rq + na_rows - 1, 0, 2 * na_rows - 2), ok.shape)
    idx_c = np.broadcast_to(np.clip(ck - cq + na_cols - 1, 0, 2 * na_cols - 2), ok.shape)
    ok = ok.reshape(shape)
    return np.where(ok, idx_r.reshape(shape), 0), np.where(ok, idx_c.reshape(shape), 0), ok


def _na_bias_table(rpb, n_rows):
    na_rows, na_cols = (rpb.shape[1] + 1) // 2, (rpb.shape[2] + 1) // 2
    n_groups = n_rows // NA_GROUP_ROWS
    per_group = [_na_group_indices(g, n_rows, na_rows, na_cols) for g in range(n_groups)]
    for g in range(2, n_groups - 1):
        assert all((a == b).all() for a, b in zip(per_group[g], per_group[1])), "interior groups differ"
    idx_r, idx_c, valid = (np.stack([per_group[g][i] for g in (0, 1, n_groups - 1)]) for i in range(3))
    return jnp.where(valid[None], rpb[:, idx_r, idx_c], MASKED).astype(F32)


def _na_call(proj, kc, vc, bias, n_heads, dh, na_rows):
    b, l, _ = proj.shape
    n_rows = l // GRID_W
    assert n_rows % NA_GROUP_ROWS == 0 and n_rows >= NA_WINDOW_ROWS + NA_GROUP_ROWS
    assert NA_GROUP_ROWS - 1 + min(na_rows, n_rows) <= NA_WINDOW_ROWS
    n_groups = n_rows // NA_GROUP_ROWS
    tq = NA_GROUP_ROWS * GRID_W
    win = NA_WINDOW_ROWS * GRID_W
    lc = kc.shape[1]

    def variant(g):
        return jnp.where(g == 0, 0, jnp.where(g == n_groups - 1, 2, 1))

    return pl.pallas_call(
        functools.partial(_na_kernel, n_rows=n_rows, half_rows=min(na_rows, n_rows) // 2),
        out_shape=jax.ShapeDtypeStruct((b, l, n_heads * dh), BF16),
        grid=(b, n_heads, n_groups),
        in_specs=[pl.BlockSpec((None, tq, dh), lambda bi, h, g: (bi, g, h)),
                  pl.BlockSpec((None, l, dh), lambda bi, h, g: (bi, 0, n_heads + h)),
                  pl.BlockSpec((None, l, dh), lambda bi, h, g: (bi, 0, 2 * n_heads + h)),
                  pl.BlockSpec((None, lc, dh), lambda bi, h, g: (bi, 0, h)),
                  pl.BlockSpec((None, lc, dh), lambda bi, h, g: (bi, 0, h)),
                  pl.BlockSpec((None, None, tq, win), lambda bi, h, g: (h, variant(g), 0, 0))],
        out_specs=pl.BlockSpec((None, tq, dh), lambda bi, h, g: (bi, g, h)),
        compiler_params=_params("parallel", "parallel", "arbitrary"),
        name="na_attn",
    )(proj, proj, proj, kc, vc, bias)


def _pool_kernel(u_ref, w_ref, s_ref, o_ref, *, chunk, window):
    l = u_ref.shape[0]
    half = lax.shift_left(jnp.int32(1), pl.program_id(1).astype(jnp.int32))
    w = w_ref[...]
    scale = s_ref[...]

    def body(c, carry):
        t0 = pl.multiple_of(c * chunk, chunk)
        s0 = pl.multiple_of(jnp.clip(t0 - (window - chunk) // 2, 0, l - window), 16)
        t = t0 + lax.broadcasted_iota(jnp.int32, (chunk, 1), 0)
        lo = jnp.maximum(t - half, 0)
        hi = jnp.minimum(t + half, l)
        pos = s0 + lax.broadcasted_iota(jnp.int32, (1, window), 1)
        band = jnp.where((pos >= lo) & (pos < hi), 1.0, 0.0).astype(BF16)
        sums = _dot(band, u_ref[pl.ds(s0, window), :])
        mean = sums / (hi - lo).astype(F32)
        d = (mean - u_ref[pl.ds(t0, chunk), :].astype(F32)).astype(BF16)
        o_ref[pl.ds(t0, chunk), :] = (_dot(d, w) * scale).astype(o_ref.dtype)
        return carry

    lax.fori_loop(0, l // chunk, body, 0)


def _pool_call(proj, u_col0, w_pool, pool_scale, i_even):
    b, l, _ = proj.shape
    n_groups, ch, _ = w_pool.shape[1:]
    assert n_groups == len(POOL_WINDOWS) and all(w == 2 << g for g, w in enumerate(POOL_WINDOWS))
    chunk = min(256, l)
    window = min(2 * chunk, l)
    assert l % chunk == 0 and (window == l or window - chunk >= max(POOL_WINDOWS))
    return pl.pallas_call(
        functools.partial(_pool_kernel, chunk=chunk, window=window),
        out_shape=jax.ShapeDtypeStruct((b, l, n_groups * ch), BF16),
        grid=(b, n_groups),
        in_specs=[pl.BlockSpec((None, l, ch), lambda bi, g: (bi, 0, u_col0 + g)),
                  pl.BlockSpec((None, None, ch, ch), lambda bi, g: (i_even, g, 0, 0)),
                  pl.BlockSpec((None, 1, ch), lambda bi, g: (i_even, 0, g))],
        out_specs=pl.BlockSpec((None, l, ch), lambda bi, g: (bi, 0, g)),
        compiler_params=_params("parallel", "parallel"),
        name="pool",
    )(proj, w_pool, pool_scale)


def _out_kernel(*refs, n_lhs, alpha):
    x_ref, mod_ref = refs[0], refs[1]
    g_ref, b_ref, o_ref = refs[-3], refs[-2], refs[-1]
    y = functools.reduce(jnp.add, [_dot(refs[2 + 2 * i][...], refs[3 + 2 * i][...])
                                   for i in range(n_lhs)])
    gate = mod_ref[...][5:6]
    o_ref[...] = _post_norm(x_ref[...], gate * y, g_ref[...], b_ref[...], alpha)


def _out_call(x, mod, tokens_per_cond, lhs_w, ln_g, ln_b, alpha, tm=256):
    t, d = x.shape
    tm = min(tm, tokens_per_cond)
    per = tokens_per_cond // tm
    in_specs = [pl.BlockSpec((tm, d), lambda i: (i, 0)),
                pl.BlockSpec((None, 9, d), lambda i: (i // per, 0, 0))]
    args = [x, mod]
    for lhs, w, w_block, w_index in lhs_w:
        in_specs.append(pl.BlockSpec((tm, lhs.shape[1]), lambda i: (i, 0)))
        in_specs.append(pl.BlockSpec(w_block, lambda i, w_index=w_index: w_index))
        args += [lhs, w]
    in_specs += [pl.BlockSpec((1, d), lambda i: (0, 0))] * 2
    args += [ln_g, ln_b]
    return pl.pallas_call(
        functools.partial(_out_kernel, n_lhs=len(lhs_w), alpha=alpha),
        out_shape=jax.ShapeDtypeStruct((t, d), F32),
        grid=(t // tm,),
        in_specs=in_specs,
        out_specs=pl.BlockSpec((tm, d), lambda i: (i, 0)),
        compiler_params=_params("parallel"),
        name="out_proj",
    )(*args)


def _rms(x, g):
    return x * lax.rsqrt(jnp.mean(x * x, axis=-1, keepdims=True) + RMS_EPS) * g


def _rotary_pair(t, tab):
    u = t * tab
    return u + pltpu.roll(u, u.shape[-1] // 2, 1)


def _rope_half_mask(shape):
    return lax.broadcasted_iota(jnp.int32, shape, 1) < shape[-1] // 2


def _head_store(ref, h, width, value):
    ref[:, h * width:h * width + value.shape[-1]] = value.astype(ref.dtype)


def _mla_proj_kernel(x_ref, mod_ref, wd_ref, qn_ref, kvn_ref, wq_ref, wk_ref, wv_ref, tab_ref,
                     q_ref, k_ref, v_ref, *state_refs, n_heads, q_lora, kv_lora, nope, q_scale):
    h = _modulated(x_ref[...], mod_ref[...], 1).astype(BF16)
    down = _dot(h, wd_ref[...])
    cq = _rms(down[:, :q_lora], qn_ref[...])
    ckv = _rms(down[:, q_lora:q_lora + kv_lora], kvn_ref[...])
    pe = down[:, q_lora + kv_lora:]
    tab = tab_ref[...]
    if state_refs:
        state_refs[0][...] = ckv
        state_refs[1][...] = pe[:, :pe.shape[-1] // 2]
    kpe = jnp.where(_rope_half_mask(pe.shape), _rotary_pair(pe, tab), 0.0)
    q = _dot(cq.astype(BF16), wq_ref[...]) * q_scale
    ckv_b = ckv.astype(BF16)
    kn = _dot(ckv_b, wk_ref[...])
    v_ref[...] = _dot(ckv_b, wv_ref[...]).astype(BF16)
    dq = 2 * nope
    for hd in range(n_heads):
        _head_store(q_ref, hd, dq, q[:, hd * dq:hd * dq + nope])
        q_ref[:, hd * dq + nope:(hd + 1) * dq] = _rotary_pair(
            q[:, hd * dq + nope:(hd + 1) * dq], tab).astype(BF16)
        _head_store(k_ref, hd, dq, kn[:, hd * nope:(hd + 1) * nope])
        k_ref[:, hd * dq + nope:(hd + 1) * dq] = kpe.astype(BF16)


def _mla_proj_call(x, mod, tokens_per_cond, wd, qn, kvn, wq, wk, wv, tab, i_odd, dims, with_state,
                   tm=256):
    n_heads, q_lora, kv_lora, nope, rope, q_scale = dims
    t, d = x.shape
    tm = min(tm, tokens_per_cond, tab.shape[0])
    per = tokens_per_cond // tm
    tab_blocks = tab.shape[0] // tm
    dq = 2 * nope
    out_shape = [jax.ShapeDtypeStruct((t, n_heads * dq), BF16),
                 jax.ShapeDtypeStruct((t, n_heads * dq), BF16),
                 jax.ShapeDtypeStruct((t, n_heads * nope), BF16)]
    out_specs = [pl.BlockSpec((tm, n_heads * dq), lambda i: (i, 0)),
                 pl.BlockSpec((tm, n_heads * dq), lambda i: (i, 0)),
                 pl.BlockSpec((tm, n_heads * nope), lambda i: (i, 0))]
    if with_state:
        out_shape += [jax.ShapeDtypeStruct((t, kv_lora), F32), jax.ShapeDtypeStruct((t, rope), F32)]
        out_specs += [pl.BlockSpec((tm, kv_lora), lambda i: (i, 0)),
                      pl.BlockSpec((tm, rope), lambda i: (i, 0))]
    whole = lambda a: pl.BlockSpec((None,) + a.shape[1:], lambda i: (i_odd,) + (0,) * (a.ndim - 1))
    return pl.pallas_call(
        functools.partial(_mla_proj_kernel, n_heads=n_heads, q_lora=q_lora, kv_lora=kv_lora,
                          nope=nope, q_scale=q_scale),
        out_shape=out_shape,
        grid=(t // tm,),
        in_specs=[pl.BlockSpec((tm, d), lambda i: (i, 0)),
                  pl.BlockSpec((None, 9, d), lambda i: (i // per, 0, 0)),
                  whole(wd), whole(qn), whole(kvn), whole(wq), whole(wk), whole(wv),
                  pl.BlockSpec((tm, tab.shape[1]), lambda i: (i % tab_blocks, 0))],
        out_specs=out_specs,
        compiler_params=_params("parallel"),
        name="mla_proj",
    )(x, mod, wd, qn, kvn, wq, wk, wv, tab)


def _mla_expand_kernel(ckv_ref, kpe_ref, wk_ref, wv_ref, k_ref, v_ref, *, n_heads, nope):
    ckv = ckv_ref[...].astype(BF16)
    kn = _dot(ckv, wk_ref[...])
    v_ref[...] = _dot(ckv, wv_ref[...]).astype(BF16)
    kpe = kpe_ref[...].astype(BF16)
    dq = 2 * nope
    for hd in range(n_heads):
        _head_store(k_ref, hd, dq, kn[:, hd * nope:(hd + 1) * nope])
        k_ref[:, hd * dq + nope:(hd + 1) * dq] = kpe


def _mla_expand_call(ckv, kpe_pad, wk, wv, i_odd, n_heads, nope, tm=256):
    t, kv_lora = ckv.shape
    tm = min(tm, t)
    dq = 2 * nope
    whole = lambda a: pl.BlockSpec((None,) + a.shape[1:], lambda i: (i_odd,) + (0,) * (a.ndim - 1))
    return pl.pallas_call(
        functools.partial(_mla_expand_kernel, n_heads=n_heads, nope=nope),
        out_shape=[jax.ShapeDtypeStruct((t, n_heads * dq), BF16),
                   jax.ShapeDtypeStruct((t, n_heads * nope), BF16)],
        grid=(t // tm,),
        in_specs=[pl.BlockSpec((tm, kv_lora), lambda i: (i, 0)),
                  pl.BlockSpec((tm, kpe_pad.shape[1]), lambda i: (i, 0)),
                  whole(wk), whole(wv)],
        out_specs=[pl.BlockSpec((tm, n_heads * dq), lambda i: (i, 0)),
                   pl.BlockSpec((tm, n_heads * nope), lambda i: (i, 0))],
        compiler_params=_params("parallel"),
        name="mla_expand",
    )(ckv, kpe_pad, wk, wv)


def _rotate_half_columns(w, rope):
    q = rope // 4
    parts = [w[..., i * q:(i + 1) * q] for i in range(4)]
    return jnp.concatenate([-parts[1], parts[0], -parts[3], parts[2]], axis=-1)


def _rope_table(n_tokens, rope):
    axis = rope // 2
    t = jnp.arange(n_tokens)
    inv = ROPE_BASE ** (-jnp.arange(0, axis, 2, dtype=F32) / axis)
    ang_r = (t // GRID_W).astype(F32)[:, None] * inv[None, :]
    ang_c = (t % GRID_W).astype(F32)[:, None] * inv[None, :]
    ang = jnp.concatenate([ang_r, ang_r, ang_c, ang_c], axis=-1)
    return jnp.concatenate([jnp.cos(ang), jnp.sin(ang)], axis=-1)


def _identity_rope_table(n_tokens, rope):
    return jnp.concatenate([jnp.ones((n_tokens, rope), F32), jnp.zeros((n_tokens, rope), F32)], axis=-1)


def kernel(x_prompt, x_sample, cache_na_k, cache_na_v, cache_mla_ckv, cache_mla_kpe, c, c_ctx,
           w_mod, b_mod, ln_g, ln_b, ffn_w1, ffn_w3, ffn_w2,
           na_w_in, mix0_w_out, na_rpb, pool_w, pool_scale,
           mla_w_down, mla_q_norm, mla_w_uq, mla_kv_norm, mla_w_ukv, mla_w_out):
    depth, d, _ = w_mod.shape
    alpha = (2 * depth) ** 0.25
    batch, seq, _ = x_prompt.shape
    dec_batch, dec_seq, _ = x_sample.shape
    na_heads, na_dh = cache_na_k.shape[-2:]
    na_width = na_heads * na_dh
    pool_groups, pool_ch = pool_w.shape[1:3]
    assert pool_groups * pool_ch == na_width and pool_ch % 128 == 0
    q_lora, kv_lora = mla_q_norm.shape[-1], mla_kv_norm.shape[-1]
    rope = cache_mla_kpe.shape[-1]
    uq_w, ukv_w, out_rows = mla_w_uq.shape[-1], mla_w_ukv.shape[-1], mla_w_out.shape[1]
    mla_heads = (uq_w - ukv_w + out_rows) // rope
    nope = uq_w // mla_heads - rope
    v_dim = out_rows // mla_heads
    assert nope == v_dim and 2 * rope == nope and nope % 128 == 0
    mla_dims = (mla_heads, q_lora, kv_lora, nope, rope, float((nope + rope) ** -0.5))

    w1, w3, w2 = ffn_w1.astype(BF16), ffn_w3.astype(BF16), ffn_w2.astype(BF16)
    w_in = na_w_in.astype(BF16)
    w_out0 = mix0_w_out.astype(BF16)
    w_pool = pool_w.astype(BF16)
    p_scale = pool_scale.reshape(pool_scale.shape[0], 1, -1)
    kpe_cols = mla_w_down[..., q_lora + kv_lora:]
    wd = jnp.concatenate([mla_w_down, _rotate_half_columns(kpe_cols, rope)], axis=-1).astype(BF16)
    n_odd = mla_w_uq.shape[0]
    uq = mla_w_uq.reshape(n_odd, q_lora, mla_heads, nope + rope)
    wq = jnp.concatenate([uq, _rotate_half_columns(uq[..., nope:], rope)], axis=-1)
    wq = wq.reshape(n_odd, q_lora, mla_heads * 2 * nope).astype(BF16)
    ukv = mla_w_ukv.reshape(n_odd, kv_lora, mla_heads, nope + v_dim)
    wk = ukv[..., :nope].reshape(n_odd, kv_lora, mla_heads * nope).astype(BF16)
    wv = ukv[..., nope:].reshape(n_odd, kv_lora, mla_heads * v_dim).astype(BF16)
    w_out1 = mla_w_out.astype(BF16)
    qn = mla_q_norm.reshape(n_odd, 1, q_lora)
    kvn = mla_kv_norm.reshape(n_odd, 1, kv_lora)

    n_cond = 1 + dec_batch
    cond = jnp.concatenate([c_ctx[None, :], c, jnp.zeros((-n_cond % 16, d), F32)], axis=0)
    mod = _mod_call(cond, w_mod, b_mod).reshape(depth, cond.shape[0], 9, d)

    rope_tab = _rope_table(dec_seq, rope)
    ident_tab = _identity_rope_table(min(256, seq), rope)

    def trunk(x3, mod_rows, per_cond, caches):
        bsz, l, _ = x3.shape
        x = x3.reshape(bsz * l, d)
        states = []
        for layer in range(depth):
            m = mod_rows[layer]
            g = lambda s: ln_g[layer, s].reshape(1, d)
            bb = lambda s: ln_b[layer, s].reshape(1, d)
            x = _ffn_call(x, m, per_cond, w1, w3, w2, layer, 0, g(0), bb(0), 0, alpha)
            i = layer // 2
            if layer % 2 == 0:
                outs = _na_proj_call(x, m, per_cond, w_in, i, na_width, float(na_dh ** -0.5),
                                     caches is None)
                proj = outs[0].reshape(bsz, l, -1)
                if caches is None:
                    states.append((outs[1], outs[2]))
                    a = _attn_call(proj, 0, na_dh, [(proj, na_heads, proj, 2 * na_heads)],
                                   na_heads, na_dh, 256, "na_dense_attn")
                else:
                    kc = caches[0][:, i].reshape(bsz, -1, na_width).astype(BF16)
                    vc = caches[1][:, i].reshape(bsz, -1, na_width).astype(BF16)
                    bias = _na_bias_table(na_rpb[i], l // GRID_W)
                    a = _na_call(proj, kc, vc, bias, na_heads, na_dh, (na_rpb.shape[2] + 1) // 2)
                pooled = _pool_call(proj, 3 * na_width // pool_ch, w_pool, p_scale, i)
                x = _out_call(x, m, per_cond,
                              [(a.reshape(bsz * l, na_width), w_out0, (None, na_width, d), (i, 0, 0)),
                               (pooled.reshape(bsz * l, na_width), w_out0, (None, na_width, d), (i, 1, 0))],
                              g(1), bb(1), alpha)
            else:
                tab = ident_tab if caches is None else rope_tab
                outs = _mla_proj_call(x, m, per_cond, wd, qn, kvn, wq, wk, wv, tab, i, mla_dims,
                                      caches is None)
                q = outs[0].reshape(bsz, l, -1)
                k = outs[1].reshape(bsz, l, -1)
                v = outs[2].reshape(bsz, l, -1)
                segments = [(k, 0, v, 0)]
                if caches is None:
                    states.append((outs[3], outs[4]))
                else:
                    ckv_c = caches[2][:, i].reshape(-1, kv_lora)
                    kpe_c = caches[3][:, i].reshape(-1, rope)
                    kpe_c = jnp.concatenate([kpe_c, jnp.zeros_like(kpe_c)], axis=-1)
                    k_c, v_c = _mla_expand_call(ckv_c, kpe_c, wk, wv, i, mla_heads, nope)
                    segments.append((k_c.reshape(bsz, -1, k.shape[-1]), 0,
                                     v_c.reshape(bsz, -1, v.shape[-1]), 0))
                o = _attn_call(q, 0, 2 * nope, segments, mla_heads, v_dim, 256, "mla_attn")
                x = _out_call(x, m, per_cond,
                              [(o.reshape(bsz * l, -1), w_out1, (None,) + w_out1.shape[1:], (i, 0, 0))],
                              g(1), bb(1), alpha)
            x = _ffn_call(x, m, per_cond, w1, w3, w2, layer, 1, g(2), bb(2), 2, alpha)
        return x.reshape(bsz, l, d), states

    y_prompt, st = trunk(x_prompt, mod[:, 0:1], batch * seq, None)
    y_sample, _ = trunk(x_sample, mod[:, 1:n_cond], dec_seq,
                        (cache_na_k, cache_na_v, cache_mla_ckv, cache_mla_kpe))
    even = [s for layer, s in enumerate(st) if layer % 2 == 0]
    odd = [s for layer, s in enumerate(st) if layer % 2 == 1]
    new_na_k = jnp.stack([s[0].reshape(batch, seq, na_heads, na_dh) for s in even], axis=1)
    new_na_v = jnp.stack([s[1].reshape(batch, seq, na_heads, na_dh) for s in even], axis=1)
    new_mla_ckv = jnp.stack([s[0].reshape(batch, seq, kv_lora) for s in odd], axis=1)
    new_mla_kpe = jnp.stack([s[1].reshape(batch, seq, rope) for s in odd], axis=1)
    return (y_prompt, y_sample, new_na_k, new_na_v, new_mla_ckv, new_mla_kpe)
```

```python
import functools
import math

import numpy as np
import jax
import jax.numpy as jnp
from jax import lax
from jax.experimental import pallas as pl
from jax.experimental.pallas import tpu as pltpu

GRID_W = 64
LN_EPS = 1e-5
RMS_EPS = 1e-6
POOL_WINDOWS = (2, 4, 8, 16)
ROPE_BASE = 10000.0
MASKED = -1e30
LOG2E = math.log2(math.e)
ATTN_ROWS = 256
NA_GROUP_ROWS = 4
NA_WINDOW_ROWS = 12
V7X_VMEM_LIMIT_BYTES = 56 * 1024 * 1024

F32 = jnp.float32
BF16 = jnp.bfloat16
_NT = (((1,), (1,)), ((), ()))


def _params(*semantics):
    return pltpu.CompilerParams(dimension_semantics=semantics,
                                vmem_limit_bytes=V7X_VMEM_LIMIT_BYTES)


def _dot(a, b):
    return jnp.dot(a, b, preferred_element_type=F32)


def _post_norm(x, update, g, b, alpha):
    z = alpha * x + update
    mu = jnp.mean(z, axis=-1, keepdims=True)
    zc = z - mu
    var = jnp.mean(zc * zc, axis=-1, keepdims=True)
    return zc * lax.rsqrt(var + LN_EPS) * g + b


def _modulated(x, m, j):
    return x * (1.0 + m[3 * j + 1:3 * j + 2]) + m[3 * j:3 * j + 1]


def _mod_kernel(c_ref, w_ref, b_ref, o_ref):
    c = c_ref[...]
    h = (c * jax.nn.sigmoid(c)).astype(BF16)
    o_ref[...] = _dot(h, w_ref[...].astype(BF16)) + b_ref[...]


def _mod_call(cond, w_mod, b_mod, tn=1024):
    depth, d, n = w_mod.shape
    r = cond.shape[0]
    return pl.pallas_call(
        _mod_kernel,
        out_shape=jax.ShapeDtypeStruct((depth, r, n), F32),
        grid=(depth, n // tn),
        in_specs=[pl.BlockSpec((r, d), lambda l, j: (0, 0)),
                  pl.BlockSpec((None, d, tn), lambda l, j: (l, 0, j)),
                  pl.BlockSpec((None, 1, tn), lambda l, j: (l, 0, j))],
        out_specs=pl.BlockSpec((None, r, tn), lambda l, j: (l, 0, j)),
        compiler_params=_params("parallel", "parallel"),
        name="mod_proj",
    )(cond, w_mod, b_mod.reshape(depth, 1, n))


def _ffn_kernel(x_ref, mod_ref, w1_ref, w3_ref, w2_ref, g_ref, b_ref, o_ref, h_ref, *, j, alpha):
    f = pl.program_id(1)

    @pl.when(f == 0)
    def _():
        h_ref[...] = _modulated(x_ref[...], mod_ref[...], j).astype(BF16)
        o_ref[...] = jnp.zeros_like(o_ref)

    h = h_ref[...]
    a = _dot(h, w1_ref[...])
    b = _dot(h, w3_ref[...])
    act = (a * jax.nn.sigmoid(a) * b).astype(BF16)
    o_ref[...] += _dot(act, w2_ref[...])

    @pl.when(f == pl.num_programs(1) - 1)
    def _():
        gate = mod_ref[...][3 * j + 2:3 * j + 3]
        o_ref[...] = _post_norm(x_ref[...], 0.5 * gate * o_ref[...], g_ref[...], b_ref[...], alpha)


def _ffn_call(x, mod, tokens_per_cond, w1, w3, w2, l, s, ln_g, ln_b, j, alpha, tm=512, tf=512):
    t, d = x.shape
    ff = w1.shape[-1]
    tm = min(tm, tokens_per_cond)
    tf = min(tf, ff)
    assert tokens_per_cond % tm == 0 and ff % tf == 0
    per = tokens_per_cond // tm
    return pl.pallas_call(
        functools.partial(_ffn_kernel, j=j, alpha=alpha),
        out_shape=jax.ShapeDtypeStruct((t, d), F32),
        grid=(t // tm, ff // tf),
        in_specs=[pl.BlockSpec((tm, d), lambda i, f: (i, 0)),
                  pl.BlockSpec((None, 9, d), lambda i, f: (i // per, 0, 0)),
                  pl.BlockSpec((None, None, d, tf), lambda i, f: (l, s, 0, f)),
                  pl.BlockSpec((None, None, d, tf), lambda i, f: (l, s, 0, f)),
                  pl.BlockSpec((None, None, tf, d), lambda i, f: (l, s, f, 0)),
                  pl.BlockSpec((1, d), lambda i, f: (0, 0)),
                  pl.BlockSpec((1, d), lambda i, f: (0, 0))],
        out_specs=pl.BlockSpec((tm, d), lambda i, f: (i, 0)),
        scratch_shapes=[pltpu.VMEM((tm, d), BF16)],
        compiler_params=_params("parallel", "arbitrary"),
        name="ffn",
    )(x, mod, w1, w3, w2, ln_g, ln_b)


def _na_proj_kernel(x_ref, mod_ref, w_ref, o_ref, *rest, q_scale, with_kv):
    h_ref = rest[-1]
    n = pl.program_id(1)

    @pl.when(n == 0)
    def _():
        h_ref[...] = _modulated(x_ref[...], mod_ref[...], 1).astype(BF16)

    y = _dot(h_ref[...], w_ref[...])
    o_ref[...] = jnp.where(n == 0, y * q_scale, y).astype(BF16)
    if with_kv:
        k_ref, v_ref = rest[0], rest[1]

        @pl.when(n == 1)
        def _():
            k_ref[...] = y

        @pl.when(n == 2)
        def _():
            v_ref[...] = y


def _na_proj_call(x, mod, tokens_per_cond, w_in, i_even, na_width, q_scale, with_kv, tm=512):
    t, d = x.shape
    n_in = w_in.shape[-1]
    tn = na_width
    assert n_in % tn == 0, "q, k, v and pooled widths must share one tile width"
    tm = min(tm, tokens_per_cond)
    per = tokens_per_cond // tm
    out_shape = [jax.ShapeDtypeStruct((t, n_in), BF16)]
    out_specs = [pl.BlockSpec((tm, tn), lambda i, n: (i, n))]
    if with_kv:
        out_shape += [jax.ShapeDtypeStruct((t, tn), F32)] * 2
        out_specs += [pl.BlockSpec((tm, tn), lambda i, n: (i, 0))] * 2
    return pl.pallas_call(
        functools.partial(_na_proj_kernel, q_scale=q_scale, with_kv=with_kv),
        out_shape=out_shape,
        grid=(t // tm, n_in // tn),
        in_specs=[pl.BlockSpec((tm, d), lambda i, n: (i, 0)),
                  pl.BlockSpec((None, 9, d), lambda i, n: (i // per, 0, 0)),
                  pl.BlockSpec((None, d, tn), lambda i, n: (i_even, 0, n))],
        out_specs=out_specs,
        scratch_shapes=[pltpu.VMEM((tm, d), BF16)],
        compiler_params=_params("parallel", "arbitrary"),
        name="na_proj",
    )(x, mod, w_in)


def _attn_rows(q, kv_refs, chunk, dv):
    rows = q.shape[0]
    den_in_v = kv_refs[1].shape[-1] == 2 * dv
    m = jnp.full((rows, 1), MASKED, F32)
    acc = jnp.zeros((rows, kv_refs[1].shape[-1]), F32)
    den = jnp.zeros((rows, 1), F32)
    for k_ref, v_ref in zip(kv_refs[0::2], kv_refs[1::2]):
        lk = k_ref.shape[0]
        step = min(chunk, lk)
        for c0 in range(0, lk, step):
            s = lax.dot_general(q, k_ref[c0:c0 + step, :], _NT, preferred_element_type=F32)
            m_new = jnp.maximum(m, jnp.max(s, axis=-1, keepdims=True))
            rescale = jnp.exp2(m - m_new)
            p = jnp.exp2(s - m_new)
            if not den_in_v:
                den = rescale * den + jnp.sum(p, axis=-1, keepdims=True)
            acc = rescale * acc + _dot(p.astype(BF16), v_ref[c0:c0 + step, :])
            m = m_new
    return acc[:, :dv] / (acc[:, dv:] if den_in_v else den)


def _attn_kernel(*refs, chunk, dv, rows):
    q_ref, o_ref = refs[0], refs[-1]
    for r0 in range(0, q_ref.shape[0], rows):
        o = _attn_rows(q_ref[r0:r0 + rows, :], refs[1:-1], chunk, dv)
        o_ref[r0:r0 + rows, :] = o.astype(o_ref.dtype)


def _attn_call(q, q_col0, dq, segments, n_heads, dv, v_width, tq, chunk, name):
    b, l, _ = q.shape
    tq = min(tq, l)
    in_specs = [pl.BlockSpec((None, tq, dq), lambda bi, h, qi: (bi, qi, q_col0 + h))]
    args = [q]
    for k, kc, v, vc in segments:
        lk = k.shape[1]
        assert lk % min(chunk, lk) == 0
        in_specs.append(pl.BlockSpec((None, lk, dq), lambda bi, h, qi, kc=kc: (bi, 0, kc + h)))
        in_specs.append(pl.BlockSpec((None, lk, v_width), lambda bi, h, qi, vc=vc: (bi, 0, vc + h)))
        args += [k, v]
    return pl.pallas_call(
        functools.partial(_attn_kernel, chunk=chunk, dv=dv, rows=min(ATTN_ROWS, tq)),
        out_shape=jax.ShapeDtypeStruct((b, l, n_heads * dv), BF16),
        grid=(b, n_heads, l // tq),
        in_specs=in_specs,
        out_specs=pl.BlockSpec((None, tq, dv), lambda bi, h, qi: (bi, qi, h)),
        compiler_params=_params("parallel", "parallel", "arbitrary"),
        name=name,
    )(*args)


def _na_kernel(q_ref, k_ref, v_ref, kc_ref, vc_ref, bias_ref, o_ref, *, n_rows, half_rows):
    g = pl.program_id(2)
    base_row = jnp.clip(g * NA_GROUP_ROWS - half_rows, 0, n_rows - NA_WINDOW_ROWS)
    base = pl.multiple_of(base_row * GRID_W, GRID_W)
    win = NA_WINDOW_ROWS * GRID_W
    q = q_ref[...]
    kw = k_ref[pl.ds(base, win), :]
    vw = v_ref[pl.ds(base, win), :]
    s_loc = lax.dot_general(q, kw, _NT, preferred_element_type=F32) + bias_ref[...]
    s_ctx = lax.dot_general(q, kc_ref[...], _NT, preferred_element_type=F32)
    m = jnp.maximum(jnp.max(s_loc, axis=-1, keepdims=True), jnp.max(s_ctx, axis=-1, keepdims=True))
    p_loc = jnp.exp2(s_loc - m)
    p_ctx = jnp.exp2(s_ctx - m)
    den = jnp.sum(p_loc, axis=-1, keepdims=True) + jnp.sum(p_ctx, axis=-1, keepdims=True)
    o = _dot(p_loc.astype(BF16), vw) + _dot(p_ctx.astype(BF16), vc_ref[...])
    o_ref[...] = (o / den).astype(o_ref.dtype)


def _na_group_rows(g, n_rows, na_rows):
    kr = min(na_rows, n_rows)
    base = int(np.clip(g * NA_GROUP_ROWS - kr // 2, 0, n_rows - NA_WINDOW_ROWS))
    rq = (g * NA_GROUP_ROWS + np.arange(NA_GROUP_ROWS))[:, None]
    rk = (base + np.arange(NA_WINDOW_ROWS))[None, :]
    rs = np.clip(rq - kr // 2, 0, n_rows - kr)
    ok = (rk >= rs) & (rk < rs + kr)
    assert (ok.sum(axis=1) == kr).all(), "key window misses part of a neighbourhood"
    return np.where(ok, rk - rq + na_rows - 1, 0), ok


def _na_bias_table(rpb, n_rows):
    n_heads, nr2, nc2 = rpb.shape
    na_rows, na_cols = (nr2 + 1) // 2, (nc2 + 1) // 2
    n_groups = n_rows // NA_GROUP_ROWS
    per_group = [_na_group_rows(g, n_rows, na_rows) for g in range(n_groups)]
    for g in range(2, n_groups - 1):
        assert all((a == b).all() for a, b in zip(per_group[g], per_group[1])), "interior groups differ"
    idx_r, row_ok = (np.stack([per_group[g][i] for g in (0, 1, n_groups - 1)]) for i in range(2))
    w = GRID_W
    e = jnp.pad(rpb, ((0, 0), (0, 0), (w - na_cols, 2 * w - (w - na_cols) - nc2)))
    band = jnp.tile(e, (1, 1, w))[..., :w * (2 * w - 1)].reshape(n_heads, nr2, w, 2 * w - 1)[..., w - 1:]
    cq, ck = np.arange(w)[:, None], np.arange(w)[None, :]
    cs = np.clip(cq - na_cols // 2, 0, w - na_cols)
    col_ok = (ck >= cs) & (ck < cs + na_cols)
    blocks = jnp.take(band, jnp.asarray(idx_r.reshape(-1)), axis=1)
    blocks = blocks.reshape((n_heads,) + idx_r.shape + (w, w))
    ok = row_ok[:, :, :, None, None] & col_ok[None, None, None]
    table = jnp.where(ok[None], blocks * LOG2E, MASKED).transpose(0, 1, 2, 4, 3, 5)
    return table.reshape(n_heads, 3, NA_GROUP_ROWS * w, NA_WINDOW_ROWS * w).astype(F32)


def _na_call(proj, kc, vc, bias, n_heads, dh, na_rows):
    b, l, _ = proj.shape
    n_rows = l // GRID_W
    assert n_rows % NA_GROUP_ROWS == 0 and n_rows >= NA_WINDOW_ROWS + NA_GROUP_ROWS
    assert NA_GROUP_ROWS - 1 + min(na_rows, n_rows) <= NA_WINDOW_ROWS
    n_groups = n_rows // NA_GROUP_ROWS
    tq = NA_GROUP_ROWS * GRID_W
    win = NA_WINDOW_ROWS * GRID_W
    lc = kc.shape[1]

    def variant(g):
        return jnp.where(g == 0, 0, jnp.where(g == n_groups - 1, 2, 1))

    return pl.pallas_call(
        functools.partial(_na_kernel, n_rows=n_rows, half_rows=min(na_rows, n_rows) // 2),
        out_shape=jax.ShapeDtypeStruct((b, l, n_heads * dh), BF16),
        grid=(b, n_heads, n_groups),
        in_specs=[pl.BlockSpec((None, tq, dh), lambda bi, h, g: (bi, g, h)),
                  pl.BlockSpec((None, l, dh), lambda bi, h, g: (bi, 0, n_heads + h)),
                  pl.BlockSpec((None, l, dh), lambda bi, h, g: (bi, 0, 2 * n_heads + h)),
                  pl.BlockSpec((None, lc, dh), lambda bi, h, g: (bi, 0, h)),
                  pl.BlockSpec((None, lc, dh), lambda bi, h, g: (bi, 0, h)),
                  pl.BlockSpec((None, None, tq, win), lambda bi, h, g: (h, variant(g), 0, 0))],
        out_specs=pl.BlockSpec((None, tq, dh), lambda bi, h, g: (bi, g, h)),
        compiler_params=_params("parallel", "parallel", "arbitrary"),
        name="na_attn",
    )(proj, proj, proj, kc, vc, bias)


def _pool_kernel(u_ref, w_ref, s_ref, o_ref, *, chunk, window):
    l = u_ref.shape[0]
    half = lax.shift_left(jnp.int32(1), pl.program_id(1).astype(jnp.int32))
    w = w_ref[...]
    scale = s_ref[...]

    def body(c, carry):
        t0 = pl.multiple_of(c * chunk, chunk)
        s0 = pl.multiple_of(jnp.clip(t0 - (window - chunk) // 2, 0, l - window), 16)
        t = t0 + lax.broadcasted_iota(jnp.int32, (chunk, 1), 0)
        lo = jnp.maximum(t - half, 0)
        hi = jnp.minimum(t + half, l)
        pos = s0 + lax.broadcasted_iota(jnp.int32, (1, window), 1)
        band = jnp.where((pos >= lo) & (pos < hi), 1.0, 0.0).astype(BF16)
        sums = _dot(band, u_ref[pl.ds(s0, window), :])
        mean = sums / (hi - lo).astype(F32)
        d = (mean - u_ref[pl.ds(t0, chunk), :].astype(F32)).astype(BF16)
        o_ref[pl.ds(t0, chunk), :] = (_dot(d, w) * scale).astype(o_ref.dtype)
        return carry

    lax.fori_loop(0, l // chunk, body, 0)


def _pool_call(proj, u_col0, w_pool, pool_scale, i_even):
    b, l, _ = proj.shape
    n_groups, ch, _ = w_pool.shape[1:]
    assert n_groups == len(POOL_WINDOWS) and all(w == 2 << g for g, w in enumerate(POOL_WINDOWS))
    chunk = min(256, l)
    window = min(2 * chunk, l)
    assert l % chunk == 0 and (window == l or window - chunk >= max(POOL_WINDOWS))
    return pl.pallas_call(
        functools.partial(_pool_kernel, chunk=chunk, window=window),
        out_shape=jax.ShapeDtypeStruct((b, l, n_groups * ch), BF16),
        grid=(b, n_groups),
        in_specs=[pl.BlockSpec((None, l, ch), lambda bi, g: (bi, 0, u_col0 + g)),
                  pl.BlockSpec((None, None, ch, ch), lambda bi, g: (i_even, g, 0, 0)),
                  pl.BlockSpec((None, 1, ch), lambda bi, g: (i_even, 0, g))],
        out_specs=pl.BlockSpec((None, l, ch), lambda bi, g: (bi, 0, g)),
        compiler_params=_params("parallel", "parallel"),
        name="pool",
    )(proj, w_pool, pool_scale)


def _out_kernel(*refs, n_lhs, alpha):
    x_ref, mod_ref = refs[0], refs[1]
    g_ref, b_ref, o_ref = refs[-3], refs[-2], refs[-1]
    y = functools.reduce(jnp.add, [_dot(refs[2 + 2 * i][...], refs[3 + 2 * i][...])
                                   for i in range(n_lhs)])
    gate = mod_ref[...][5:6]
    o_ref[...] = _post_norm(x_ref[...], gate * y, g_ref[...], b_ref[...], alpha)


def _out_call(x, mod, tokens_per_cond, lhs_w, ln_g, ln_b, alpha, tm=256):
    t, d = x.shape
    tm = min(tm, tokens_per_cond)
    per = tokens_per_cond // tm
    in_specs = [pl.BlockSpec((tm, d), lambda i: (i, 0)),
                pl.BlockSpec((None, 9, d), lambda i: (i // per, 0, 0))]
    args = [x, mod]
    for lhs, w, w_block, w_index in lhs_w:
        in_specs.append(pl.BlockSpec((tm, lhs.shape[1]), lambda i: (i, 0)))
        in_specs.append(pl.BlockSpec(w_block, lambda i, w_index=w_index: w_index))
        args += [lhs, w]
    in_specs += [pl.BlockSpec((1, d), lambda i: (0, 0))] * 2
    args += [ln_g, ln_b]
    return pl.pallas_call(
        functools.partial(_out_kernel, n_lhs=len(lhs_w), alpha=alpha),
        out_shape=jax.ShapeDtypeStruct((t, d), F32),
        grid=(t // tm,),
        in_specs=in_specs,
        out_specs=pl.BlockSpec((tm, d), lambda i: (i, 0)),
        compiler_params=_params("parallel"),
        name="out_proj",
    )(*args)


def _rms(x, g):
    return x * lax.rsqrt(jnp.mean(x * x, axis=-1, keepdims=True) + RMS_EPS) * g


def _rotary_pair(t, tab):
    u = t * tab
    return u + pltpu.roll(u, u.shape[-1] // 2, 1)


def _rope_half_mask(shape):
    return lax.broadcasted_iota(jnp.int32, shape, 1) < shape[-1] // 2


def _head_store(ref, h, width, value):
    ref[:, h * width:h * width + value.shape[-1]] = value.astype(ref.dtype)


def _store_values_with_ones(v_ref, v, n_heads, dv):
    ones = jnp.ones((v.shape[0], dv), v_ref.dtype)
    for hd in range(n_heads):
        _head_store(v_ref, hd, 2 * dv, v[:, hd * dv:(hd + 1) * dv])
        v_ref[:, (2 * hd + 1) * dv:(2 * hd + 2) * dv] = ones


def _mla_proj_kernel(x_ref, mod_ref, wd_ref, qn_ref, kvn_ref, wq_ref, wk_ref, wv_ref, tab_ref,
                     q_ref, k_ref, v_ref, *state_refs, n_heads, q_lora, kv_lora, nope, q_scale):
    h = _modulated(x_ref[...], mod_ref[...], 1).astype(BF16)
    down = _dot(h, wd_ref[...])
    cq = _rms(down[:, :q_lora], qn_ref[...])
    ckv = _rms(down[:, q_lora:q_lora + kv_lora], kvn_ref[...])
    pe = down[:, q_lora + kv_lora:]
    tab = tab_ref[...]
    if state_refs:
        state_refs[0][...] = ckv
        state_refs[1][...] = pe[:, :pe.shape[-1] // 2]
    kpe = jnp.where(_rope_half_mask(pe.shape), _rotary_pair(pe, tab), 0.0)
    q = _dot(cq.astype(BF16), wq_ref[...]) * q_scale
    ckv_b = ckv.astype(BF16)
    kn = _dot(ckv_b, wk_ref[...])
    _store_values_with_ones(v_ref, _dot(ckv_b, wv_ref[...]), n_heads, nope)
    dq = 2 * nope
    for hd in range(n_heads):
        _head_store(q_ref, hd, dq, q[:, hd * dq:hd * dq + nope])
        q_ref[:, hd * dq + nope:(hd + 1) * dq] = _rotary_pair(
            q[:, hd * dq + nope:(hd + 1) * dq], tab).astype(BF16)
        _head_store(k_ref, hd, dq, kn[:, hd * nope:(hd + 1) * nope])
        k_ref[:, hd * dq + nope:(hd + 1) * dq] = kpe.astype(BF16)


def _mla_proj_call(x, mod, tokens_per_cond, wd, qn, kvn, wq, wk, wv, tab, i_odd, dims, with_state,
                   tm=256):
    n_heads, q_lora, kv_lora, nope, rope, q_scale = dims
    t, d = x.shape
    tm = min(tm, tokens_per_cond, tab.shape[0])
    per = tokens_per_cond // tm
    tab_blocks = tab.shape[0] // tm
    dq = 2 * nope
    out_shape = [jax.ShapeDtypeStruct((t, n_heads * dq), BF16),
                 jax.ShapeDtypeStruct((t, n_heads * dq), BF16),
                 jax.ShapeDtypeStruct((t, n_heads * dq), BF16)]
    out_specs = [pl.BlockSpec((tm, n_heads * dq), lambda i: (i, 0)),
                 pl.BlockSpec((tm, n_heads * dq), lambda i: (i, 0)),
                 pl.BlockSpec((tm, n_heads * dq), lambda i: (i, 0))]
    if with_state:
        out_shape += [jax.ShapeDtypeStruct((t, kv_lora), F32), jax.ShapeDtypeStruct((t, rope), F32)]
        out_specs += [pl.BlockSpec((tm, kv_lora), lambda i: (i, 0)),
                      pl.BlockSpec((tm, rope), lambda i: (i, 0))]
    whole = lambda a: pl.BlockSpec((None,) + a.shape[1:], lambda i: (i_odd,) + (0,) * (a.ndim - 1))
    return pl.pallas_call(
        functools.partial(_mla_proj_kernel, n_heads=n_heads, q_lora=q_lora, kv_lora=kv_lora,
                          nope=nope, q_scale=q_scale),
        out_shape=out_shape,
        grid=(t // tm,),
        in_specs=[pl.BlockSpec((tm, d), lambda i: (i, 0)),
                  pl.BlockSpec((None, 9, d), lambda i: (i // per, 0, 0)),
                  whole(wd), whole(qn), whole(kvn), whole(wq), whole(wk), whole(wv),
                  pl.BlockSpec((tm, tab.shape[1]), lambda i: (i % tab_blocks, 0))],
        out_specs=out_specs,
        compiler_params=_params("parallel"),
        name="mla_proj",
    )(x, mod, wd, qn, kvn, wq, wk, wv, tab)


def _mla_expand_kernel(ckv_ref, kpe_ref, wk_ref, wv_ref, k_ref, v_ref, *, n_heads, nope):
    ckv = ckv_ref[...].astype(BF16)
    kn = _dot(ckv, wk_ref[...])
    _store_values_with_ones(v_ref, _dot(ckv, wv_ref[...]), n_heads, nope)
    kpe = kpe_ref[...].astype(BF16)
    dq = 2 * nope
    for hd in range(n_heads):
        _head_store(k_ref, hd, dq, kn[:, hd * nope:(hd + 1) * nope])
        k_ref[:, hd * dq + nope:(hd + 1) * dq] = kpe


def _mla_expand_call(ckv, kpe_pad, wk, wv, i_odd, n_heads, nope, tm=256):
    t, kv_lora = ckv.shape
    tm = min(tm, t)
    dq = 2 * nope
    whole = lambda a: pl.BlockSpec((None,) + a.shape[1:], lambda i: (i_odd,) + (0,) * (a.ndim - 1))
    return pl.pallas_call(
        functools.partial(_mla_expand_kernel, n_heads=n_heads, nope=nope),
        out_shape=[jax.ShapeDtypeStruct((t, n_heads * dq), BF16),
                   jax.ShapeDtypeStruct((t, n_heads * dq), BF16)],
        grid=(t // tm,),
        in_specs=[pl.BlockSpec((tm, kv_lora), lambda i: (i, 0)),
                  pl.BlockSpec((tm, kpe_pad.shape[1]), lambda i: (i, 0)),
                  whole(wk), whole(wv)],
        out_specs=[pl.BlockSpec((tm, n_heads * dq), lambda i: (i, 0)),
                   pl.BlockSpec((tm, n_heads * dq), lambda i: (i, 0))],
        compiler_params=_params("parallel"),
        name="mla_expand",
    )(ckv, kpe_pad, wk, wv)


def _rotate_half_columns(w, rope):
    q = rope // 4
    parts = [w[..., i * q:(i + 1) * q] for i in range(4)]
    return jnp.concatenate([-parts[1], parts[0], -parts[3], parts[2]], axis=-1)


def _rope_table(n_tokens, rope):
    axis = rope // 2
    t = jnp.arange(n_tokens)
    inv = ROPE_BASE ** (-jnp.arange(0, axis, 2, dtype=F32) / axis)
    ang_r = (t // GRID_W).astype(F32)[:, None] * inv[None, :]
    ang_c = (t % GRID_W).astype(F32)[:, None] * inv[None, :]
    ang = jnp.concatenate([ang_r, ang_r, ang_c, ang_c], axis=-1)
    return jnp.concatenate([jnp.cos(ang), jnp.sin(ang)], axis=-1)


def _identity_rope_table(n_tokens, rope):
    return jnp.concatenate([jnp.ones((n_tokens, rope), F32), jnp.zeros((n_tokens, rope), F32)], axis=-1)


def kernel(x_prompt, x_sample, cache_na_k, cache_na_v, cache_mla_ckv, cache_mla_kpe, c, c_ctx,
           w_mod, b_mod, ln_g, ln_b, ffn_w1, ffn_w3, ffn_w2,
           na_w_in, mix0_w_out, na_rpb, pool_w, pool_scale,
           mla_w_down, mla_q_norm, mla_w_uq, mla_kv_norm, mla_w_ukv, mla_w_out):
    depth, d, _ = w_mod.shape
    alpha = (2 * depth) ** 0.25
    batch, seq, _ = x_prompt.shape
    dec_batch, dec_seq, _ = x_sample.shape
    na_heads, na_dh = cache_na_k.shape[-2:]
    na_width = na_heads * na_dh
    pool_groups, pool_ch = pool_w.shape[1:3]
    assert pool_groups * pool_ch == na_width and pool_ch % 128 == 0
    q_lora, kv_lora = mla_q_norm.shape[-1], mla_kv_norm.shape[-1]
    rope = cache_mla_kpe.shape[-1]
    uq_w, ukv_w, out_rows = mla_w_uq.shape[-1], mla_w_ukv.shape[-1], mla_w_out.shape[1]
    mla_heads = (uq_w - ukv_w + out_rows) // rope
    nope = uq_w // mla_heads - rope
    v_dim = out_rows // mla_heads
    assert nope == v_dim and 2 * rope == nope and nope % 128 == 0
    mla_dims = (mla_heads, q_lora, kv_lora, nope, rope, float((nope + rope) ** -0.5) * LOG2E)

    w1, w3, w2 = ffn_w1.astype(BF16), ffn_w3.astype(BF16), ffn_w2.astype(BF16)
    w_in = na_w_in.astype(BF16)
    w_out0 = mix0_w_out.astype(BF16)
    w_pool = pool_w.astype(BF16)
    p_scale = pool_scale.reshape(pool_scale.shape[0], 1, -1)
    kpe_cols = mla_w_down[..., q_lora + kv_lora:]
    wd = jnp.concatenate([mla_w_down, _rotate_half_columns(kpe_cols, rope)], axis=-1).astype(BF16)
    n_odd = mla_w_uq.shape[0]
    uq = mla_w_uq.reshape(n_odd, q_lora, mla_heads, nope + rope)
    wq = jnp.concatenate([uq, _rotate_half_columns(uq[..., nope:], rope)], axis=-1)
    wq = wq.reshape(n_odd, q_lora, mla_heads * 2 * nope).astype(BF16)
    ukv = mla_w_ukv.reshape(n_odd, kv_lora, mla_heads, nope + v_dim)
    wk = ukv[..., :nope].reshape(n_odd, kv_lora, mla_heads * nope).astype(BF16)
    wv = ukv[..., nope:].reshape(n_odd, kv_lora, mla_heads * v_dim).astype(BF16)
    w_out1 = mla_w_out.astype(BF16)
    qn = mla_q_norm.reshape(n_odd, 1, q_lora)
    kvn = mla_kv_norm.reshape(n_odd, 1, kv_lora)

    n_cond = 1 + dec_batch
    cond = jnp.concatenate([c_ctx[None, :], c, jnp.zeros((-n_cond % 16, d), F32)], axis=0)
    mod = _mod_call(cond, w_mod, b_mod).reshape(depth, cond.shape[0], 9, d)

    rope_tab = _rope_table(dec_seq, rope)
    ident_tab = _identity_rope_table(min(256, seq), rope)

    def trunk(x3, mod_rows, per_cond, caches):
        bsz, l, _ = x3.shape
        x = x3.reshape(bsz * l, d)
        states = []
        for layer in range(depth):
            m = mod_rows[layer]
            g = lambda s: ln_g[layer, s].reshape(1, d)
            bb = lambda s: ln_b[layer, s].reshape(1, d)
            x = _ffn_call(x, m, per_cond, w1, w3, w2, layer, 0, g(0), bb(0), 0, alpha)
            i = layer // 2
            if layer % 2 == 0:
                outs = _na_proj_call(x, m, per_cond, w_in, i, na_width, float(na_dh ** -0.5) * LOG2E,
                                     caches is None)
                proj = outs[0].reshape(bsz, l, -1)
                if caches is None:
                    states.append((outs[1], outs[2]))
                    a = _attn_call(proj, 0, na_dh, [(proj, na_heads, proj, 2 * na_heads)],
                                   na_heads, na_dh, na_dh, 256, 512, "na_dense_attn")
                else:
                    kc = caches[0][:, i].reshape(bsz, -1, na_width).astype(BF16)
                    vc = caches[1][:, i].reshape(bsz, -1, na_width).astype(BF16)
                    bias = _na_bias_table(na_rpb[i], l // GRID_W)
                    a = _na_call(proj, kc, vc, bias, na_heads, na_dh, (na_rpb.shape[2] + 1) // 2)
                pooled = _pool_call(proj, 3 * na_width // pool_ch, w_pool, p_scale, i)
                x = _out_call(x, m, per_cond,
                              [(a.reshape(bsz * l, na_width), w_out0, (None, na_width, d), (i, 0, 0)),
                               (pooled.reshape(bsz * l, na_width), w_out0, (None, na_width, d), (i, 1, 0))],
                              g(1), bb(1), alpha)
            else:
                tab = ident_tab if caches is None else rope_tab
                outs = _mla_proj_call(x, m, per_cond, wd, qn, kvn, wq, wk, wv, tab, i, mla_dims,
                                      caches is None)
                q = outs[0].reshape(bsz, l, -1)
                k = outs[1].reshape(bsz, l, -1)
                v = outs[2].reshape(bsz, l, -1)
                segments = [(k, 0, v, 0)]
                if caches is None:
                    states.append((outs[3], outs[4]))
                else:
                    ckv_c = caches[2][:, i].reshape(-1, kv_lora)
                    kpe_c = caches[3][:, i].reshape(-1, rope)
                    kpe_c = jnp.concatenate([kpe_c, jnp.zeros_like(kpe_c)], axis=-1)
                    k_c, v_c = _mla_expand_call(ckv_c, kpe_c, wk, wv, i, mla_heads, nope)
                    segments.append((k_c.reshape(bsz, -1, k.shape[-1]), 0,
                                     v_c.reshape(bsz, -1, v.shape[-1]), 0))
                o = _attn_call(q, 0, 2 * nope, segments, mla_heads, v_dim, 2 * v_dim, 1024, 512,
                               "mla_attn")
                x = _out_call(x, m, per_cond,
                              [(o.reshape(bsz * l, -1), w_out1, (None,) + w_out1.shape[1:], (i, 0, 0))],
                              g(1), bb(1), alpha)
            x = _ffn_call(x, m, per_cond, w1, w3, w2, layer, 1, g(2), bb(2), 2, alpha)
        return x.reshape(bsz, l, d), states

    y_prompt, st = trunk(x_prompt, mod[:, 0:1], batch * seq, None)
    y_sample, _ = trunk(x_sample, mod[:, 1:n_cond], dec_seq,
                        (cache_na_k, cache_na_v, cache_mla_ckv, cache_mla_kpe))
    even = [s for layer, s in enumerate(st) if layer % 2 == 0]
    odd = [s for layer, s in enumerate(st) if layer % 2 == 1]
    new_na_k = jnp.stack([s[0].reshape(batch, seq, na_heads, na_dh) for s in even], axis=1)
    new_na_v = jnp.stack([s[1].reshape(batch, seq, na_heads, na_dh) for s in even], axis=1)
    new_mla_ckv = jnp.stack([s[0].reshape(batch, seq, kv_lora) for s in odd], axis=1)
    new_mla_kpe = jnp.stack([s[1].reshape(batch, seq, rope) for s in odd], axis=1)
    return (y_prompt, y_sample, new_na_k, new_na_v, new_mla_ckv, new_mla_kpe)
```

```python
import functools
import math

import numpy as np
import jax
import jax.numpy as jnp
from jax import lax
from jax.experimental import pallas as pl
from jax.experimental.pallas import tpu as pltpu

GRID_W = 64
LN_EPS = 1e-5
RMS_EPS = 1e-6
POOL_WINDOWS = (2, 4, 8, 16)
ROPE_BASE = 10000.0
MASKED = -1e30
LOG2E = math.log2(math.e)
ATTN_ROWS = 256
NA_GROUP_ROWS = 4
NA_WINDOW_ROWS = 12
V7X_VMEM_LIMIT_BYTES = 56 * 1024 * 1024

F32 = jnp.float32
BF16 = jnp.bfloat16
_NT = (((1,), (1,)), ((), ()))


def _params(*semantics):
    return pltpu.CompilerParams(dimension_semantics=semantics,
                                vmem_limit_bytes=V7X_VMEM_LIMIT_BYTES)


def _dot(a, b):
    return jnp.dot(a, b, preferred_element_type=F32)


def _post_norm(x, update, g, b, alpha):
    z = alpha * x + update
    mu = jnp.mean(z, axis=-1, keepdims=True)
    zc = z - mu
    var = jnp.mean(zc * zc, axis=-1, keepdims=True)
    return zc * lax.rsqrt(var + LN_EPS) * g + b


def _modulated(x, m, j):
    return x * (1.0 + m[3 * j + 1:3 * j + 2]) + m[3 * j:3 * j + 1]


def _mod_kernel(c_ref, w_ref, b_ref, o_ref):
    c = c_ref[...]
    h = (c * jax.nn.sigmoid(c)).astype(BF16)
    o_ref[...] = _dot(h, w_ref[...].astype(BF16)) + b_ref[...]


def _mod_call(cond, w_mod, b_mod, tn=1024):
    depth, d, n = w_mod.shape
    r = cond.shape[0]
    return pl.pallas_call(
        _mod_kernel,
        out_shape=jax.ShapeDtypeStruct((depth, r, n), F32),
        grid=(depth, n // tn),
        in_specs=[pl.BlockSpec((r, d), lambda l, j: (0, 0)),
                  pl.BlockSpec((None, d, tn), lambda l, j: (l, 0, j)),
                  pl.BlockSpec((None, 1, tn), lambda l, j: (l, 0, j))],
        out_specs=pl.BlockSpec((None, r, tn), lambda l, j: (l, 0, j)),
        compiler_params=_params("parallel", "parallel"),
        name="mod_proj",
    )(cond, w_mod, b_mod.reshape(depth, 1, n))


def _ffn_kernel(x_ref, mod_ref, w1_ref, w3_ref, w2_ref, g_ref, b_ref, o_ref, h_ref, *, j, alpha, rows):
    f = pl.program_id(1)
    last = pl.num_programs(1) - 1
    tm = x_ref.shape[0]

    def chunk_update(h):
        a = _dot(h, w1_ref[...])
        b = _dot(h, w3_ref[...])
        act = (a * jax.nn.sigmoid(a) * b).astype(BF16)
        return _dot(act, w2_ref[...])

    @pl.when(f == 0)
    def _():
        m = mod_ref[...]
        for r0 in range(0, tm, rows):
            h = _modulated(x_ref[r0:r0 + rows, :], m, j).astype(BF16)
            h_ref[r0:r0 + rows, :] = h
            o_ref[r0:r0 + rows, :] = chunk_update(h)

    @pl.when((f > 0) & (f < last))
    def _():
        o_ref[...] += chunk_update(h_ref[...])

    @pl.when(f == last)
    def _():
        half_gate = 0.5 * mod_ref[...][3 * j + 2:3 * j + 3]
        for r0 in range(0, tm, rows):
            y = o_ref[r0:r0 + rows, :] + chunk_update(h_ref[r0:r0 + rows, :])
            o_ref[r0:r0 + rows, :] = _post_norm(x_ref[r0:r0 + rows, :], half_gate * y,
                                                g_ref[...], b_ref[...], alpha)


def _ffn_call(x, mod, tokens_per_cond, w1, w3, w2, l, s, ln_g, ln_b, j, alpha, tm=512, tf=512,
              rows=256):
    t, d = x.shape
    ff = w1.shape[-1]
    tm = min(tm, tokens_per_cond)
    tf = min(tf, ff)
    assert tokens_per_cond % tm == 0 and ff % tf == 0 and ff // tf >= 2
    per = tokens_per_cond // tm
    return pl.pallas_call(
        functools.partial(_ffn_kernel, j=j, alpha=alpha, rows=min(rows, tm)),
        out_shape=jax.ShapeDtypeStruct((t, d), F32),
        grid=(t // tm, ff // tf),
        in_specs=[pl.BlockSpec((tm, d), lambda i, f: (i, 0)),
                  pl.BlockSpec((None, 9, d), lambda i, f: (i // per, 0, 0)),
                  pl.BlockSpec((None, None, d, tf), lambda i, f: (l, s, 0, f)),
                  pl.BlockSpec((None, None, d, tf), lambda i, f: (l, s, 0, f)),
                  pl.BlockSpec((None, None, tf, d), lambda i, f: (l, s, f, 0)),
                  pl.BlockSpec((1, d), lambda i, f: (0, 0)),
                  pl.BlockSpec((1, d), lambda i, f: (0, 0))],
        out_specs=pl.BlockSpec((tm, d), lambda i, f: (i, 0)),
        scratch_shapes=[pltpu.VMEM((tm, d), BF16)],
        compiler_params=_params("parallel", "arbitrary"),
        name="ffn",
    )(x, mod, w1, w3, w2, ln_g, ln_b)


def _na_proj_kernel(x_ref, mod_ref, w_ref, o_ref, *kv_refs, q_scale, width, rows):
    m = mod_ref[...]
    for r0 in range(0, x_ref.shape[0], rows):
        r = slice(r0, r0 + rows)
        h = _modulated(x_ref[r, :], m, 1).astype(BF16)
        for n in range(w_ref.shape[-1] // width):
            y = _dot(h, w_ref[:, n * width:(n + 1) * width])
            o_ref[r, n * width:(n + 1) * width] = (y * q_scale if n == 0 else y).astype(BF16)
            if kv_refs and n in (1, 2):
                kv_refs[n - 1][r, :] = y


def _resident(block_shape, index):
    return pl.BlockSpec(block_shape, lambda *_: index, pipeline_mode=pl.Buffered(1))


def _na_proj_call(x, mod, tokens_per_cond, w_in, i_even, na_width, q_scale, with_kv, tm=512, rows=256):
    t, d = x.shape
    n_in = w_in.shape[-1]
    assert n_in % na_width == 0, "q, k, v and pooled widths must share one tile width"
    tm = min(tm, tokens_per_cond)
    per = tokens_per_cond // tm
    out_shape = [jax.ShapeDtypeStruct((t, n_in), BF16)]
    out_specs = [pl.BlockSpec((tm, n_in), lambda i: (i, 0))]
    if with_kv:
        out_shape += [jax.ShapeDtypeStruct((t, na_width), F32)] * 2
        out_specs += [pl.BlockSpec((tm, na_width), lambda i: (i, 0))] * 2
    return pl.pallas_call(
        functools.partial(_na_proj_kernel, q_scale=q_scale, width=na_width, rows=min(rows, tm)),
        out_shape=out_shape,
        grid=(t // tm,),
        in_specs=[pl.BlockSpec((tm, d), lambda i: (i, 0)),
                  pl.BlockSpec((None, 9, d), lambda i: (i // per, 0, 0)),
                  _resident((None, d, n_in), (i_even, 0, 0))],
        out_specs=out_specs,
        compiler_params=_params("parallel"),
        name="na_proj",
    )(x, mod, w_in)


def _attn_rows(q, kv_refs, chunk, dv):
    rows = q.shape[0]
    den_in_v = kv_refs[1].shape[-1] == 2 * dv
    m = jnp.full((rows, 1), MASKED, F32)
    acc = jnp.zeros((rows, kv_refs[1].shape[-1]), F32)
    den = jnp.zeros((rows, 1), F32)
    for k_ref, v_ref in zip(kv_refs[0::2], kv_refs[1::2]):
        lk = k_ref.shape[0]
        step = min(chunk, lk)
        for c0 in range(0, lk, step):
            s = lax.dot_general(q, k_ref[c0:c0 + step, :], _NT, preferred_element_type=F32)
            m_new = jnp.maximum(m, jnp.max(s, axis=-1, keepdims=True))
            rescale = jnp.exp2(m - m_new)
            p = jnp.exp2(s - m_new)
            if not den_in_v:
                den = rescale * den + jnp.sum(p, axis=-1, keepdims=True)
            acc = rescale * acc + _dot(p.astype(BF16), v_ref[c0:c0 + step, :])
            m = m_new
    return acc[:, :dv] / (acc[:, dv:] if den_in_v else den)


def _attn_kernel(*refs, chunk, dv, rows):
    q_ref, o_ref = refs[0], refs[-1]
    for r0 in range(0, q_ref.shape[0], rows):
        o = _attn_rows(q_ref[r0:r0 + rows, :], refs[1:-1], chunk, dv)
        o_ref[r0:r0 + rows, :] = o.astype(o_ref.dtype)


def _attn_call(q, q_col0, dq, segments, n_heads, dv, v_width, tq, chunk, name):
    b, l, _ = q.shape
    tq = min(tq, l)
    in_specs = [pl.BlockSpec((None, tq, dq), lambda bi, h, qi: (bi, qi, q_col0 + h))]
    args = [q]
    for k, kc, v, vc in segments:
        lk = k.shape[1]
        assert lk % min(chunk, lk) == 0
        in_specs.append(pl.BlockSpec((None, lk, dq), lambda bi, h, qi, kc=kc: (bi, 0, kc + h)))
        in_specs.append(pl.BlockSpec((None, lk, v_width), lambda bi, h, qi, vc=vc: (bi, 0, vc + h)))
        args += [k, v]
    return pl.pallas_call(
        functools.partial(_attn_kernel, chunk=chunk, dv=dv, rows=min(ATTN_ROWS, tq)),
        out_shape=jax.ShapeDtypeStruct((b, l, n_heads * dv), BF16),
        grid=(b, n_heads, l // tq),
        in_specs=in_specs,
        out_specs=pl.BlockSpec((None, tq, dv), lambda bi, h, qi: (bi, qi, h)),
        compiler_params=_params("parallel", "parallel", "arbitrary"),
        name=name,
    )(*args)


def _na_kernel(q_ref, k_ref, v_ref, kc_ref, vc_ref, bias_ref, o_ref, *, n_rows, half_rows, groups):
    n_groups = n_rows // NA_GROUP_ROWS
    tq = NA_GROUP_ROWS * GRID_W
    win = NA_WINDOW_ROWS * GRID_W
    for c in range(groups):
        g = pl.program_id(2) * groups + c
        base_row = jnp.clip(g * NA_GROUP_ROWS - half_rows, 0, n_rows - NA_WINDOW_ROWS)
        base = pl.multiple_of(base_row * GRID_W, GRID_W)
        variant = jnp.where(g == 0, 0, jnp.where(g == n_groups - 1, 2, 1))
        q = q_ref[c * tq:(c + 1) * tq, :]
        kw = k_ref[pl.ds(base, win), :]
        vw = v_ref[pl.ds(base, win), :]
        s_loc = lax.dot_general(q, kw, _NT, preferred_element_type=F32) + bias_ref[variant]
        s_ctx = lax.dot_general(q, kc_ref[...], _NT, preferred_element_type=F32)
        m = jnp.maximum(jnp.max(s_loc, axis=-1, keepdims=True),
                        jnp.max(s_ctx, axis=-1, keepdims=True))
        p_loc = jnp.exp2(s_loc - m)
        p_ctx = jnp.exp2(s_ctx - m)
        den = jnp.sum(p_loc, axis=-1, keepdims=True) + jnp.sum(p_ctx, axis=-1, keepdims=True)
        o = _dot(p_loc.astype(BF16), vw) + _dot(p_ctx.astype(BF16), vc_ref[...])
        o_ref[c * tq:(c + 1) * tq, :] = (o / den).astype(o_ref.dtype)


def _na_group_rows(g, n_rows, na_rows):
    kr = min(na_rows, n_rows)
    base = int(np.clip(g * NA_GROUP_ROWS - kr // 2, 0, n_rows - NA_WINDOW_ROWS))
    rq = (g * NA_GROUP_ROWS + np.arange(NA_GROUP_ROWS))[:, None]
    rk = (base + np.arange(NA_WINDOW_ROWS))[None, :]
    rs = np.clip(rq - kr // 2, 0, n_rows - kr)
    ok = (rk >= rs) & (rk < rs + kr)
    assert (ok.sum(axis=1) == kr).all(), "key window misses part of a neighbourhood"
    return np.where(ok, rk - rq + na_rows - 1, 0), ok


def _na_bias_table(rpb, n_rows):
    n_heads, nr2, nc2 = rpb.shape
    na_rows, na_cols = (nr2 + 1) // 2, (nc2 + 1) // 2
    n_groups = n_rows // NA_GROUP_ROWS
    per_group = [_na_group_rows(g, n_rows, na_rows) for g in range(n_groups)]
    for g in range(2, n_groups - 1):
        assert all((a == b).all() for a, b in zip(per_group[g], per_group[1])), "interior groups differ"
    idx_r, row_ok = (np.stack([per_group[g][i] for g in (0, 1, n_groups - 1)]) for i in range(2))
    w = GRID_W
    e = jnp.pad(rpb, ((0, 0), (0, 0), (w - na_cols, 2 * w - (w - na_cols) - nc2)))
    band = jnp.tile(e, (1, 1, w))[..., :w * (2 * w - 1)].reshape(n_heads, nr2, w, 2 * w - 1)[..., w - 1:]
    cq, ck = np.arange(w)[:, None], np.arange(w)[None, :]
    cs = np.clip(cq - na_cols // 2, 0, w - na_cols)
    col_ok = (ck >= cs) & (ck < cs + na_cols)
    blocks = jnp.take(band, jnp.asarray(idx_r.reshape(-1)), axis=1)
    blocks = blocks.reshape((n_heads,) + idx_r.shape + (w, w))
    ok = row_ok[:, :, :, None, None] & col_ok[None, None, None]
    table = jnp.where(ok[None], blocks * LOG2E, MASKED).transpose(0, 1, 2, 4, 3, 5)
    return table.reshape(n_heads, 3, NA_GROUP_ROWS * w, NA_WINDOW_ROWS * w).astype(F32)


def _na_call(proj, kc, vc, bias, n_heads, dh, na_rows, groups=4):
    b, l, _ = proj.shape
    n_rows = l // GRID_W
    assert n_rows % NA_GROUP_ROWS == 0 and n_rows >= NA_WINDOW_ROWS + NA_GROUP_ROWS
    assert NA_GROUP_ROWS - 1 + min(na_rows, n_rows) <= NA_WINDOW_ROWS
    n_groups = n_rows // NA_GROUP_ROWS
    groups = math.gcd(n_groups, groups)
    tq = groups * NA_GROUP_ROWS * GRID_W
    lc = kc.shape[1]
    return pl.pallas_call(
        functools.partial(_na_kernel, n_rows=n_rows, half_rows=min(na_rows, n_rows) // 2,
                          groups=groups),
        out_shape=jax.ShapeDtypeStruct((b, l, n_heads * dh), BF16),
        grid=(b, n_heads, n_groups // groups),
        in_specs=[pl.BlockSpec((None, tq, dh), lambda bi, h, g: (bi, g, h)),
                  pl.BlockSpec((None, l, dh), lambda bi, h, g: (bi, 0, n_heads + h)),
                  pl.BlockSpec((None, l, dh), lambda bi, h, g: (bi, 0, 2 * n_heads + h)),
                  pl.BlockSpec((None, lc, dh), lambda bi, h, g: (bi, 0, h)),
                  pl.BlockSpec((None, lc, dh), lambda bi, h, g: (bi, 0, h)),
                  pl.BlockSpec((None,) + bias.shape[1:], lambda bi, h, g: (h, 0, 0, 0))],
        out_specs=pl.BlockSpec((None, tq, dh), lambda bi, h, g: (bi, g, h)),
        compiler_params=_params("parallel", "parallel", "arbitrary"),
        name="na_attn",
    )(proj, proj, proj, kc, vc, bias)


def _pool_kernel(u_ref, w_ref, s_ref, o_ref, *, chunk, window):
    l = u_ref.shape[0]
    half = lax.shift_left(jnp.int32(1), pl.program_id(1).astype(jnp.int32))
    w = w_ref[...]
    scale = s_ref[...]

    def body(c, carry):
        t0 = pl.multiple_of(c * chunk, chunk)
        s0 = pl.multiple_of(jnp.clip(t0 - (window - chunk) // 2, 0, l - window), 16)
        t = t0 + lax.broadcasted_iota(jnp.int32, (chunk, 1), 0)
        lo = jnp.maximum(t - half, 0)
        hi = jnp.minimum(t + half, l)
        pos = s0 + lax.broadcasted_iota(jnp.int32, (1, window), 1)
        band = jnp.where((pos >= lo) & (pos < hi), 1.0, 0.0).astype(BF16)
        sums = _dot(band, u_ref[pl.ds(s0, window), :])
        mean = sums / (hi - lo).astype(F32)
        d = (mean - u_ref[pl.ds(t0, chunk), :].astype(F32)).astype(BF16)
        o_ref[pl.ds(t0, chunk), :] = (_dot(d, w) * scale).astype(o_ref.dtype)
        return carry

    lax.fori_loop(0, l // chunk, body, 0)


def _pool_call(proj, u_col0, w_pool, pool_scale, i_even):
    b, l, _ = proj.shape
    n_groups, ch, _ = w_pool.shape[1:]
    assert n_groups == len(POOL_WINDOWS) and all(w == 2 << g for g, w in enumerate(POOL_WINDOWS))
    chunk = min(256, l)
    window = min(2 * chunk, l)
    assert l % chunk == 0 and (window == l or window - chunk >= max(POOL_WINDOWS))
    return pl.pallas_call(
        functools.partial(_pool_kernel, chunk=chunk, window=window),
        out_shape=jax.ShapeDtypeStruct((b, l, n_groups * ch), BF16),
        grid=(b, n_groups),
        in_specs=[pl.BlockSpec((None, l, ch), lambda bi, g: (bi, 0, u_col0 + g)),
                  pl.BlockSpec((None, None, ch, ch), lambda bi, g: (i_even, g, 0, 0)),
                  pl.BlockSpec((None, 1, ch), lambda bi, g: (i_even, 0, g))],
        out_specs=pl.BlockSpec((None, l, ch), lambda bi, g: (bi, 0, g)),
        compiler_params=_params("parallel", "parallel"),
        name="pool",
    )(proj, w_pool, pool_scale)


def _out_kernel(*refs, n_lhs, alpha, rows):
    x_ref, mod_ref = refs[0], refs[1]
    g_ref, b_ref, o_ref = refs[-3], refs[-2], refs[-1]
    gate = mod_ref[...][5:6]
    for r0 in range(0, x_ref.shape[0], rows):
        r = slice(r0, r0 + rows)
        y = functools.reduce(jnp.add, [_dot(refs[2 + 2 * i][r, :], refs[3 + 2 * i][...])
                                       for i in range(n_lhs)])
        o_ref[r, :] = _post_norm(x_ref[r, :], gate * y, g_ref[...], b_ref[...], alpha)


def _out_call(x, mod, tokens_per_cond, lhs_w, ln_g, ln_b, alpha, tm=1024, rows=256):
    t, d = x.shape
    tm = min(tm, tokens_per_cond)
    per = tokens_per_cond // tm
    in_specs = [pl.BlockSpec((tm, d), lambda i: (i, 0)),
                pl.BlockSpec((None, 9, d), lambda i: (i // per, 0, 0))]
    args = [x, mod]
    for lhs, w, w_block, w_index in lhs_w:
        in_specs.append(pl.BlockSpec((tm, lhs.shape[1]), lambda i: (i, 0)))
        in_specs.append(_resident(w_block, w_index))
        args += [lhs, w]
    in_specs += [pl.BlockSpec((1, d), lambda i: (0, 0))] * 2
    args += [ln_g, ln_b]
    return pl.pallas_call(
        functools.partial(_out_kernel, n_lhs=len(lhs_w), alpha=alpha, rows=min(rows, tm)),
        out_shape=jax.ShapeDtypeStruct((t, d), F32),
        grid=(t // tm,),
        in_specs=in_specs,
        out_specs=pl.BlockSpec((tm, d), lambda i: (i, 0)),
        compiler_params=_params("parallel"),
        name="out_proj",
    )(*args)


def _rms(x, g):
    return x * lax.rsqrt(jnp.mean(x * x, axis=-1, keepdims=True) + RMS_EPS) * g


def _rotary_pair(t, tab):
    u = t * tab
    return u + pltpu.roll(u, u.shape[-1] // 2, 1)


def _rope_half_mask(shape):
    return lax.broadcasted_iota(jnp.int32, shape, 1) < shape[-1] // 2


def _head_store(ref, r, h, width, value):
    ref[r, h * width:h * width + value.shape[-1]] = value.astype(ref.dtype)


def _store_values_with_ones(v_ref, r, v, n_heads, dv):
    ones = jnp.ones((v.shape[0], dv), v_ref.dtype)
    for hd in range(n_heads):
        _head_store(v_ref, r, hd, 2 * dv, v[:, hd * dv:(hd + 1) * dv])
        v_ref[r, (2 * hd + 1) * dv:(2 * hd + 2) * dv] = ones


def _mla_proj_kernel(x_ref, mod_ref, wd_ref, qn_ref, kvn_ref, wq_ref, wk_ref, wv_ref, tab_ref,
                     q_ref, k_ref, v_ref, *state_refs, n_heads, q_lora, kv_lora, nope, q_scale, rows):
    m = mod_ref[...]
    dq = 2 * nope
    for r0 in range(0, x_ref.shape[0], rows):
        r = slice(r0, r0 + rows)
        h = _modulated(x_ref[r, :], m, 1).astype(BF16)
        down = _dot(h, wd_ref[...])
        cq = _rms(down[:, :q_lora], qn_ref[...])
        ckv = _rms(down[:, q_lora:q_lora + kv_lora], kvn_ref[...])
        pe = down[:, q_lora + kv_lora:]
        tab = tab_ref[r, :]
        if state_refs:
            state_refs[0][r, :] = ckv
            state_refs[1][r, :] = pe[:, :pe.shape[-1] // 2]
        kpe = jnp.where(_rope_half_mask(pe.shape), _rotary_pair(pe, tab), 0.0).astype(BF16)
        q = _dot(cq.astype(BF16), wq_ref[...]) * q_scale
        ckv_b = ckv.astype(BF16)
        kn = _dot(ckv_b, wk_ref[...])
        _store_values_with_ones(v_ref, r, _dot(ckv_b, wv_ref[...]), n_heads, nope)
        for hd in range(n_heads):
            _head_store(q_ref, r, hd, dq, q[:, hd * dq:hd * dq + nope])
            q_ref[r, hd * dq + nope:(hd + 1) * dq] = _rotary_pair(
                q[:, hd * dq + nope:(hd + 1) * dq], tab).astype(BF16)
            _head_store(k_ref, r, hd, dq, kn[:, hd * nope:(hd + 1) * nope])
            k_ref[r, hd * dq + nope:(hd + 1) * dq] = kpe


def _mla_proj_call(x, mod, tokens_per_cond, wd, qn, kvn, wq, wk, wv, tab, i_odd, dims, with_state,
                   tm=512, rows=256):
    n_heads, q_lora, kv_lora, nope, rope, q_scale = dims
    t, d = x.shape
    tm = min(tm, tokens_per_cond, tab.shape[0])
    per = tokens_per_cond // tm
    tab_blocks = tab.shape[0] // tm
    dq = 2 * nope
    out_shape = [jax.ShapeDtypeStruct((t, n_heads * dq), BF16),
                 jax.ShapeDtypeStruct((t, n_heads * dq), BF16),
                 jax.ShapeDtypeStruct((t, n_heads * dq), BF16)]
    out_specs = [pl.BlockSpec((tm, n_heads * dq), lambda i: (i, 0)),
                 pl.BlockSpec((tm, n_heads * dq), lambda i: (i, 0)),
                 pl.BlockSpec((tm, n_heads * dq), lambda i: (i, 0))]
    if with_state:
        out_shape += [jax.ShapeDtypeStruct((t, kv_lora), F32), jax.ShapeDtypeStruct((t, rope), F32)]
        out_specs += [pl.BlockSpec((tm, kv_lora), lambda i: (i, 0)),
                      pl.BlockSpec((tm, rope), lambda i: (i, 0))]
    whole = lambda a: _resident((None,) + a.shape[1:], (i_odd,) + (0,) * (a.ndim - 1))
    return pl.pallas_call(
        functools.partial(_mla_proj_kernel, n_heads=n_heads, q_lora=q_lora, kv_lora=kv_lora,
                          nope=nope, q_scale=q_scale, rows=min(rows, tm)),
        out_shape=out_shape,
        grid=(t // tm,),
        in_specs=[pl.BlockSpec((tm, d), lambda i: (i, 0)),
                  pl.BlockSpec((None, 9, d), lambda i: (i // per, 0, 0)),
                  whole(wd), whole(qn), whole(kvn), whole(wq), whole(wk), whole(wv),
                  pl.BlockSpec((tm, tab.shape[1]), lambda i: (i % tab_blocks, 0))],
        out_specs=out_specs,
        compiler_params=_params("parallel"),
        name="mla_proj",
    )(x, mod, wd, qn, kvn, wq, wk, wv, tab)


def _mla_expand_kernel(ckv_ref, kpe_ref, wk_ref, wv_ref, k_ref, v_ref, *, n_heads, nope):
    ckv = ckv_ref[...].astype(BF16)
    kn = _dot(ckv, wk_ref[...])
    r = slice(None)
    _store_values_with_ones(v_ref, r, _dot(ckv, wv_ref[...]), n_heads, nope)
    kpe = kpe_ref[...].astype(BF16)
    dq = 2 * nope
    for hd in range(n_heads):
        _head_store(k_ref, r, hd, dq, kn[:, hd * nope:(hd + 1) * nope])
        k_ref[:, hd * dq + nope:(hd + 1) * dq] = kpe


def _mla_expand_call(ckv, kpe_pad, wk, wv, i_odd, n_heads, nope, tm=256):
    t, kv_lora = ckv.shape
    tm = min(tm, t)
    dq = 2 * nope
    whole = lambda a: pl.BlockSpec((None,) + a.shape[1:], lambda i: (i_odd,) + (0,) * (a.ndim - 1))
    return pl.pallas_call(
        functools.partial(_mla_expand_kernel, n_heads=n_heads, nope=nope),
        out_shape=[jax.ShapeDtypeStruct((t, n_heads * dq), BF16),
                   jax.ShapeDtypeStruct((t, n_heads * dq), BF16)],
        grid=(t // tm,),
        in_specs=[pl.BlockSpec((tm, kv_lora), lambda i: (i, 0)),
                  pl.BlockSpec((tm, kpe_pad.shape[1]), lambda i: (i, 0)),
                  whole(wk), whole(wv)],
        out_specs=[pl.BlockSpec((tm, n_heads * dq), lambda i: (i, 0)),
                   pl.BlockSpec((tm, n_heads * dq), lambda i: (i, 0))],
        compiler_params=_params("parallel"),
        name="mla_expand",
    )(ckv, kpe_pad, wk, wv)


def _rotate_half_columns(w, rope):
    q = rope // 4
    parts = [w[..., i * q:(i + 1) * q] for i in range(4)]
    return jnp.concatenate([-parts[1], parts[0], -parts[3], parts[2]], axis=-1)


def _rope_table(n_tokens, rope):
    axis = rope // 2
    t = jnp.arange(n_tokens)
    inv = ROPE_BASE ** (-jnp.arange(0, axis, 2, dtype=F32) / axis)
    ang_r = (t // GRID_W).astype(F32)[:, None] * inv[None, :]
    ang_c = (t % GRID_W).astype(F32)[:, None] * inv[None, :]
    ang = jnp.concatenate([ang_r, ang_r, ang_c, ang_c], axis=-1)
    return jnp.concatenate([jnp.cos(ang), jnp.sin(ang)], axis=-1)


def _identity_rope_table(n_tokens, rope):
    return jnp.concatenate([jnp.ones((n_tokens, rope), F32), jnp.zeros((n_tokens, rope), F32)], axis=-1)


def kernel(x_prompt, x_sample, cache_na_k, cache_na_v, cache_mla_ckv, cache_mla_kpe, c, c_ctx,
           w_mod, b_mod, ln_g, ln_b, ffn_w1, ffn_w3, ffn_w2,
           na_w_in, mix0_w_out, na_rpb, pool_w, pool_scale,
           mla_w_down, mla_q_norm, mla_w_uq, mla_kv_norm, mla_w_ukv, mla_w_out):
    depth, d, _ = w_mod.shape
    alpha = (2 * depth) ** 0.25
    batch, seq, _ = x_prompt.shape
    dec_batch, dec_seq, _ = x_sample.shape
    na_heads, na_dh = cache_na_k.shape[-2:]
    na_width = na_heads * na_dh
    pool_groups, pool_ch = pool_w.shape[1:3]
    assert pool_groups * pool_ch == na_width and pool_ch % 128 == 0
    q_lora, kv_lora = mla_q_norm.shape[-1], mla_kv_norm.shape[-1]
    rope = cache_mla_kpe.shape[-1]
    uq_w, ukv_w, out_rows = mla_w_uq.shape[-1], mla_w_ukv.shape[-1], mla_w_out.shape[1]
    mla_heads = (uq_w - ukv_w + out_rows) // rope
    nope = uq_w // mla_heads - rope
    v_dim = out_rows // mla_heads
    assert nope == v_dim and 2 * rope == nope and nope % 128 == 0
    mla_dims = (mla_heads, q_lora, kv_lora, nope, rope, float((nope + rope) ** -0.5) * LOG2E)

    w1, w3, w2 = ffn_w1.astype(BF16), ffn_w3.astype(BF16), ffn_w2.astype(BF16)
    w_in = na_w_in.astype(BF16)
    w_out0 = mix0_w_out.astype(BF16)
    w_pool = pool_w.astype(BF16)
    p_scale = pool_scale.reshape(pool_scale.shape[0], 1, -1)
    kpe_cols = mla_w_down[..., q_lora + kv_lora:]
    wd = jnp.concatenate([mla_w_down, _rotate_half_columns(kpe_cols, rope)], axis=-1).astype(BF16)
    n_odd = mla_w_uq.shape[0]
    uq = mla_w_uq.reshape(n_odd, q_lora, mla_heads, nope + rope)
    wq = jnp.concatenate([uq, _rotate_half_columns(uq[..., nope:], rope)], axis=-1)
    wq = wq.reshape(n_odd, q_lora, mla_heads * 2 * nope).astype(BF16)
    ukv = mla_w_ukv.reshape(n_odd, kv_lora, mla_heads, nope + v_dim)
    wk = ukv[..., :nope].reshape(n_odd, kv_lora, mla_heads * nope).astype(BF16)
    wv = ukv[..., nope:].reshape(n_odd, kv_lora, mla_heads * v_dim).astype(BF16)
    w_out1 = mla_w_out.astype(BF16)
    qn = mla_q_norm.reshape(n_odd, 1, q_lora)
    kvn = mla_kv_norm.reshape(n_odd, 1, kv_lora)

    n_cond = 1 + dec_batch
    cond = jnp.concatenate([c_ctx[None, :], c, jnp.zeros((-n_cond % 16, d), F32)], axis=0)
    mod = _mod_call(cond, w_mod, b_mod).reshape(depth, cond.shape[0], 9, d)

    rope_tab = _rope_table(dec_seq, rope)
    ident_tab = _identity_rope_table(min(256, seq), rope)

    def trunk(x3, mod_rows, per_cond, caches):
        bsz, l, _ = x3.shape
        x = x3.reshape(bsz * l, d)
        states = []
        for layer in range(depth):
            m = mod_rows[layer]
            g = lambda s: ln_g[layer, s].reshape(1, d)
            bb = lambda s: ln_b[layer, s].reshape(1, d)
            x = _ffn_call(x, m, per_cond, w1, w3, w2, layer, 0, g(0), bb(0), 0, alpha)
            i = layer // 2
            if layer % 2 == 0:
                outs = _na_proj_call(x, m, per_cond, w_in, i, na_width, float(na_dh ** -0.5) * LOG2E,
                                     caches is None)
                proj = outs[0].reshape(bsz, l, -1)
                if caches is None:
                    states.append((outs[1], outs[2]))
                    a = _attn_call(proj, 0, na_dh, [(proj, na_heads, proj, 2 * na_heads)],
                                   na_heads, na_dh, na_dh, 256, 512, "na_dense_attn")
                else:
                    kc = caches[0][:, i].reshape(bsz, -1, na_width).astype(BF16)
                    vc = caches[1][:, i].reshape(bsz, -1, na_width).astype(BF16)
                    bias = _na_bias_table(na_rpb[i], l // GRID_W)
                    a = _na_call(proj, kc, vc, bias, na_heads, na_dh, (na_rpb.shape[2] + 1) // 2)
                pooled = _pool_call(proj, 3 * na_width // pool_ch, w_pool, p_scale, i)
                x = _out_call(x, m, per_cond,
                              [(a.reshape(bsz * l, na_width), w_out0, (None, na_width, d), (i, 0, 0)),
                               (pooled.reshape(bsz * l, na_width), w_out0, (None, na_width, d), (i, 1, 0))],
                              g(1), bb(1), alpha)
            else:
                tab = ident_tab if caches is None else rope_tab
                outs = _mla_proj_call(x, m, per_cond, wd, qn, kvn, wq, wk, wv, tab, i, mla_dims,
                                      caches is None)
                q = outs[0].reshape(bsz, l, -1)
                k = outs[1].reshape(bsz, l, -1)
                v = outs[2].reshape(bsz, l, -1)
                segments = [(k, 0, v, 0)]
                if caches is None:
                    states.append((outs[3], outs[4]))
                else:
                    ckv_c = caches[2][:, i].reshape(-1, kv_lora)
                    kpe_c = caches[3][:, i].reshape(-1, rope)
                    kpe_c = jnp.concatenate([kpe_c, jnp.zeros_like(kpe_c)], axis=-1)
                    k_c, v_c = _mla_expand_call(ckv_c, kpe_c, wk, wv, i, mla_heads, nope)
                    segments.append((k_c.reshape(bsz, -1, k.shape[-1]), 0,
                                     v_c.reshape(bsz, -1, v.shape[-1]), 0))
                o = _attn_call(q, 0, 2 * nope, segments, mla_heads, v_dim, 2 * v_dim, 1024, 512,
                               "mla_attn")
                x = _out_call(x, m, per_cond,
                              [(o.reshape(bsz * l, -1), w_out1, (None,) + w_out1.shape[1:], (i, 0, 0))],
                              g(1), bb(1), alpha)
            x = _ffn_call(x, m, per_cond, w1, w3, w2, layer, 1, g(2), bb(2), 2, alpha)
        return x.reshape(bsz, l, d), states

    y_prompt, st = trunk(x_prompt, mod[:, 0:1], batch * seq, None)
    y_sample, _ = trunk(x_sample, mod[:, 1:n_cond], dec_seq,
                        (cache_na_k, cache_na_v, cache_mla_ckv, cache_mla_kpe))
    even = [s for layer, s in enumerate(st) if layer % 2 == 0]
    odd = [s for layer, s in enumerate(st) if layer % 2 == 1]
    new_na_k = jnp.stack([s[0].reshape(batch, seq, na_heads, na_dh) for s in even], axis=1)
    new_na_v = jnp.stack([s[1].reshape(batch, seq, na_heads, na_dh) for s in even], axis=1)
    new_mla_ckv = jnp.stack([s[0].reshape(batch, seq, kv_lora) for s in odd], axis=1)
    new_mla_kpe = jnp.stack([s[1].reshape(batch, seq, rope) for s in odd], axis=1)
    return (y_prompt, y_sample, new_na_k, new_na_v, new_mla_ckv, new_mla_kpe)
```

```python
import functools
import math

import numpy as np
import jax
import jax.numpy as jnp
from jax import lax
from jax.experimental import pallas as pl
from jax.experimental.pallas import tpu as pltpu

GRID_W = 64
LN_EPS = 1e-5
RMS_EPS = 1e-6
POOL_WINDOWS = (2, 4, 8, 16)
ROPE_BASE = 10000.0
MASKED = -1e30
LOG2E = math.log2(math.e)
FFN_CHUNK = 512
ATTN_ROWS = 256
NA_GROUP_ROWS = 4
NA_WINDOW_ROWS = 12
V7X_VMEM_LIMIT_BYTES = 56 * 1024 * 1024

F32 = jnp.float32
BF16 = jnp.bfloat16
_NT = (((1,), (1,)), ((), ()))


def _params(*semantics):
    return pltpu.CompilerParams(dimension_semantics=semantics,
                                vmem_limit_bytes=V7X_VMEM_LIMIT_BYTES)


def _dot(a, b):
    return jnp.dot(a, b, preferred_element_type=F32)


def _post_norm(x, update, g, b, alpha):
    z = alpha * x + update
    mu = jnp.mean(z, axis=-1, keepdims=True)
    zc = z - mu
    var = jnp.mean(zc * zc, axis=-1, keepdims=True)
    return zc * lax.rsqrt(var + LN_EPS) * g + b


def _modulated(x, m, j):
    return x * (1.0 + m[3 * j + 1:3 * j + 2]) + m[3 * j:3 * j + 1]


def _mod_kernel(c_ref, w_ref, b_ref, o_ref):
    c = c_ref[...]
    h = (c * jax.nn.sigmoid(c)).astype(BF16)
    o_ref[...] = _dot(h, w_ref[...].astype(BF16)) + b_ref[...]


def _mod_call(cond, w_mod, b_mod, tn=1024):
    depth, d, n = w_mod.shape
    r = cond.shape[0]
    return pl.pallas_call(
        _mod_kernel,
        out_shape=jax.ShapeDtypeStruct((depth, r, n), F32),
        grid=(depth, n // tn),
        in_specs=[pl.BlockSpec((r, d), lambda l, j: (0, 0)),
                  pl.BlockSpec((None, d, tn), lambda l, j: (l, 0, j)),
                  pl.BlockSpec((None, 1, tn), lambda l, j: (l, 0, j))],
        out_specs=pl.BlockSpec((None, r, tn), lambda l, j: (l, 0, j)),
        compiler_params=_params("parallel", "parallel"),
        name="mod_proj",
    )(cond, w_mod, b_mod.reshape(depth, 1, n))


def _ffn_kernel(x_ref, mod_ref, w1_ref, w3_ref, w2_ref, g_ref, b_ref, o_ref, h_ref, *, j, alpha, rows,
                mid_rows):
    f = pl.program_id(1)
    last = pl.num_programs(1) - 1
    tm = x_ref.shape[0]

    def chunk_update(h):
        a = _dot(h, w1_ref[...])
        b = _dot(h, w3_ref[...])
        act = (a * jax.nn.sigmoid(a) * b).astype(BF16)
        return _dot(act, w2_ref[...])

    @pl.when(f == 0)
    def _():
        m = mod_ref[...]
        for r0 in range(0, tm, rows):
            h = _modulated(x_ref[r0:r0 + rows, :], m, j).astype(BF16)
            h_ref[r0:r0 + rows, :] = h
            o_ref[r0:r0 + rows, :] = chunk_update(h)

    @pl.when((f > 0) & (f < last))
    def _():
        for r0 in range(0, tm, mid_rows):
            o_ref[r0:r0 + mid_rows, :] += chunk_update(h_ref[r0:r0 + mid_rows, :])

    @pl.when(f == last)
    def _():
        half_gate = 0.5 * mod_ref[...][3 * j + 2:3 * j + 3]
        for r0 in range(0, tm, rows):
            y = o_ref[r0:r0 + rows, :] + chunk_update(h_ref[r0:r0 + rows, :])
            o_ref[r0:r0 + rows, :] = _post_norm(x_ref[r0:r0 + rows, :], half_gate * y,
                                                g_ref[...], b_ref[...], alpha)


def _ffn_chunked(w, tf):
    *lead, d, ff = w.shape
    tf = min(tf, ff)
    assert ff % tf == 0 and ff // tf >= 2
    w = w.reshape(*lead, d, ff // tf, tf)
    return jnp.swapaxes(w, -3, -2).astype(BF16)


def _ffn_call(x, mod, tokens_per_cond, w1, w3, w2, l, s, ln_g, ln_b, j, alpha, tm=1024, rows=256,
              mid_rows=512):
    t, d = x.shape
    n_chunks, _, tf = w1.shape[2:]
    tm = min(tm, tokens_per_cond)
    assert tokens_per_cond % tm == 0
    per = tokens_per_cond // tm
    return pl.pallas_call(
        functools.partial(_ffn_kernel, j=j, alpha=alpha, rows=min(rows, tm), mid_rows=min(mid_rows, tm)),
        out_shape=jax.ShapeDtypeStruct((t, d), F32),
        grid=(t // tm, n_chunks),
        in_specs=[pl.BlockSpec((tm, d), lambda i, f: (i, 0)),
                  pl.BlockSpec((None, 9, d), lambda i, f: (i // per, 0, 0)),
                  pl.BlockSpec((None, None, None, d, tf), lambda i, f: (l, s, f, 0, 0)),
                  pl.BlockSpec((None, None, None, d, tf), lambda i, f: (l, s, f, 0, 0)),
                  pl.BlockSpec((None, None, tf, d), lambda i, f: (l, s, f, 0)),
                  pl.BlockSpec((1, d), lambda i, f: (0, 0)),
                  pl.BlockSpec((1, d), lambda i, f: (0, 0))],
        out_specs=pl.BlockSpec((tm, d), lambda i, f: (i, 0)),
        scratch_shapes=[pltpu.VMEM((tm, d), BF16)],
        compiler_params=_params("parallel", "arbitrary"),
        name="ffn",
    )(x, mod, w1, w3, w2, ln_g, ln_b)


def _na_proj_kernel(x_ref, mod_ref, w_ref, o_ref, *kv_refs, q_scale, width, rows):
    m = mod_ref[...]
    for r0 in range(0, x_ref.shape[0], rows):
        r = slice(r0, r0 + rows)
        h = _modulated(x_ref[r, :], m, 1).astype(BF16)
        for n in range(w_ref.shape[-1] // width):
            y = _dot(h, w_ref[:, n * width:(n + 1) * width])
            o_ref[r, n * width:(n + 1) * width] = (y * q_scale if n == 0 else y).astype(BF16)
            if kv_refs and n in (1, 2):
                kv_refs[n - 1][r, :] = y


def _resident(block_shape, index):
    return pl.BlockSpec(block_shape, lambda *_: index, pipeline_mode=pl.Buffered(1))


def _na_proj_call(x, mod, tokens_per_cond, w_in, i_even, na_width, q_scale, with_kv, tm=512, rows=256):
    t, d = x.shape
    n_in = w_in.shape[-1]
    assert n_in % na_width == 0, "q, k, v and pooled widths must share one tile width"
    tm = min(tm, tokens_per_cond)
    per = tokens_per_cond // tm
    out_shape = [jax.ShapeDtypeStruct((t, n_in), BF16)]
    out_specs = [pl.BlockSpec((tm, n_in), lambda i: (i, 0))]
    if with_kv:
        out_shape += [jax.ShapeDtypeStruct((t, na_width), F32)] * 2
        out_specs += [pl.BlockSpec((tm, na_width), lambda i: (i, 0))] * 2
    return pl.pallas_call(
        functools.partial(_na_proj_kernel, q_scale=q_scale, width=na_width, rows=min(rows, tm)),
        out_shape=out_shape,
        grid=(t // tm,),
        in_specs=[pl.BlockSpec((tm, d), lambda i: (i, 0)),
                  pl.BlockSpec((None, 9, d), lambda i: (i // per, 0, 0)),
                  _resident((None, d, n_in), (i_even, 0, 0))],
        out_specs=out_specs,
        compiler_params=_params("parallel"),
        name="na_proj",
    )(x, mod, w_in)


def _attn_rows(q, kv_refs, chunk, dv):
    rows = q.shape[0]
    den_in_v = kv_refs[1].shape[-1] == 2 * dv
    m = jnp.full((rows, 1), MASKED, F32)
    acc = jnp.zeros((rows, kv_refs[1].shape[-1]), F32)
    den = jnp.zeros((rows, 1), F32)
    for k_ref, v_ref in zip(kv_refs[0::2], kv_refs[1::2]):
        lk = k_ref.shape[0]
        step = min(chunk, lk)
        for c0 in range(0, lk, step):
            s = lax.dot_general(q, k_ref[c0:c0 + step, :], _NT, preferred_element_type=F32)
            m_new = jnp.maximum(m, jnp.max(s, axis=-1, keepdims=True))
            rescale = jnp.exp2(m - m_new)
            p = jnp.exp2(s - m_new)
            if not den_in_v:
                den = rescale * den + jnp.sum(p, axis=-1, keepdims=True)
            acc = rescale * acc + _dot(p.astype(BF16), v_ref[c0:c0 + step, :])
            m = m_new
    return acc[:, :dv] / (acc[:, dv:] if den_in_v else den)


def _attn_kernel(*refs, chunk, dv, rows):
    q_ref, o_ref = refs[0], refs[-1]
    for r0 in range(0, q_ref.shape[0], rows):
        o = _attn_rows(q_ref[r0:r0 + rows, :], refs[1:-1], chunk, dv)
        o_ref[r0:r0 + rows, :] = o.astype(o_ref.dtype)


def _attn_call(q, q_col0, dq, segments, n_heads, dv, v_width, tq, chunk, name):
    b, l, _ = q.shape
    tq = min(tq, l)
    in_specs = [pl.BlockSpec((None, tq, dq), lambda bi, h, qi: (bi, qi, q_col0 + h))]
    args = [q]
    for k, kc, v, vc in segments:
        lk = k.shape[1]
        assert lk % min(chunk, lk) == 0
        in_specs.append(pl.BlockSpec((None, lk, dq), lambda bi, h, qi, kc=kc: (bi, 0, kc + h)))
        in_specs.append(pl.BlockSpec((None, lk, v_width), lambda bi, h, qi, vc=vc: (bi, 0, vc + h)))
        args += [k, v]
    return pl.pallas_call(
        functools.partial(_attn_kernel, chunk=chunk, dv=dv, rows=min(ATTN_ROWS, tq)),
        out_shape=jax.ShapeDtypeStruct((b, l, n_heads * dv), BF16),
        grid=(b, n_heads, l // tq),
        in_specs=in_specs,
        out_specs=pl.BlockSpec((None, tq, dv), lambda bi, h, qi: (bi, qi, h)),
        compiler_params=_params("parallel", "parallel", "arbitrary"),
        name=name,
    )(*args)


def _na_kernel(q_ref, k_ref, v_ref, kc_ref, vc_ref, bias_ref, o_ref, *, n_rows, half_rows, groups):
    n_groups = n_rows // NA_GROUP_ROWS
    tq = NA_GROUP_ROWS * GRID_W
    win = NA_WINDOW_ROWS * GRID_W
    for c in range(groups):
        g = pl.program_id(2) * groups + c
        base_row = jnp.clip(g * NA_GROUP_ROWS - half_rows, 0, n_rows - NA_WINDOW_ROWS)
        base = pl.multiple_of(base_row * GRID_W, GRID_W)
        variant = jnp.where(g == 0, 0, jnp.where(g == n_groups - 1, 2, 1))
        q = q_ref[c * tq:(c + 1) * tq, :]
        kw = k_ref[pl.ds(base, win), :]
        vw = v_ref[pl.ds(base, win), :]
        s_loc = lax.dot_general(q, kw, _NT, preferred_element_type=F32) + bias_ref[variant]
        s_ctx = lax.dot_general(q, kc_ref[...], _NT, preferred_element_type=F32)
        m = jnp.maximum(jnp.max(s_loc, axis=-1, keepdims=True),
                        jnp.max(s_ctx, axis=-1, keepdims=True))
        p_loc = jnp.exp2(s_loc - m)
        p_ctx = jnp.exp2(s_ctx - m)
        den = jnp.sum(p_loc, axis=-1, keepdims=True) + jnp.sum(p_ctx, axis=-1, keepdims=True)
        o = _dot(p_loc.astype(BF16), vw) + _dot(p_ctx.astype(BF16), vc_ref[...])
        o_ref[c * tq:(c + 1) * tq, :] = (o / den).astype(o_ref.dtype)


def _na_group_rows(g, n_rows, na_rows):
    kr = min(na_rows, n_rows)
    base = int(np.clip(g * NA_GROUP_ROWS - kr // 2, 0, n_rows - NA_WINDOW_ROWS))
    rq = (g * NA_GROUP_ROWS + np.arange(NA_GROUP_ROWS))[:, None]
    rk = (base + np.arange(NA_WINDOW_ROWS))[None, :]
    rs = np.clip(rq - kr // 2, 0, n_rows - kr)
    ok = (rk >= rs) & (rk < rs + kr)
    assert (ok.sum(axis=1) == kr).all(), "key window misses part of a neighbourhood"
    return np.where(ok, rk - rq + na_rows - 1, 0), ok


def _na_bias_table(rpb, n_rows):
    n_heads, nr2, nc2 = rpb.shape
    na_rows, na_cols = (nr2 + 1) // 2, (nc2 + 1) // 2
    n_groups = n_rows // NA_GROUP_ROWS
    per_group = [_na_group_rows(g, n_rows, na_rows) for g in range(n_groups)]
    for g in range(2, n_groups - 1):
        assert all((a == b).all() for a, b in zip(per_group[g], per_group[1])), "interior groups differ"
    idx_r, row_ok = (np.stack([per_group[g][i] for g in (0, 1, n_groups - 1)]) for i in range(2))
    w = GRID_W
    e = jnp.pad(rpb, ((0, 0), (0, 0), (w - na_cols, 2 * w - (w - na_cols) - nc2)))
    band = jnp.tile(e, (1, 1, w))[..., :w * (2 * w - 1)].reshape(n_heads, nr2, w, 2 * w - 1)[..., w - 1:]
    cq, ck = np.arange(w)[:, None], np.arange(w)[None, :]
    cs = np.clip(cq - na_cols // 2, 0, w - na_cols)
    col_ok = (ck >= cs) & (ck < cs + na_cols)
    blocks = jnp.take(band, jnp.asarray(idx_r.reshape(-1)), axis=1)
    blocks = blocks.reshape((n_heads,) + idx_r.shape + (w, w))
    ok = row_ok[:, :, :, None, None] & col_ok[None, None, None]
    table = jnp.where(ok[None], blocks * LOG2E, MASKED).transpose(0, 1, 2, 4, 3, 5)
    return table.reshape(n_heads, 3, NA_GROUP_ROWS * w, NA_WINDOW_ROWS * w).astype(F32)


def _na_call(proj, kc, vc, bias, n_heads, dh, na_rows, groups=4):
    b, l, _ = proj.shape
    n_rows = l // GRID_W
    assert n_rows % NA_GROUP_ROWS == 0 and n_rows >= NA_WINDOW_ROWS + NA_GROUP_ROWS
    assert NA_GROUP_ROWS - 1 + min(na_rows, n_rows) <= NA_WINDOW_ROWS
    n_groups = n_rows // NA_GROUP_ROWS
    groups = math.gcd(n_groups, groups)
    tq = groups * NA_GROUP_ROWS * GRID_W
    lc = kc.shape[1]
    return pl.pallas_call(
        functools.partial(_na_kernel, n_rows=n_rows, half_rows=min(na_rows, n_rows) // 2,
                          groups=groups),
        out_shape=jax.ShapeDtypeStruct((b, l, n_heads * dh), BF16),
        grid=(b, n_heads, n_groups // groups),
        in_specs=[pl.BlockSpec((None, tq, dh), lambda bi, h, g: (bi, g, h)),
                  pl.BlockSpec((None, l, dh), lambda bi, h, g: (bi, 0, n_heads + h)),
                  pl.BlockSpec((None, l, dh), lambda bi, h, g: (bi, 0, 2 * n_heads + h)),
                  pl.BlockSpec((None, lc, dh), lambda bi, h, g: (bi, 0, h)),
                  pl.BlockSpec((None, lc, dh), lambda bi, h, g: (bi, 0, h)),
                  pl.BlockSpec((None,) + bias.shape[1:], lambda bi, h, g: (h, 0, 0, 0))],
        out_specs=pl.BlockSpec((None, tq, dh), lambda bi, h, g: (bi, g, h)),
        compiler_params=_params("parallel", "parallel", "arbitrary"),
        name="na_attn",
    )(proj, proj, proj, kc, vc, bias)


def _pool_kernel(u_ref, w_ref, s_ref, o_ref, *, chunk, window):
    l = u_ref.shape[0]
    half = lax.shift_left(jnp.int32(1), pl.program_id(1).astype(jnp.int32))
    w = w_ref[...]
    scale = s_ref[...]

    def body(c, carry):
        t0 = pl.multiple_of(c * chunk, chunk)
        s0 = pl.multiple_of(jnp.clip(t0 - (window - chunk) // 2, 0, l - window), 16)
        t = t0 + lax.broadcasted_iota(jnp.int32, (chunk, 1), 0)
        lo = jnp.maximum(t - half, 0)
        hi = jnp.minimum(t + half, l)
        pos = s0 + lax.broadcasted_iota(jnp.int32, (1, window), 1)
        band = jnp.where((pos >= lo) & (pos < hi), 1.0, 0.0).astype(BF16)
        sums = _dot(band, u_ref[pl.ds(s0, window), :])
        mean = sums / (hi - lo).astype(F32)
        d = (mean - u_ref[pl.ds(t0, chunk), :].astype(F32)).astype(BF16)
        o_ref[pl.ds(t0, chunk), :] = (_dot(d, w) * scale).astype(o_ref.dtype)
        return carry

    lax.fori_loop(0, l // chunk, body, 0)


def _pool_call(proj, u_col0, w_pool, pool_scale, i_even):
    b, l, _ = proj.shape
    n_groups, ch, _ = w_pool.shape[1:]
    assert n_groups == len(POOL_WINDOWS) and all(w == 2 << g for g, w in enumerate(POOL_WINDOWS))
    chunk = min(256, l)
    window = min(2 * chunk, l)
    assert l % chunk == 0 and (window == l or window - chunk >= max(POOL_WINDOWS))
    return pl.pallas_call(
        functools.partial(_pool_kernel, chunk=chunk, window=window),
        out_shape=jax.ShapeDtypeStruct((b, l, n_groups * ch), BF16),
        grid=(b, n_groups),
        in_specs=[pl.BlockSpec((None, l, ch), lambda bi, g: (bi, 0, u_col0 + g)),
                  pl.BlockSpec((None, None, ch, ch), lambda bi, g: (i_even, g, 0, 0)),
                  pl.BlockSpec((None, 1, ch), lambda bi, g: (i_even, 0, g))],
        out_specs=pl.BlockSpec((None, l, ch), lambda bi, g: (bi, 0, g)),
        compiler_params=_params("parallel", "parallel"),
        name="pool",
    )(proj, w_pool, pool_scale)


def _out_kernel(*refs, n_lhs, alpha, rows):
    x_ref, mod_ref = refs[0], refs[1]
    g_ref, b_ref, o_ref = refs[-3], refs[-2], refs[-1]
    gate = mod_ref[...][5:6]
    for r0 in range(0, x_ref.shape[0], rows):
        r = slice(r0, r0 + rows)
        y = functools.reduce(jnp.add, [_dot(refs[2 + 2 * i][r, :], refs[3 + 2 * i][...])
                                       for i in range(n_lhs)])
        o_ref[r, :] = _post_norm(x_ref[r, :], gate * y, g_ref[...], b_ref[...], alpha)


def _out_call(x, mod, tokens_per_cond, lhs_w, ln_g, ln_b, alpha, tm=1024, rows=256):
    t, d = x.shape
    tm = min(tm, tokens_per_cond)
    per = tokens_per_cond // tm
    in_specs = [pl.BlockSpec((tm, d), lambda i: (i, 0)),
                pl.BlockSpec((None, 9, d), lambda i: (i // per, 0, 0))]
    args = [x, mod]
    for lhs, w, w_block, w_index in lhs_w:
        in_specs.append(pl.BlockSpec((tm, lhs.shape[1]), lambda i: (i, 0)))
        in_specs.append(_resident(w_block, w_index))
        args += [lhs, w]
    in_specs += [pl.BlockSpec((1, d), lambda i: (0, 0))] * 2
    args += [ln_g, ln_b]
    return pl.pallas_call(
        functools.partial(_out_kernel, n_lhs=len(lhs_w), alpha=alpha, rows=min(rows, tm)),
        out_shape=jax.ShapeDtypeStruct((t, d), F32),
        grid=(t // tm,),
        in_specs=in_specs,
        out_specs=pl.BlockSpec((tm, d), lambda i: (i, 0)),
        compiler_params=_params("parallel"),
        name="out_proj",
    )(*args)


def _rms(x, g):
    return x * lax.rsqrt(jnp.mean(x * x, axis=-1, keepdims=True) + RMS_EPS) * g


def _rotary_pair(t, tab):
    u = t * tab
    return u + pltpu.roll(u, u.shape[-1] // 2, 1)


def _rope_half_mask(shape):
    return lax.broadcasted_iota(jnp.int32, shape, 1) < shape[-1] // 2


def _head_store(ref, r, h, width, value):
    ref[r, h * width:h * width + value.shape[-1]] = value.astype(ref.dtype)


def _store_values_with_ones(v_ref, r, v, n_heads, dv):
    ones = jnp.ones((v.shape[0], dv), v_ref.dtype)
    for hd in range(n_heads):
        _head_store(v_ref, r, hd, 2 * dv, v[:, hd * dv:(hd + 1) * dv])
        v_ref[r, (2 * hd + 1) * dv:(2 * hd + 2) * dv] = ones


def _mla_proj_kernel(x_ref, mod_ref, wd_ref, qn_ref, kvn_ref, wq_ref, wk_ref, wv_ref, tab_ref,
                     q_ref, k_ref, v_ref, *state_refs, n_heads, q_lora, kv_lora, nope, q_scale, rows):
    m = mod_ref[...]
    dq = 2 * nope
    for r0 in range(0, x_ref.shape[0], rows):
        r = slice(r0, r0 + rows)
        h = _modulated(x_ref[r, :], m, 1).astype(BF16)
        down = _dot(h, wd_ref[...])
        cq = _rms(down[:, :q_lora], qn_ref[...])
        ckv = _rms(down[:, q_lora:q_lora + kv_lora], kvn_ref[...])
        pe = down[:, q_lora + kv_lora:]
        tab = tab_ref[r, :]
        if state_refs:
            state_refs[0][r, :] = ckv
            state_refs[1][r, :] = pe[:, :pe.shape[-1] // 2]
        kpe = jnp.where(_rope_half_mask(pe.shape), _rotary_pair(pe, tab), 0.0).astype(BF16)
        q = _dot(cq.astype(BF16), wq_ref[...]) * q_scale
        ckv_b = ckv.astype(BF16)
        kn = _dot(ckv_b, wk_ref[...])
        _store_values_with_ones(v_ref, r, _dot(ckv_b, wv_ref[...]), n_heads, nope)
        for hd in range(n_heads):
            _head_store(q_ref, r, hd, dq, q[:, hd * dq:hd * dq + nope])
            q_ref[r, hd * dq + nope:(hd + 1) * dq] = _rotary_pair(
                q[:, hd * dq + nope:(hd + 1) * dq], tab).astype(BF16)
            _head_store(k_ref, r, hd, dq, kn[:, hd * nope:(hd + 1) * nope])
            k_ref[r, hd * dq + nope:(hd + 1) * dq] = kpe


def _mla_proj_call(x, mod, tokens_per_cond, wd, qn, kvn, wq, wk, wv, tab, i_odd, dims, with_state,
                   tm=512, rows=256):
    n_heads, q_lora, kv_lora, nope, rope, q_scale = dims
    t, d = x.shape
    tm = min(tm, tokens_per_cond, tab.shape[0])
    per = tokens_per_cond // tm
    tab_blocks = tab.shape[0] // tm
    dq = 2 * nope
    out_shape = [jax.ShapeDtypeStruct((t, n_heads * dq), BF16),
                 jax.ShapeDtypeStruct((t, n_heads * dq), BF16),
                 jax.ShapeDtypeStruct((t, n_heads * dq), BF16)]
    out_specs = [pl.BlockSpec((tm, n_heads * dq), lambda i: (i, 0)),
                 pl.BlockSpec((tm, n_heads * dq), lambda i: (i, 0)),
                 pl.BlockSpec((tm, n_heads * dq), lambda i: (i, 0))]
    if with_state:
        out_shape += [jax.ShapeDtypeStruct((t, kv_lora), F32), jax.ShapeDtypeStruct((t, rope), F32)]
        out_specs += [pl.BlockSpec((tm, kv_lora), lambda i: (i, 0)),
                      pl.BlockSpec((tm, rope), lambda i: (i, 0))]
    whole = lambda a: _resident((None,) + a.shape[1:], (i_odd,) + (0,) * (a.ndim - 1))
    return pl.pallas_call(
        functools.partial(_mla_proj_kernel, n_heads=n_heads, q_lora=q_lora, kv_lora=kv_lora,
                          nope=nope, q_scale=q_scale, rows=min(rows, tm)),
        out_shape=out_shape,
        grid=(t // tm,),
        in_specs=[pl.BlockSpec((tm, d), lambda i: (i, 0)),
                  pl.BlockSpec((None, 9, d), lambda i: (i // per, 0, 0)),
                  whole(wd), whole(qn), whole(kvn), whole(wq), whole(wk), whole(wv),
                  pl.BlockSpec((tm, tab.shape[1]), lambda i: (i % tab_blocks, 0))],
        out_specs=out_specs,
        compiler_params=_params("parallel"),
        name="mla_proj",
    )(x, mod, wd, qn, kvn, wq, wk, wv, tab)


def _mla_expand_kernel(ckv_ref, kpe_ref, wk_ref, wv_ref, k_ref, v_ref, *, n_heads, nope):
    ckv = ckv_ref[...].astype(BF16)
    kn = _dot(ckv, wk_ref[...])
    r = slice(None)
    _store_values_with_ones(v_ref, r, _dot(ckv, wv_ref[...]), n_heads, nope)
    kpe = kpe_ref[...].astype(BF16)
    dq = 2 * nope
    for hd in range(n_heads):
        _head_store(k_ref, r, hd, dq, kn[:, hd * nope:(hd + 1) * nope])
        k_ref[:, hd * dq + nope:(hd + 1) * dq] = kpe


def _mla_expand_call(ckv, kpe_pad, wk, wv, i_odd, n_heads, nope, tm=256):
    t, kv_lora = ckv.shape
    tm = min(tm, t)
    dq = 2 * nope
    whole = lambda a: pl.BlockSpec((None,) + a.shape[1:], lambda i: (i_odd,) + (0,) * (a.ndim - 1))
    return pl.pallas_call(
        functools.partial(_mla_expand_kernel, n_heads=n_heads, nope=nope),
        out_shape=[jax.ShapeDtypeStruct((t, n_heads * dq), BF16),
                   jax.ShapeDtypeStruct((t, n_heads * dq), BF16)],
        grid=(t // tm,),
        in_specs=[pl.BlockSpec((tm, kv_lora), lambda i: (i, 0)),
                  pl.BlockSpec((tm, kpe_pad.shape[1]), lambda i: (i, 0)),
                  whole(wk), whole(wv)],
        out_specs=[pl.BlockSpec((tm, n_heads * dq), lambda i: (i, 0)),
                   pl.BlockSpec((tm, n_heads * dq), lambda i: (i, 0))],
        compiler_params=_params("parallel"),
        name="mla_expand",
    )(ckv, kpe_pad, wk, wv)


def _rotate_half_columns(w, rope):
    q = rope // 4
    parts = [w[..., i * q:(i + 1) * q] for i in range(4)]
    return jnp.concatenate([-parts[1], parts[0], -parts[3], parts[2]], axis=-1)


def _rope_table(n_tokens, rope):
    axis = rope // 2
    t = jnp.arange(n_tokens)
    inv = ROPE_BASE ** (-jnp.arange(0, axis, 2, dtype=F32) / axis)
    ang_r = (t // GRID_W).astype(F32)[:, None] * inv[None, :]
    ang_c = (t % GRID_W).astype(F32)[:, None] * inv[None, :]
    ang = jnp.concatenate([ang_r, ang_r, ang_c, ang_c], axis=-1)
    return jnp.concatenate([jnp.cos(ang), jnp.sin(ang)], axis=-1)


def _identity_rope_table(n_tokens, rope):
    return jnp.concatenate([jnp.ones((n_tokens, rope), F32), jnp.zeros((n_tokens, rope), F32)], axis=-1)


def kernel(x_prompt, x_sample, cache_na_k, cache_na_v, cache_mla_ckv, cache_mla_kpe, c, c_ctx,
           w_mod, b_mod, ln_g, ln_b, ffn_w1, ffn_w3, ffn_w2,
           na_w_in, mix0_w_out, na_rpb, pool_w, pool_scale,
           mla_w_down, mla_q_norm, mla_w_uq, mla_kv_norm, mla_w_ukv, mla_w_out):
    depth, d, _ = w_mod.shape
    alpha = (2 * depth) ** 0.25
    batch, seq, _ = x_prompt.shape
    dec_batch, dec_seq, _ = x_sample.shape
    na_heads, na_dh = cache_na_k.shape[-2:]
    na_width = na_heads * na_dh
    pool_groups, pool_ch = pool_w.shape[1:3]
    assert pool_groups * pool_ch == na_width and pool_ch % 128 == 0
    q_lora, kv_lora = mla_q_norm.shape[-1], mla_kv_norm.shape[-1]
    rope = cache_mla_kpe.shape[-1]
    uq_w, ukv_w, out_rows = mla_w_uq.shape[-1], mla_w_ukv.shape[-1], mla_w_out.shape[1]
    mla_heads = (uq_w - ukv_w + out_rows) // rope
    nope = uq_w // mla_heads - rope
    v_dim = out_rows // mla_heads
    assert nope == v_dim and 2 * rope == nope and nope % 128 == 0
    mla_dims = (mla_heads, q_lora, kv_lora, nope, rope, float((nope + rope) ** -0.5) * LOG2E)

    w1, w3, w2 = _ffn_chunked(ffn_w1, FFN_CHUNK), _ffn_chunked(ffn_w3, FFN_CHUNK), ffn_w2.astype(BF16)
    w_in = na_w_in.astype(BF16)
    w_out0 = mix0_w_out.astype(BF16)
    w_pool = pool_w.astype(BF16)
    p_scale = pool_scale.reshape(pool_scale.shape[0], 1, -1)
    kpe_cols = mla_w_down[..., q_lora + kv_lora:]
    wd = jnp.concatenate([mla_w_down, _rotate_half_columns(kpe_cols, rope)], axis=-1).astype(BF16)
    n_odd = mla_w_uq.shape[0]
    uq = mla_w_uq.reshape(n_odd, q_lora, mla_heads, nope + rope)
    wq = jnp.concatenate([uq, _rotate_half_columns(uq[..., nope:], rope)], axis=-1)
    wq = wq.reshape(n_odd, q_lora, mla_heads * 2 * nope).astype(BF16)
    ukv = mla_w_ukv.reshape(n_odd, kv_lora, mla_heads, nope + v_dim)
    wk = ukv[..., :nope].reshape(n_odd, kv_lora, mla_heads * nope).astype(BF16)
    wv = ukv[..., nope:].reshape(n_odd, kv_lora, mla_heads * v_dim).astype(BF16)
    w_out1 = mla_w_out.astype(BF16)
    qn = mla_q_norm.reshape(n_odd, 1, q_lora)
    kvn = mla_kv_norm.reshape(n_odd, 1, kv_lora)

    n_cond = 1 + dec_batch
    cond = jnp.concatenate([c_ctx[None, :], c, jnp.zeros((-n_cond % 16, d), F32)], axis=0)
    mod = _mod_call(cond, w_mod, b_mod).reshape(depth, cond.shape[0], 9, d)

    rope_tab = _rope_table(dec_seq, rope)
    ident_tab = _identity_rope_table(min(256, seq), rope)

    def trunk(x3, mod_rows, per_cond, caches):
        bsz, l, _ = x3.shape
        x = x3.reshape(bsz * l, d)
        states = []
        for layer in range(depth):
            m = mod_rows[layer]
            g = lambda s: ln_g[layer, s].reshape(1, d)
            bb = lambda s: ln_b[layer, s].reshape(1, d)
            x = _ffn_call(x, m, per_cond, w1, w3, w2, layer, 0, g(0), bb(0), 0, alpha)
            i = layer // 2
            if layer % 2 == 0:
                outs = _na_proj_call(x, m, per_cond, w_in, i, na_width, float(na_dh ** -0.5) * LOG2E,
                                     caches is None)
                proj = outs[0].reshape(bsz, l, -1)
                if caches is None:
                    states.append((outs[1], outs[2]))
                    a = _attn_call(proj, 0, na_dh, [(proj, na_heads, proj, 2 * na_heads)],
                                   na_heads, na_dh, na_dh, 256, 512, "na_dense_attn")
                else:
                    kc = caches[0][:, i].reshape(bsz, -1, na_width).astype(BF16)
                    vc = caches[1][:, i].reshape(bsz, -1, na_width).astype(BF16)
                    bias = _na_bias_table(na_rpb[i], l // GRID_W)
                    a = _na_call(proj, kc, vc, bias, na_heads, na_dh, (na_rpb.shape[2] + 1) // 2)
                pooled = _pool_call(proj, 3 * na_width // pool_ch, w_pool, p_scale, i)
                x = _out_call(x, m, per_cond,
                              [(a.reshape(bsz * l, na_width), w_out0, (None, na_width, d), (i, 0, 0)),
                               (pooled.reshape(bsz * l, na_width), w_out0, (None, na_width, d), (i, 1, 0))],
                              g(1), bb(1), alpha)
            else:
                tab = ident_tab if caches is None else rope_tab
                outs = _mla_proj_call(x, m, per_cond, wd, qn, kvn, wq, wk, wv, tab, i, mla_dims,
                                      caches is None)
                q = outs[0].reshape(bsz, l, -1)
                k = outs[1].reshape(bsz, l, -1)
                v = outs[2].reshape(bsz, l, -1)
                segments = [(k, 0, v, 0)]
                if caches is None:
                    states.append((outs[3], outs[4]))
                else:
                    ckv_c = caches[2][:, i].reshape(-1, kv_lora)
                    kpe_c = caches[3][:, i].reshape(-1, rope)
                    kpe_c = jnp.concatenate([kpe_c, jnp.zeros_like(kpe_c)], axis=-1)
                    k_c, v_c = _mla_expand_call(ckv_c, kpe_c, wk, wv, i, mla_heads, nope)
                    segments.append((k_c.reshape(bsz, -1, k.shape[-1]), 0,
                                     v_c.reshape(bsz, -1, v.shape[-1]), 0))
                o = _attn_call(q, 0, 2 * nope, segments, mla_heads, v_dim, 2 * v_dim, 1024, 512,
                               "mla_attn")
                x = _out_call(x, m, per_cond,
                              [(o.reshape(bsz * l, -1), w_out1, (None,) + w_out1.shape[1:], (i, 0, 0))],
                              g(1), bb(1), alpha)
            x = _ffn_call(x, m, per_cond, w1, w3, w2, layer, 1, g(2), bb(2), 2, alpha)
        return x.reshape(bsz, l, d), states

    y_prompt, st = trunk(x_prompt, mod[:, 0:1], batch * seq, None)
    y_sample, _ = trunk(x_sample, mod[:, 1:n_cond], dec_seq,
                        (cache_na_k, cache_na_v, cache_mla_ckv, cache_mla_kpe))
    even = [s for layer, s in enumerate(st) if layer % 2 == 0]
    odd = [s for layer, s in enumerate(st) if layer % 2 == 1]
    new_na_k = jnp.stack([s[0].reshape(batch, seq, na_heads, na_dh) for s in even], axis=1)
    new_na_v = jnp.stack([s[1].reshape(batch, seq, na_heads, na_dh) for s in even], axis=1)
    new_mla_ckv = jnp.stack([s[0].reshape(batch, seq, kv_lora) for s in odd], axis=1)
    new_mla_kpe = jnp.stack([s[1].reshape(batch, seq, rope) for s in odd], axis=1)
    return (y_prompt, y_sample, new_na_k, new_na_v, new_mla_ckv, new_mla_kpe)
```

```python
import functools
import math

import numpy as np
import jax
import jax.numpy as jnp
from jax import lax
from jax.experimental import pallas as pl
from jax.experimental.pallas import tpu as pltpu

GRID_W = 64
LN_EPS = 1e-5
RMS_EPS = 1e-6
POOL_WINDOWS = (2, 4, 8, 16)
ROPE_BASE = 10000.0
MASKED = -1e30
LOG2E = math.log2(math.e)
FFN_CHUNK = 512
ATTN_ROWS = 256
NA_GROUP_ROWS = 4
NA_WINDOW_ROWS = 12
V7X_VMEM_LIMIT_BYTES = 56 * 1024 * 1024

F32 = jnp.float32
BF16 = jnp.bfloat16
_NT = (((1,), (1,)), ((), ()))


def _params(*semantics):
    return pltpu.CompilerParams(dimension_semantics=semantics,
                                vmem_limit_bytes=V7X_VMEM_LIMIT_BYTES)


def _dot(a, b):
    return jnp.dot(a, b, preferred_element_type=F32)


def _post_norm(x, update, g, b, alpha):
    z = alpha * x + update
    mu = jnp.mean(z, axis=-1, keepdims=True)
    zc = z - mu
    var = jnp.mean(zc * zc, axis=-1, keepdims=True)
    return zc * lax.rsqrt(var + LN_EPS) * g + b


def _modulated(x, m, j):
    return x * (1.0 + m[3 * j + 1:3 * j + 2]) + m[3 * j:3 * j + 1]


def _mod_kernel(c_ref, w_ref, b_ref, o_ref):
    c = c_ref[...]
    h = (c * jax.nn.sigmoid(c)).astype(BF16)
    o_ref[...] = _dot(h, w_ref[...].astype(BF16)) + b_ref[...]


def _mod_call(cond, w_mod, b_mod, tn=1024):
    depth, d, n = w_mod.shape
    r = cond.shape[0]
    return pl.pallas_call(
        _mod_kernel,
        out_shape=jax.ShapeDtypeStruct((depth, r, n), F32),
        grid=(depth, n // tn),
        in_specs=[pl.BlockSpec((r, d), lambda l, j: (0, 0)),
                  pl.BlockSpec((None, d, tn), lambda l, j: (l, 0, j)),
                  pl.BlockSpec((None, 1, tn), lambda l, j: (l, 0, j))],
        out_specs=pl.BlockSpec((None, r, tn), lambda l, j: (l, 0, j)),
        compiler_params=_params("parallel", "parallel"),
        name="mod_proj",
    )(cond, w_mod, b_mod.reshape(depth, 1, n))


def _ffn_kernel(x_ref, mod_ref, w1_ref, w3_ref, w2_ref, g_ref, b_ref, o_ref, h_ref, *, j, alpha, rows,
                mid_rows):
    f = pl.program_id(1)
    last = pl.num_programs(1) - 1
    tm = x_ref.shape[0]

    def chunk_update(h):
        a = _dot(h, w1_ref[...])
        b = _dot(h, w3_ref[...])
        act = (a * jax.nn.sigmoid(a) * b).astype(BF16)
        return _dot(act, w2_ref[...])

    @pl.when(f == 0)
    def _():
        m = mod_ref[...]
        for r0 in range(0, tm, rows):
            h = _modulated(x_ref[r0:r0 + rows, :], m, j).astype(BF16)
            h_ref[r0:r0 + rows, :] = h
            o_ref[r0:r0 + rows, :] = chunk_update(h)

    @pl.when((f > 0) & (f < last))
    def _():
        for r0 in range(0, tm, mid_rows):
            o_ref[r0:r0 + mid_rows, :] += chunk_update(h_ref[r0:r0 + mid_rows, :])

    @pl.when(f == last)
    def _():
        half_gate = 0.5 * mod_ref[...][3 * j + 2:3 * j + 3]
        for r0 in range(0, tm, rows):
            y = o_ref[r0:r0 + rows, :] + chunk_update(h_ref[r0:r0 + rows, :])
            o_ref[r0:r0 + rows, :] = _post_norm(x_ref[r0:r0 + rows, :], half_gate * y,
                                                g_ref[...], b_ref[...], alpha)


def _ffn_chunked(w, tf):
    *lead, d, ff = w.shape
    tf = min(tf, ff)
    assert ff % tf == 0 and ff // tf >= 2
    w = w.reshape(*lead, d, ff // tf, tf)
    return jnp.swapaxes(w, -3, -2).astype(BF16)


def _ffn_call(x, mod, tokens_per_cond, w1, w3, w2, l, s, ln_g, ln_b, j, alpha, tm=1024, rows=256,
              mid_rows=512):
    t, d = x.shape
    n_chunks, _, tf = w1.shape[2:]
    tm = min(tm, tokens_per_cond)
    assert tokens_per_cond % tm == 0
    per = tokens_per_cond // tm
    return pl.pallas_call(
        functools.partial(_ffn_kernel, j=j, alpha=alpha, rows=min(rows, tm), mid_rows=min(mid_rows, tm)),
        out_shape=jax.ShapeDtypeStruct((t, d), F32),
        grid=(t // tm, n_chunks),
        in_specs=[pl.BlockSpec((tm, d), lambda i, f: (i, 0)),
                  pl.BlockSpec((None, 9, d), lambda i, f: (i // per, 0, 0)),
                  pl.BlockSpec((None, None, None, d, tf), lambda i, f: (l, s, f, 0, 0)),
                  pl.BlockSpec((None, None, None, d, tf), lambda i, f: (l, s, f, 0, 0)),
                  pl.BlockSpec((None, None, tf, d), lambda i, f: (l, s, f, 0)),
                  pl.BlockSpec((1, d), lambda i, f: (0, 0)),
                  pl.BlockSpec((1, d), lambda i, f: (0, 0))],
        out_specs=pl.BlockSpec((tm, d), lambda i, f: (i, 0)),
        scratch_shapes=[pltpu.VMEM((tm, d), BF16)],
        compiler_params=_params("parallel", "arbitrary"),
        name="ffn",
    )(x, mod, w1, w3, w2, ln_g, ln_b)


def _na_proj_kernel(x_ref, mod_ref, w_ref, qkv_ref, u_ref, *kv_refs, q_scale, rows):
    m = mod_ref[...]
    n_heads, dh = qkv_ref.shape[0] // 3, qkv_ref.shape[-1]
    n_pool, ch = u_ref.shape[0], u_ref.shape[-1]
    width = n_heads * dh
    for r0 in range(0, x_ref.shape[0], rows):
        r = slice(r0, r0 + rows)
        h = _modulated(x_ref[r, :], m, 1).astype(BF16)
        for n in range(3):
            y = _dot(h, w_ref[:, n * width:(n + 1) * width])
            if kv_refs and n > 0:
                kv_refs[n - 1][r, :] = y
            y = (y * q_scale if n == 0 else y).astype(BF16)
            for hd in range(n_heads):
                qkv_ref[n * n_heads + hd, r, :] = y[:, hd * dh:(hd + 1) * dh]
        y = _dot(h, w_ref[:, 3 * width:]).astype(BF16)
        for g in range(n_pool):
            u_ref[g, r, :] = y[:, g * ch:(g + 1) * ch]


def _resident(block_shape, index):
    return pl.BlockSpec(block_shape, lambda *_: index, pipeline_mode=pl.Buffered(1))


def _na_proj_call(x, mod, tokens_per_cond, w_in, i_even, n_heads, dh, n_pool, ch, q_scale, with_kv,
                  tm=512, rows=256):
    t, d = x.shape
    n_in = w_in.shape[-1]
    na_width = n_heads * dh
    assert n_in == 3 * na_width + n_pool * ch
    tm = min(tm, tokens_per_cond)
    per = tokens_per_cond // tm
    out_shape = [jax.ShapeDtypeStruct((3 * n_heads, t, dh), BF16),
                 jax.ShapeDtypeStruct((n_pool, t, ch), BF16)]
    out_specs = [pl.BlockSpec((3 * n_heads, tm, dh), lambda i: (0, i, 0)),
                 pl.BlockSpec((n_pool, tm, ch), lambda i: (0, i, 0))]
    if with_kv:
        out_shape += [jax.ShapeDtypeStruct((t, na_width), F32)] * 2
        out_specs += [pl.BlockSpec((tm, na_width), lambda i: (i, 0))] * 2
    return pl.pallas_call(
        functools.partial(_na_proj_kernel, q_scale=q_scale, rows=min(rows, tm)),
        out_shape=out_shape,
        grid=(t // tm,),
        in_specs=[pl.BlockSpec((tm, d), lambda i: (i, 0)),
                  pl.BlockSpec((None, 9, d), lambda i: (i // per, 0, 0)),
                  _resident((None, d, n_in), (i_even, 0, 0))],
        out_specs=out_specs,
        compiler_params=_params("parallel"),
        name="na_proj",
    )(x, mod, w_in)


def _attn_rows(q, kv_refs, chunk, dv):
    rows = q.shape[0]
    den_in_v = kv_refs[1].shape[-1] == 2 * dv
    m = jnp.full((rows, 1), MASKED, F32)
    acc = jnp.zeros((rows, kv_refs[1].shape[-1]), F32)
    den = jnp.zeros((rows, 1), F32)
    for k_ref, v_ref in zip(kv_refs[0::2], kv_refs[1::2]):
        lk = k_ref.shape[0]
        step = min(chunk, lk)
        for c0 in range(0, lk, step):
            s = lax.dot_general(q, k_ref[c0:c0 + step, :], _NT, preferred_element_type=F32)
            m_new = jnp.maximum(m, jnp.max(s, axis=-1, keepdims=True))
            rescale = jnp.exp2(m - m_new)
            p = jnp.exp2(s - m_new)
            if not den_in_v:
                den = rescale * den + jnp.sum(p, axis=-1, keepdims=True)
            acc = rescale * acc + _dot(p.astype(BF16), v_ref[c0:c0 + step, :])
            m = m_new
    return acc[:, :dv] / (acc[:, dv:] if den_in_v else den)


def _attn_kernel(*refs, chunk, dv, rows):
    q_ref, o_ref = refs[0], refs[-1]
    for hd in range(q_ref.shape[0]):
        kv_refs = [ref.at[hd] for ref in refs[1:-1]]
        for r0 in range(0, q_ref.shape[1], rows):
            o = _attn_rows(q_ref[hd, r0:r0 + rows, :], kv_refs, chunk, dv)
            o_ref[r0:r0 + rows, hd * dv:(hd + 1) * dv] = o.astype(o_ref.dtype)


def _attn_call(q, q_head0, segments, n_heads, dv, tq, chunk, name, heads=1):
    _, b, l, dq = q.shape
    tq = min(tq, l)
    heads = math.gcd(heads, n_heads, q_head0, *[h0 for _, k0, _, v0 in segments for h0 in (k0, v0)])
    in_specs = [pl.BlockSpec((heads, None, tq, dq),
                             lambda bi, h, qi: (q_head0 // heads + h, bi, qi, 0))]
    args = [q]
    for k, k0, v, v0 in segments:
        lk = k.shape[2]
        assert lk % min(chunk, lk) == 0
        in_specs.append(pl.BlockSpec((heads, None, lk, dq),
                                     lambda bi, h, qi, k0=k0: (k0 // heads + h, bi, 0, 0)))
        in_specs.append(pl.BlockSpec((heads, None, lk, v.shape[-1]),
                                     lambda bi, h, qi, v0=v0: (v0 // heads + h, bi, 0, 0)))
        args += [k, v]
    return pl.pallas_call(
        functools.partial(_attn_kernel, chunk=chunk, dv=dv, rows=min(ATTN_ROWS, tq)),
        out_shape=jax.ShapeDtypeStruct((b, l, n_heads * dv), BF16),
        grid=(b, n_heads // heads, l // tq),
        in_specs=in_specs,
        out_specs=pl.BlockSpec((None, tq, heads * dv), lambda bi, h, qi: (bi, qi, h)),
        compiler_params=_params("parallel", "parallel", "arbitrary"),
        name=name,
    )(*args)


def _na_kernel(q_ref, k_ref, v_ref, kc_ref, vc_ref, bias_ref, o_ref, *, n_rows, half_rows, groups):
    n_groups = n_rows // NA_GROUP_ROWS
    tq = NA_GROUP_ROWS * GRID_W
    win = NA_WINDOW_ROWS * GRID_W
    for c in range(groups):
        g = pl.program_id(2) * groups + c
        base_row = jnp.clip(g * NA_GROUP_ROWS - half_rows, 0, n_rows - NA_WINDOW_ROWS)
        base = pl.multiple_of(base_row * GRID_W, GRID_W)
        variant = jnp.where(g == 0, 0, jnp.where(g == n_groups - 1, 2, 1))
        q = q_ref[c * tq:(c + 1) * tq, :]
        kw = k_ref[pl.ds(base, win), :]
        vw = v_ref[pl.ds(base, win), :]
        s_loc = lax.dot_general(q, kw, _NT, preferred_element_type=F32) + bias_ref[variant]
        s_ctx = lax.dot_general(q, kc_ref[...], _NT, preferred_element_type=F32)
        m = jnp.maximum(jnp.max(s_loc, axis=-1, keepdims=True),
                        jnp.max(s_ctx, axis=-1, keepdims=True))
        p_loc = jnp.exp2(s_loc - m)
        p_ctx = jnp.exp2(s_ctx - m)
        den = jnp.sum(p_loc, axis=-1, keepdims=True) + jnp.sum(p_ctx, axis=-1, keepdims=True)
        o = _dot(p_loc.astype(BF16), vw) + _dot(p_ctx.astype(BF16), vc_ref[...])
        o_ref[c * tq:(c + 1) * tq, :] = (o / den).astype(o_ref.dtype)


def _na_group_rows(g, n_rows, na_rows):
    kr = min(na_rows, n_rows)
    base = int(np.clip(g * NA_GROUP_ROWS - kr // 2, 0, n_rows - NA_WINDOW_ROWS))
    rq = (g * NA_GROUP_ROWS + np.arange(NA_GROUP_ROWS))[:, None]
    rk = (base + np.arange(NA_WINDOW_ROWS))[None, :]
    rs = np.clip(rq - kr // 2, 0, n_rows - kr)
    ok = (rk >= rs) & (rk < rs + kr)
    assert (ok.sum(axis=1) == kr).all(), "key window misses part of a neighbourhood"
    return np.where(ok, rk - rq + na_rows - 1, 0), ok


def _na_bias_table(rpb, n_rows):
    n_heads, nr2, nc2 = rpb.shape
    na_rows, na_cols = (nr2 + 1) // 2, (nc2 + 1) // 2
    n_groups = n_rows // NA_GROUP_ROWS
    per_group = [_na_group_rows(g, n_rows, na_rows) for g in range(n_groups)]
    for g in range(2, n_groups - 1):
        assert all((a == b).all() for a, b in zip(per_group[g], per_group[1])), "interior groups differ"
    idx_r, row_ok = (np.stack([per_group[g][i] for g in (0, 1, n_groups - 1)]) for i in range(2))
    w = GRID_W
    e = jnp.pad(rpb, ((0, 0), (0, 0), (w - na_cols, 2 * w - (w - na_cols) - nc2)))
    band = jnp.tile(e, (1, 1, w))[..., :w * (2 * w - 1)].reshape(n_heads, nr2, w, 2 * w - 1)[..., w - 1:]
    cq, ck = np.arange(w)[:, None], np.arange(w)[None, :]
    cs = np.clip(cq - na_cols // 2, 0, w - na_cols)
    col_ok = (ck >= cs) & (ck < cs + na_cols)
    blocks = jnp.take(band, jnp.asarray(idx_r.reshape(-1)), axis=1)
    blocks = blocks.reshape((n_heads,) + idx_r.shape + (w, w))
    ok = row_ok[:, :, :, None, None] & col_ok[None, None, None]
    table = jnp.where(ok[None], blocks * LOG2E, MASKED).transpose(0, 1, 2, 4, 3, 5)
    return table.reshape(n_heads, 3, NA_GROUP_ROWS * w, NA_WINDOW_ROWS * w).astype(F32)


def _na_call(qkv, kc, vc, bias, na_rows, groups=4):
    n_heads, b, l, dh = qkv.shape[0] // 3, *qkv.shape[1:]
    n_rows = l // GRID_W
    assert n_rows % NA_GROUP_ROWS == 0 and n_rows >= NA_WINDOW_ROWS + NA_GROUP_ROWS
    assert NA_GROUP_ROWS - 1 + min(na_rows, n_rows) <= NA_WINDOW_ROWS
    n_groups = n_rows // NA_GROUP_ROWS
    groups = math.gcd(n_groups, groups)
    tq = groups * NA_GROUP_ROWS * GRID_W
    lc = kc.shape[1]
    return pl.pallas_call(
        functools.partial(_na_kernel, n_rows=n_rows, half_rows=min(na_rows, n_rows) // 2,
                          groups=groups),
        out_shape=jax.ShapeDtypeStruct((b, l, n_heads * dh), BF16),
        grid=(b, n_heads, n_groups // groups),
        in_specs=[pl.BlockSpec((None, None, tq, dh), lambda bi, h, g: (h, bi, g, 0)),
                  pl.BlockSpec((None, None, l, dh), lambda bi, h, g: (n_heads + h, bi, 0, 0)),
                  pl.BlockSpec((None, None, l, dh), lambda bi, h, g: (2 * n_heads + h, bi, 0, 0)),
                  pl.BlockSpec((None, lc, dh), lambda bi, h, g: (bi, 0, h)),
                  pl.BlockSpec((None, lc, dh), lambda bi, h, g: (bi, 0, h)),
                  pl.BlockSpec((None,) + bias.shape[1:], lambda bi, h, g: (h, 0, 0, 0))],
        out_specs=pl.BlockSpec((None, tq, dh), lambda bi, h, g: (bi, g, h)),
        compiler_params=_params("parallel", "parallel", "arbitrary"),
        name="na_attn",
    )(qkv, qkv, qkv, kc, vc, bias)


def _pool_kernel(u_ref, w_ref, s_ref, o_ref, *, chunk, window):
    l = u_ref.shape[0]
    half = lax.shift_left(jnp.int32(1), pl.program_id(1).astype(jnp.int32))
    w = w_ref[...]
    scale = s_ref[...]

    def body(c, carry):
        t0 = pl.multiple_of(c * chunk, chunk)
        s0 = pl.multiple_of(jnp.clip(t0 - (window - chunk) // 2, 0, l - window), 16)
        t = t0 + lax.broadcasted_iota(jnp.int32, (chunk, 1), 0)
        lo = jnp.maximum(t - half, 0)
        hi = jnp.minimum(t + half, l)
        pos = s0 + lax.broadcasted_iota(jnp.int32, (1, window), 1)
        band = jnp.where((pos >= lo) & (pos < hi), 1.0, 0.0).astype(BF16)
        sums = _dot(band, u_ref[pl.ds(s0, window), :])
        mean = sums / (hi - lo).astype(F32)
        d = (mean - u_ref[pl.ds(t0, chunk), :].astype(F32)).astype(BF16)
        o_ref[pl.ds(t0, chunk), :] = (_dot(d, w) * scale).astype(o_ref.dtype)
        return carry

    lax.fori_loop(0, l // chunk, body, 0, unroll=min(4, l // chunk))


def _pool_call(u, w_pool, pool_scale, i_even):
    _, b, l, _ = u.shape
    n_groups, ch, _ = w_pool.shape[1:]
    assert n_groups == len(POOL_WINDOWS) and all(w == 2 << g for g, w in enumerate(POOL_WINDOWS))
    chunk = min(256, l)
    window = min(2 * chunk, l)
    assert l % chunk == 0 and (window == l or window - chunk >= max(POOL_WINDOWS))
    return pl.pallas_call(
        functools.partial(_pool_kernel, chunk=chunk, window=window),
        out_shape=jax.ShapeDtypeStruct((b, l, n_groups * ch), BF16),
        grid=(b, n_groups),
        in_specs=[pl.BlockSpec((None, None, l, ch), lambda bi, g: (g, bi, 0, 0)),
                  pl.BlockSpec((None, None, ch, ch), lambda bi, g: (i_even, g, 0, 0)),
                  pl.BlockSpec((None, 1, ch), lambda bi, g: (i_even, 0, g))],
        out_specs=pl.BlockSpec((None, l, ch), lambda bi, g: (bi, 0, g)),
        compiler_params=_params("parallel", "parallel"),
        name="pool",
    )(u, w_pool, pool_scale)


def _out_kernel(*refs, n_lhs, alpha, rows):
    x_ref, mod_ref = refs[0], refs[1]
    g_ref, b_ref, o_ref = refs[-3], refs[-2], refs[-1]
    gate = mod_ref[...][5:6]
    for r0 in range(0, x_ref.shape[0], rows):
        r = slice(r0, r0 + rows)
        y = functools.reduce(jnp.add, [_dot(refs[2 + 2 * i][r, :], refs[3 + 2 * i][...])
                                       for i in range(n_lhs)])
        o_ref[r, :] = _post_norm(x_ref[r, :], gate * y, g_ref[...], b_ref[...], alpha)


def _out_call(x, mod, tokens_per_cond, lhs_w, ln_g, ln_b, alpha, tm=1024, rows=256):
    t, d = x.shape
    tm = min(tm, tokens_per_cond)
    per = tokens_per_cond // tm
    in_specs = [pl.BlockSpec((tm, d), lambda i: (i, 0)),
                pl.BlockSpec((None, 9, d), lambda i: (i // per, 0, 0))]
    args = [x, mod]
    for lhs, w, w_block, w_index in lhs_w:
        in_specs.append(pl.BlockSpec((tm, lhs.shape[1]), lambda i: (i, 0)))
        in_specs.append(_resident(w_block, w_index))
        args += [lhs, w]
    in_specs += [pl.BlockSpec((1, d), lambda i: (0, 0))] * 2
    args += [ln_g, ln_b]
    return pl.pallas_call(
        functools.partial(_out_kernel, n_lhs=len(lhs_w), alpha=alpha, rows=min(rows, tm)),
        out_shape=jax.ShapeDtypeStruct((t, d), F32),
        grid=(t // tm,),
        in_specs=in_specs,
        out_specs=pl.BlockSpec((tm, d), lambda i: (i, 0)),
        compiler_params=_params("parallel"),
        name="out_proj",
    )(*args)


def _rms(x, g):
    return x * lax.rsqrt(jnp.mean(x * x, axis=-1, keepdims=True) + RMS_EPS) * g


def _rotary_pair(t, tab):
    u = t * tab
    return u + pltpu.roll(u, u.shape[-1] // 2, 1)


def _rope_half_mask(shape):
    return lax.broadcasted_iota(jnp.int32, shape, 1) < shape[-1] // 2


def _head_store(ref, h, r, lane0, value):
    ref[h, r, lane0:lane0 + value.shape[-1]] = value.astype(ref.dtype)


def _store_values_with_ones(v_ref, r, v, dv):
    ones = jnp.ones((v.shape[0], dv), v_ref.dtype)
    for hd in range(v_ref.shape[0]):
        _head_store(v_ref, hd, r, 0, v[:, hd * dv:(hd + 1) * dv])
        _head_store(v_ref, hd, r, dv, ones)


def _mla_proj_kernel(x_ref, mod_ref, wd_ref, qn_ref, kvn_ref, wq_ref, wk_ref, wv_ref, tab_ref,
                     q_ref, k_ref, v_ref, *state_refs, n_heads, q_lora, kv_lora, nope, q_scale, rows):
    m = mod_ref[...]
    dq = 2 * nope
    for r0 in range(0, x_ref.shape[0], rows):
        r = slice(r0, r0 + rows)
        h = _modulated(x_ref[r, :], m, 1).astype(BF16)
        down = _dot(h, wd_ref[...])
        cq = _rms(down[:, :q_lora], qn_ref[...])
        ckv = _rms(down[:, q_lora:q_lora + kv_lora], kvn_ref[...])
        pe = down[:, q_lora + kv_lora:]
        tab = tab_ref[r, :]
        if state_refs:
            state_refs[0][r, :] = ckv
            state_refs[1][r, :] = pe[:, :pe.shape[-1] // 2]
        kpe = jnp.where(_rope_half_mask(pe.shape), _rotary_pair(pe, tab), 0.0).astype(BF16)
        q = _dot(cq.astype(BF16), wq_ref[...]) * q_scale
        ckv_b = ckv.astype(BF16)
        kn = _dot(ckv_b, wk_ref[...])
        _store_values_with_ones(v_ref, r, _dot(ckv_b, wv_ref[...]), nope)
        for hd in range(n_heads):
            _head_store(q_ref, hd, r, 0, q[:, hd * dq:hd * dq + nope])
            _head_store(q_ref, hd, r, nope, _rotary_pair(q[:, hd * dq + nope:(hd + 1) * dq], tab))
            _head_store(k_ref, hd, r, 0, kn[:, hd * nope:(hd + 1) * nope])
            _head_store(k_ref, hd, r, nope, kpe)


def _mla_proj_call(x, mod, tokens_per_cond, wd, qn, kvn, wq, wk, wv, tab, i_odd, dims, with_state,
                   tm=512, rows=256):
    n_heads, q_lora, kv_lora, nope, rope, q_scale = dims
    t, d = x.shape
    tm = min(tm, tokens_per_cond, tab.shape[0])
    per = tokens_per_cond // tm
    tab_blocks = tab.shape[0] // tm
    dq = 2 * nope
    out_shape = [jax.ShapeDtypeStruct((n_heads, t, dq), BF16)] * 3
    out_specs = [pl.BlockSpec((n_heads, tm, dq), lambda i: (0, i, 0))] * 3
    if with_state:
        out_shape += [jax.ShapeDtypeStruct((t, kv_lora), F32), jax.ShapeDtypeStruct((t, rope), F32)]
        out_specs += [pl.BlockSpec((tm, kv_lora), lambda i: (i, 0)),
                      pl.BlockSpec((tm, rope), lambda i: (i, 0))]
    whole = lambda a: _resident((None,) + a.shape[1:], (i_odd,) + (0,) * (a.ndim - 1))
    return pl.pallas_call(
        functools.partial(_mla_proj_kernel, n_heads=n_heads, q_lora=q_lora, kv_lora=kv_lora,
                          nope=nope, q_scale=q_scale, rows=min(rows, tm)),
        out_shape=out_shape,
        grid=(t // tm,),
        in_specs=[pl.BlockSpec((tm, d), lambda i: (i, 0)),
                  pl.BlockSpec((None, 9, d), lambda i: (i // per, 0, 0)),
                  whole(wd), whole(qn), whole(kvn), whole(wq), whole(wk), whole(wv),
                  pl.BlockSpec((tm, tab.shape[1]), lambda i: (i % tab_blocks, 0))],
        out_specs=out_specs,
        compiler_params=_params("parallel"),
        name="mla_proj",
    )(x, mod, wd, qn, kvn, wq, wk, wv, tab)


def _mla_expand_kernel(ckv_ref, kpe_ref, wk_ref, wv_ref, k_ref, v_ref, *, n_heads, nope):
    ckv = ckv_ref[...].astype(BF16)
    kn = _dot(ckv, wk_ref[...])
    r = slice(None)
    _store_values_with_ones(v_ref, r, _dot(ckv, wv_ref[...]), nope)
    kpe = kpe_ref[...].astype(BF16)
    for hd in range(n_heads):
        _head_store(k_ref, hd, r, 0, kn[:, hd * nope:(hd + 1) * nope])
        _head_store(k_ref, hd, r, nope, kpe)


def _mla_expand_call(ckv, kpe_pad, wk, wv, i_odd, n_heads, nope, tm=256):
    t, kv_lora = ckv.shape
    tm = min(tm, t)
    dq = 2 * nope
    whole = lambda a: pl.BlockSpec((None,) + a.shape[1:], lambda i: (i_odd,) + (0,) * (a.ndim - 1))
    return pl.pallas_call(
        functools.partial(_mla_expand_kernel, n_heads=n_heads, nope=nope),
        out_shape=[jax.ShapeDtypeStruct((n_heads, t, dq), BF16)] * 2,
        grid=(t // tm,),
        in_specs=[pl.BlockSpec((tm, kv_lora), lambda i: (i, 0)),
                  pl.BlockSpec((tm, kpe_pad.shape[1]), lambda i: (i, 0)),
                  whole(wk), whole(wv)],
        out_specs=[pl.BlockSpec((n_heads, tm, dq), lambda i: (0, i, 0))] * 2,
        compiler_params=_params("parallel"),
        name="mla_expand",
    )(ckv, kpe_pad, wk, wv)


def _rotate_half_columns(w, rope):
    q = rope // 4
    parts = [w[..., i * q:(i + 1) * q] for i in range(4)]
    return jnp.concatenate([-parts[1], parts[0], -parts[3], parts[2]], axis=-1)


def _rope_table(n_tokens, rope):
    axis = rope // 2
    t = jnp.arange(n_tokens)
    inv = ROPE_BASE ** (-jnp.arange(0, axis, 2, dtype=F32) / axis)
    ang_r = (t // GRID_W).astype(F32)[:, None] * inv[None, :]
    ang_c = (t % GRID_W).astype(F32)[:, None] * inv[None, :]
    ang = jnp.concatenate([ang_r, ang_r, ang_c, ang_c], axis=-1)
    return jnp.concatenate([jnp.cos(ang), jnp.sin(ang)], axis=-1)


def _identity_rope_table(n_tokens, rope):
    return jnp.concatenate([jnp.ones((n_tokens, rope), F32), jnp.zeros((n_tokens, rope), F32)], axis=-1)


def kernel(x_prompt, x_sample, cache_na_k, cache_na_v, cache_mla_ckv, cache_mla_kpe, c, c_ctx,
           w_mod, b_mod, ln_g, ln_b, ffn_w1, ffn_w3, ffn_w2,
           na_w_in, mix0_w_out, na_rpb, pool_w, pool_scale,
           mla_w_down, mla_q_norm, mla_w_uq, mla_kv_norm, mla_w_ukv, mla_w_out):
    depth, d, _ = w_mod.shape
    alpha = (2 * depth) ** 0.25
    batch, seq, _ = x_prompt.shape
    dec_batch, dec_seq, _ = x_sample.shape
    na_heads, na_dh = cache_na_k.shape[-2:]
    na_width = na_heads * na_dh
    pool_groups, pool_ch = pool_w.shape[1:3]
    assert pool_groups * pool_ch == na_width and pool_ch % 128 == 0
    q_lora, kv_lora = mla_q_norm.shape[-1], mla_kv_norm.shape[-1]
    rope = cache_mla_kpe.shape[-1]
    uq_w, ukv_w, out_rows = mla_w_uq.shape[-1], mla_w_ukv.shape[-1], mla_w_out.shape[1]
    mla_heads = (uq_w - ukv_w + out_rows) // rope
    nope = uq_w // mla_heads - rope
    v_dim = out_rows // mla_heads
    assert nope == v_dim and 2 * rope == nope and nope % 128 == 0
    mla_dims = (mla_heads, q_lora, kv_lora, nope, rope, float((nope + rope) ** -0.5) * LOG2E)

    w1, w3, w2 = _ffn_chunked(ffn_w1, FFN_CHUNK), _ffn_chunked(ffn_w3, FFN_CHUNK), ffn_w2.astype(BF16)
    w_in = na_w_in.astype(BF16)
    w_out0 = mix0_w_out.astype(BF16)
    w_pool = pool_w.astype(BF16)
    p_scale = pool_scale.reshape(pool_scale.shape[0], 1, -1)
    kpe_cols = mla_w_down[..., q_lora + kv_lora:]
    wd = jnp.concatenate([mla_w_down, _rotate_half_columns(kpe_cols, rope)], axis=-1).astype(BF16)
    n_odd = mla_w_uq.shape[0]
    uq = mla_w_uq.reshape(n_odd, q_lora, mla_heads, nope + rope)
    wq = jnp.concatenate([uq, _rotate_half_columns(uq[..., nope:], rope)], axis=-1)
    wq = wq.reshape(n_odd, q_lora, mla_heads * 2 * nope).astype(BF16)
    ukv = mla_w_ukv.reshape(n_odd, kv_lora, mla_heads, nope + v_dim)
    wk = ukv[..., :nope].reshape(n_odd, kv_lora, mla_heads * nope).astype(BF16)
    wv = ukv[..., nope:].reshape(n_odd, kv_lora, mla_heads * v_dim).astype(BF16)
    w_out1 = mla_w_out.astype(BF16)
    qn = mla_q_norm.reshape(n_odd, 1, q_lora)
    kvn = mla_kv_norm.reshape(n_odd, 1, kv_lora)

    n_cond = 1 + dec_batch
    cond = jnp.concatenate([c_ctx[None, :], c, jnp.zeros((-n_cond % 16, d), F32)], axis=0)
    mod = _mod_call(cond, w_mod, b_mod).reshape(depth, cond.shape[0], 9, d)

    rope_tab = _rope_table(dec_seq, rope)
    ident_tab = _identity_rope_table(min(256, seq), rope)

    def trunk(x3, mod_rows, per_cond, caches):
        bsz, l, _ = x3.shape
        x = x3.reshape(bsz * l, d)
        states = []
        for layer in range(depth):
            m = mod_rows[layer]
            g = lambda s: ln_g[layer, s].reshape(1, d)
            bb = lambda s: ln_b[layer, s].reshape(1, d)
            x = _ffn_call(x, m, per_cond, w1, w3, w2, layer, 0, g(0), bb(0), 0, alpha)
            i = layer // 2
            if layer % 2 == 0:
                outs = _na_proj_call(x, m, per_cond, w_in, i, na_heads, na_dh, pool_groups, pool_ch,
                                     float(na_dh ** -0.5) * LOG2E, caches is None)
                qkv = outs[0].reshape(3 * na_heads, bsz, l, na_dh)
                u = outs[1].reshape(pool_groups, bsz, l, pool_ch)
                if caches is None:
                    states.append((outs[2], outs[3]))
                    a = _attn_call(qkv, 0, [(qkv, na_heads, qkv, 2 * na_heads)], na_heads, na_dh,
                                   256, 512, "na_dense_attn", heads=4)
                else:
                    kc = caches[0][:, i].reshape(bsz, -1, na_width).astype(BF16)
                    vc = caches[1][:, i].reshape(bsz, -1, na_width).astype(BF16)
                    bias = _na_bias_table(na_rpb[i], l // GRID_W)
                    a = _na_call(qkv, kc, vc, bias, (na_rpb.shape[2] + 1) // 2)
                pooled = _pool_call(u, w_pool, p_scale, i)
                x = _out_call(x, m, per_cond,
                              [(a.reshape(bsz * l, na_width), w_out0, (None, na_width, d), (i, 0, 0)),
                               (pooled.reshape(bsz * l, na_width), w_out0, (None, na_width, d), (i, 1, 0))],
                              g(1), bb(1), alpha)
            else:
                tab = ident_tab if caches is None else rope_tab
                outs = _mla_proj_call(x, m, per_cond, wd, qn, kvn, wq, wk, wv, tab, i, mla_dims,
                                      caches is None)
                q, k, v = (o.reshape(mla_heads, bsz, l, 2 * nope) for o in outs[:3])
                segments = [(k, 0, v, 0)]
                if caches is None:
                    states.append((outs[3], outs[4]))
                else:
                    ckv_c = caches[2][:, i].reshape(-1, kv_lora)
                    kpe_c = caches[3][:, i].reshape(-1, rope)
                    kpe_c = jnp.concatenate([kpe_c, jnp.zeros_like(kpe_c)], axis=-1)
                    k_c, v_c = _mla_expand_call(ckv_c, kpe_c, wk, wv, i, mla_heads, nope)
                    segments.append((k_c.reshape(mla_heads, bsz, -1, 2 * nope), 0,
                                     v_c.reshape(mla_heads, bsz, -1, 2 * nope), 0))
                o = _attn_call(q, 0, segments, mla_heads, v_dim, 1024, 512, "mla_attn",
                               heads=1 if l > 1024 else 4)
                x = _out_call(x, m, per_cond,
                              [(o.reshape(bsz * l, -1), w_out1, (None,) + w_out1.shape[1:], (i, 0, 0))],
                              g(1), bb(1), alpha)
            x = _ffn_call(x, m, per_cond, w1, w3, w2, layer, 1, g(2), bb(2), 2, alpha)
        return x.reshape(bsz, l, d), states

    y_prompt, st = trunk(x_prompt, mod[:, 0:1], batch * seq, None)
    y_sample, _ = trunk(x_sample, mod[:, 1:n_cond], dec_seq,
                        (cache_na_k, cache_na_v, cache_mla_ckv, cache_mla_kpe))
    even = [s for layer, s in enumerate(st) if layer % 2 == 0]
    odd = [s for layer, s in enumerate(st) if layer % 2 == 1]
    new_na_k = jnp.stack([s[0].reshape(batch, seq, na_heads, na_dh) for s in even], axis=1)
    new_na_v = jnp.stack([s[1].reshape(batch, seq, na_heads, na_dh) for s in even], axis=1)
    new_mla_ckv = jnp.stack([s[0].reshape(batch, seq, kv_lora) for s in odd], axis=1)
    new_mla_kpe = jnp.stack([s[1].reshape(batch, seq, rope) for s in odd], axis=1)
    return (y_prompt, y_sample, new_na_k, new_na_v, new_mla_ckv, new_mla_kpe)
```

```python
import functools
import math

import numpy as np
import jax
import jax.numpy as jnp
from jax import lax
from jax.experimental import pallas as pl
from jax.experimental.pallas import tpu as pltpu

GRID_W = 64
LN_EPS = 1e-5
RMS_EPS = 1e-6
POOL_WINDOWS = (2, 4, 8, 16)
ROPE_BASE = 10000.0
MASKED = -1e30
LOG2E = math.log2(math.e)
FFN_CHUNK = 512
ATTN_ROWS = 256
NA_GROUP_ROWS = 4
NA_WINDOW_ROWS = 12
V7X_VMEM_LIMIT_BYTES = 56 * 1024 * 1024

F32 = jnp.float32
BF16 = jnp.bfloat16
_NT = (((1,), (1,)), ((), ()))


def _params(*semantics):
    return pltpu.CompilerParams(dimension_semantics=semantics,
                                vmem_limit_bytes=V7X_VMEM_LIMIT_BYTES)


def _dot(a, b):
    return jnp.dot(a, b, preferred_element_type=F32)


def _post_norm(x, update, g, b, alpha):
    z = alpha * x + update
    mu = jnp.mean(z, axis=-1, keepdims=True)
    zc = z - mu
    var = jnp.mean(zc * zc, axis=-1, keepdims=True)
    return zc * lax.rsqrt(var + LN_EPS) * g + b


def _modulated(x, m, j):
    return x * (1.0 + m[3 * j + 1:3 * j + 2]) + m[3 * j:3 * j + 1]


def _mod_kernel(c_ref, w_ref, b_ref, o_ref):
    c = c_ref[...]
    h = (c * jax.nn.sigmoid(c)).astype(BF16)
    o_ref[...] = _dot(h, w_ref[...].astype(BF16)) + b_ref[...]


def _mod_call(cond, w_mod, b_mod, tn=1024):
    depth, d, n = w_mod.shape
    r = cond.shape[0]
    return pl.pallas_call(
        _mod_kernel,
        out_shape=jax.ShapeDtypeStruct((depth, r, n), F32),
        grid=(depth, n // tn),
        in_specs=[pl.BlockSpec((r, d), lambda l, j: (0, 0)),
                  pl.BlockSpec((None, d, tn), lambda l, j: (l, 0, j)),
                  pl.BlockSpec((None, 1, tn), lambda l, j: (l, 0, j))],
        out_specs=pl.BlockSpec((None, r, tn), lambda l, j: (l, 0, j)),
        compiler_params=_params("parallel", "parallel"),
        name="mod_proj",
    )(cond, w_mod, b_mod.reshape(depth, 1, n))


def _ffn_kernel(x_ref, mod_ref, w13_ref, w2_ref, g_ref, b_ref, o_ref, h_ref, *, j, alpha, rows, mid_rows):
    f = pl.program_id(1)
    last = pl.num_programs(1) - 1
    tm = x_ref.shape[0]
    tf = w2_ref.shape[0]

    def chunk_update(h):
        ab = _dot(h, w13_ref[...])
        a, b = ab[:, :tf], ab[:, tf:]
        act = (a * jax.nn.sigmoid(a) * b).astype(BF16)
        return _dot(act, w2_ref[...])

    @pl.when(f == 0)
    def _():
        m = mod_ref[...]
        for r0 in range(0, tm, rows):
            h = _modulated(x_ref[r0:r0 + rows, :], m, j).astype(BF16)
            h_ref[r0:r0 + rows, :] = h
            o_ref[r0:r0 + rows, :] = chunk_update(h)

    @pl.when((f > 0) & (f < last))
    def _():
        for r0 in range(0, tm, mid_rows):
            o_ref[r0:r0 + mid_rows, :] += chunk_update(h_ref[r0:r0 + mid_rows, :])

    @pl.when(f == last)
    def _():
        half_gate = 0.5 * mod_ref[...][3 * j + 2:3 * j + 3]
        for r0 in range(0, tm, rows):
            y = o_ref[r0:r0 + rows, :] + chunk_update(h_ref[r0:r0 + rows, :])
            o_ref[r0:r0 + rows, :] = _post_norm(x_ref[r0:r0 + rows, :], half_gate * y,
                                                g_ref[...], b_ref[...], alpha)


def _ffn_pack_kernel(w1_ref, w3_ref, o_ref):
    tf = w1_ref.shape[-1]
    o_ref[:, :tf] = w1_ref[...].astype(BF16)
    o_ref[:, tf:] = w3_ref[...].astype(BF16)


def _ffn_chunked(w1, w3, tf):
    depth, n_sub, d, ff = w1.shape
    tf = min(tf, ff)
    assert ff % tf == 0 and ff // tf >= 2
    spec = pl.BlockSpec((None, None, d, tf), lambda l, s, f: (l, s, 0, f))
    return pl.pallas_call(
        _ffn_pack_kernel,
        out_shape=jax.ShapeDtypeStruct((depth, n_sub, ff // tf, d, 2 * tf), BF16),
        grid=(depth, n_sub, ff // tf),
        in_specs=[spec, spec],
        out_specs=pl.BlockSpec((None, None, None, d, 2 * tf), lambda l, s, f: (l, s, f, 0, 0)),
        compiler_params=_params("parallel", "parallel", "parallel"),
        name="ffn_pack",
    )(w1, w3)


def _ffn_call(x, mod, tokens_per_cond, w13, w2, l, s, ln_g, ln_b, j, alpha, tm=1024, rows=256,
              mid_rows=512):
    t, d = x.shape
    n_chunks = w13.shape[2]
    tf = w13.shape[-1] // 2
    tm = min(tm, tokens_per_cond)
    assert tokens_per_cond % tm == 0
    per = tokens_per_cond // tm
    return pl.pallas_call(
        functools.partial(_ffn_kernel, j=j, alpha=alpha, rows=min(rows, tm), mid_rows=min(mid_rows, tm)),
        out_shape=jax.ShapeDtypeStruct((t, d), F32),
        grid=(t // tm, n_chunks),
        in_specs=[pl.BlockSpec((tm, d), lambda i, f: (i, 0)),
                  pl.BlockSpec((None, 9, d), lambda i, f: (i // per, 0, 0)),
                  pl.BlockSpec((None, None, None, d, 2 * tf), lambda i, f: (l, s, f, 0, 0)),
                  pl.BlockSpec((None, None, tf, d), lambda i, f: (l, s, f, 0)),
                  pl.BlockSpec((1, d), lambda i, f: (0, 0)),
                  pl.BlockSpec((1, d), lambda i, f: (0, 0))],
        out_specs=pl.BlockSpec((tm, d), lambda i, f: (i, 0)),
        scratch_shapes=[pltpu.VMEM((tm, d), BF16)],
        compiler_params=_params("parallel", "arbitrary"),
        name="ffn",
    )(x, mod, w13, w2, ln_g, ln_b)


def _na_proj_kernel(x_ref, mod_ref, w_ref, qkv_ref, u_ref, *kv_refs, q_scale, rows):
    m = mod_ref[...]
    n_heads, dh = qkv_ref.shape[0] // 3, qkv_ref.shape[-1]
    n_pool, ch = u_ref.shape[0], u_ref.shape[-1]
    width = n_heads * dh
    for r0 in range(0, x_ref.shape[0], rows):
        r = slice(r0, r0 + rows)
        h = _modulated(x_ref[r, :], m, 1).astype(BF16)
        for n in range(3):
            y = _dot(h, w_ref[:, n * width:(n + 1) * width])
            if kv_refs and n > 0:
                kv_refs[n - 1][r, :] = y
            y = (y * q_scale if n == 0 else y).astype(BF16)
            for hd in range(n_heads):
                qkv_ref[n * n_heads + hd, r, :] = y[:, hd * dh:(hd + 1) * dh]
        y = _dot(h, w_ref[:, 3 * width:]).astype(BF16)
        for g in range(n_pool):
            u_ref[g, r, :] = y[:, g * ch:(g + 1) * ch]


def _resident(block_shape, index):
    return pl.BlockSpec(block_shape, lambda *_: index, pipeline_mode=pl.Buffered(1))


def _na_proj_call(x, mod, tokens_per_cond, w_in, i_even, n_heads, dh, n_pool, ch, q_scale, with_kv,
                  tm=512, rows=256):
    t, d = x.shape
    n_in = w_in.shape[-1]
    na_width = n_heads * dh
    assert n_in == 3 * na_width + n_pool * ch
    tm = min(tm, tokens_per_cond)
    per = tokens_per_cond // tm
    out_shape = [jax.ShapeDtypeStruct((3 * n_heads, t, dh), BF16),
                 jax.ShapeDtypeStruct((n_pool, t, ch), BF16)]
    out_specs = [pl.BlockSpec((3 * n_heads, tm, dh), lambda i: (0, i, 0)),
                 pl.BlockSpec((n_pool, tm, ch), lambda i: (0, i, 0))]
    if with_kv:
        out_shape += [jax.ShapeDtypeStruct((t, na_width), F32)] * 2
        out_specs += [pl.BlockSpec((tm, na_width), lambda i: (i, 0))] * 2
    return pl.pallas_call(
        functools.partial(_na_proj_kernel, q_scale=q_scale, rows=min(rows, tm)),
        out_shape=out_shape,
        grid=(t // tm,),
        in_specs=[pl.BlockSpec((tm, d), lambda i: (i, 0)),
                  pl.BlockSpec((None, 9, d), lambda i: (i // per, 0, 0)),
                  _resident((None, d, n_in), (i_even, 0, 0))],
        out_specs=out_specs,
        compiler_params=_params("parallel"),
        name="na_proj",
    )(x, mod, w_in)


def _attn_rows(q, kv_refs, chunk, dv):
    rows = q.shape[0]
    den_in_v = kv_refs[1].shape[-1] == 2 * dv
    m = jnp.full((rows, 1), MASKED, F32)
    acc = jnp.zeros((rows, kv_refs[1].shape[-1]), F32)
    den = jnp.zeros((rows, 1), F32)
    for k_ref, v_ref in zip(kv_refs[0::2], kv_refs[1::2]):
        lk = k_ref.shape[0]
        step = min(chunk, lk)
        for c0 in range(0, lk, step):
            s = lax.dot_general(q, k_ref[c0:c0 + step, :], _NT, preferred_element_type=F32)
            m_new = jnp.maximum(m, jnp.max(s, axis=-1, keepdims=True))
            rescale = jnp.exp2(m - m_new)
            p = jnp.exp2(s - m_new)
            if not den_in_v:
                den = rescale * den + jnp.sum(p, axis=-1, keepdims=True)
            acc = rescale * acc + _dot(p.astype(BF16), v_ref[c0:c0 + step, :])
            m = m_new
    return acc[:, :dv] / (acc[:, dv:] if den_in_v else den)


def _attn_kernel(*refs, chunk, dv, rows):
    q_ref, o_ref = refs[0], refs[-1]
    for hd in range(q_ref.shape[0]):
        kv_refs = [ref.at[hd] for ref in refs[1:-1]]
        for r0 in range(0, q_ref.shape[1], rows):
            o = _attn_rows(q_ref[hd, r0:r0 + rows, :], kv_refs, chunk, dv)
            o_ref[r0:r0 + rows, hd * dv:(hd + 1) * dv] = o.astype(o_ref.dtype)


def _attn_call(q, q_head0, segments, n_heads, dv, tq, chunk, name, heads=1):
    _, b, l, dq = q.shape
    tq = min(tq, l)
    heads = math.gcd(heads, n_heads, q_head0, *[h0 for _, k0, _, v0 in segments for h0 in (k0, v0)])
    in_specs = [pl.BlockSpec((heads, None, tq, dq),
                             lambda bi, h, qi: (q_head0 // heads + h, bi, qi, 0))]
    args = [q]
    for k, k0, v, v0 in segments:
        lk = k.shape[2]
        assert lk % min(chunk, lk) == 0
        in_specs.append(pl.BlockSpec((heads, None, lk, dq),
                                     lambda bi, h, qi, k0=k0: (k0 // heads + h, bi, 0, 0)))
        in_specs.append(pl.BlockSpec((heads, None, lk, v.shape[-1]),
                                     lambda bi, h, qi, v0=v0: (v0 // heads + h, bi, 0, 0)))
        args += [k, v]
    return pl.pallas_call(
        functools.partial(_attn_kernel, chunk=chunk, dv=dv, rows=min(ATTN_ROWS, tq)),
        out_shape=jax.ShapeDtypeStruct((b, l, n_heads * dv), BF16),
        grid=(b, n_heads // heads, l // tq),
        in_specs=in_specs,
        out_specs=pl.BlockSpec((None, tq, heads * dv), lambda bi, h, qi: (bi, qi, h)),
        compiler_params=_params("parallel", "parallel", "arbitrary"),
        name=name,
    )(*args)


def _na_kernel(q_ref, k_ref, v_ref, kc_ref, vc_ref, bias_ref, o_ref, *, n_rows, half_rows, groups):
    n_groups = n_rows // NA_GROUP_ROWS
    tq = NA_GROUP_ROWS * GRID_W
    win = NA_WINDOW_ROWS * GRID_W
    for c in range(groups):
        g = pl.program_id(2) * groups + c
        base_row = jnp.clip(g * NA_GROUP_ROWS - half_rows, 0, n_rows - NA_WINDOW_ROWS)
        base = pl.multiple_of(base_row * GRID_W, GRID_W)
        variant = jnp.where(g == 0, 0, jnp.where(g == n_groups - 1, 2, 1))
        q = q_ref[c * tq:(c + 1) * tq, :]
        kw = k_ref[pl.ds(base, win), :]
        vw = v_ref[pl.ds(base, win), :]
        s_loc = lax.dot_general(q, kw, _NT, preferred_element_type=F32) + bias_ref[variant]
        s_ctx = lax.dot_general(q, kc_ref[...], _NT, preferred_element_type=F32)
        m = jnp.maximum(jnp.max(s_loc, axis=-1, keepdims=True),
                        jnp.max(s_ctx, axis=-1, keepdims=True))
        p_loc = jnp.exp2(s_loc - m)
        p_ctx = jnp.exp2(s_ctx - m)
        den = jnp.sum(p_loc, axis=-1, keepdims=True) + jnp.sum(p_ctx, axis=-1, keepdims=True)
        o = _dot(p_loc.astype(BF16), vw) + _dot(p_ctx.astype(BF16), vc_ref[...])
        o_ref[c * tq:(c + 1) * tq, :] = (o / den).astype(o_ref.dtype)


def _na_group_rows(g, n_rows, na_rows):
    kr = min(na_rows, n_rows)
    base = int(np.clip(g * NA_GROUP_ROWS - kr // 2, 0, n_rows - NA_WINDOW_ROWS))
    rq = (g * NA_GROUP_ROWS + np.arange(NA_GROUP_ROWS))[:, None]
    rk = (base + np.arange(NA_WINDOW_ROWS))[None, :]
    rs = np.clip(rq - kr // 2, 0, n_rows - kr)
    ok = (rk >= rs) & (rk < rs + kr)
    assert (ok.sum(axis=1) == kr).all(), "key window misses part of a neighbourhood"
    return np.where(ok, rk - rq + na_rows - 1, 0), ok


def _na_bias_table(rpb, n_rows):
    n_heads, nr2, nc2 = rpb.shape
    na_rows, na_cols = (nr2 + 1) // 2, (nc2 + 1) // 2
    n_groups = n_rows // NA_GROUP_ROWS
    per_group = [_na_group_rows(g, n_rows, na_rows) for g in range(n_groups)]
    for g in range(2, n_groups - 1):
        assert all((a == b).all() for a, b in zip(per_group[g], per_group[1])), "interior groups differ"
    idx_r, row_ok = (np.stack([per_group[g][i] for g in (0, 1, n_groups - 1)]) for i in range(2))
    w = GRID_W
    e = jnp.pad(rpb, ((0, 0), (0, 0), (w - na_cols, 2 * w - (w - na_cols) - nc2)))
    band = jnp.tile(e, (1, 1, w))[..., :w * (2 * w - 1)].reshape(n_heads, nr2, w, 2 * w - 1)[..., w - 1:]
    cq, ck = np.arange(w)[:, None], np.arange(w)[None, :]
    cs = np.clip(cq - na_cols // 2, 0, w - na_cols)
    col_ok = (ck >= cs) & (ck < cs + na_cols)
    blocks = jnp.take(band, jnp.asarray(idx_r.reshape(-1)), axis=1)
    blocks = blocks.reshape((n_heads,) + idx_r.shape + (w, w))
    ok = row_ok[:, :, :, None, None] & col_ok[None, None, None]
    table = jnp.where(ok[None], blocks * LOG2E, MASKED).transpose(0, 1, 2, 4, 3, 5)
    return table.reshape(n_heads, 3, NA_GROUP_ROWS * w, NA_WINDOW_ROWS * w).astype(F32)


def _na_call(qkv, kc, vc, bias, na_rows, groups=4):
    n_heads, b, l, dh = qkv.shape[0] // 3, *qkv.shape[1:]
    n_rows = l // GRID_W
    assert n_rows % NA_GROUP_ROWS == 0 and n_rows >= NA_WINDOW_ROWS + NA_GROUP_ROWS
    assert NA_GROUP_ROWS - 1 + min(na_rows, n_rows) <= NA_WINDOW_ROWS
    n_groups = n_rows // NA_GROUP_ROWS
    groups = math.gcd(n_groups, groups)
    tq = groups * NA_GROUP_ROWS * GRID_W
    lc = kc.shape[1]
    return pl.pallas_call(
        functools.partial(_na_kernel, n_rows=n_rows, half_rows=min(na_rows, n_rows) // 2,
                          groups=groups),
        out_shape=jax.ShapeDtypeStruct((b, l, n_heads * dh), BF16),
        grid=(b, n_heads, n_groups // groups),
        in_specs=[pl.BlockSpec((None, None, tq, dh), lambda bi, h, g: (h, bi, g, 0)),
                  pl.BlockSpec((None, None, l, dh), lambda bi, h, g: (n_heads + h, bi, 0, 0)),
                  pl.BlockSpec((None, None, l, dh), lambda bi, h, g: (2 * n_heads + h, bi, 0, 0)),
                  pl.BlockSpec((None, lc, dh), lambda bi, h, g: (bi, 0, h)),
                  pl.BlockSpec((None, lc, dh), lambda bi, h, g: (bi, 0, h)),
                  pl.BlockSpec((None,) + bias.shape[1:], lambda bi, h, g: (h, 0, 0, 0))],
        out_specs=pl.BlockSpec((None, tq, dh), lambda bi, h, g: (bi, g, h)),
        compiler_params=_params("parallel", "parallel", "arbitrary"),
        name="na_attn",
    )(qkv, qkv, qkv, kc, vc, bias)


def _pool_kernel(u_ref, w_ref, s_ref, o_ref, *, chunk, window):
    l = u_ref.shape[0]
    half = lax.shift_left(jnp.int32(1), pl.program_id(1).astype(jnp.int32))
    w = w_ref[...]
    scale = s_ref[...]

    def body(c, carry):
        t0 = pl.multiple_of(c * chunk, chunk)
        s0 = pl.multiple_of(jnp.clip(t0 - (window - chunk) // 2, 0, l - window), 16)
        t = t0 + lax.broadcasted_iota(jnp.int32, (chunk, 1), 0)
        lo = jnp.maximum(t - half, 0)
        hi = jnp.minimum(t + half, l)
        pos = s0 + lax.broadcasted_iota(jnp.int32, (1, window), 1)
        band = jnp.where((pos >= lo) & (pos < hi), 1.0, 0.0).astype(BF16)
        sums = _dot(band, u_ref[pl.ds(s0, window), :])
        mean = sums / (hi - lo).astype(F32)
        d = (mean - u_ref[pl.ds(t0, chunk), :].astype(F32)).astype(BF16)
        o_ref[pl.ds(t0, chunk), :] = (_dot(d, w) * scale).astype(o_ref.dtype)
        return carry

    lax.fori_loop(0, l // chunk, body, 0, unroll=min(4, l // chunk))


def _pool_call(u, w_pool, pool_scale, i_even):
    _, b, l, _ = u.shape
    n_groups, ch, _ = w_pool.shape[1:]
    assert n_groups == len(POOL_WINDOWS) and all(w == 2 << g for g, w in enumerate(POOL_WINDOWS))
    chunk = min(256, l)
    window = min(2 * chunk, l)
    assert l % chunk == 0 and (window == l or window - chunk >= max(POOL_WINDOWS))
    return pl.pallas_call(
        functools.partial(_pool_kernel, chunk=chunk, window=window),
        out_shape=jax.ShapeDtypeStruct((b, l, n_groups * ch), BF16),
        grid=(b, n_groups),
        in_specs=[pl.BlockSpec((None, None, l, ch), lambda bi, g: (g, bi, 0, 0)),
                  pl.BlockSpec((None, None, ch, ch), lambda bi, g: (i_even, g, 0, 0)),
                  pl.BlockSpec((None, 1, ch), lambda bi, g: (i_even, 0, g))],
        out_specs=pl.BlockSpec((None, l, ch), lambda bi, g: (bi, 0, g)),
        compiler_params=_params("parallel", "parallel"),
        name="pool",
    )(u, w_pool, pool_scale)


def _out_kernel(*refs, n_lhs, alpha, rows):
    x_ref, mod_ref = refs[0], refs[1]
    g_ref, b_ref, o_ref = refs[-3], refs[-2], refs[-1]
    gate = mod_ref[...][5:6]
    for r0 in range(0, x_ref.shape[0], rows):
        r = slice(r0, r0 + rows)
        y = functools.reduce(jnp.add, [_dot(refs[2 + 2 * i][r, :], refs[3 + 2 * i][...])
                                       for i in range(n_lhs)])
        o_ref[r, :] = _post_norm(x_ref[r, :], gate * y, g_ref[...], b_ref[...], alpha)


def _out_call(x, mod, tokens_per_cond, lhs_w, ln_g, ln_b, alpha, tm=1024, rows=256):
    t, d = x.shape
    tm = min(tm, tokens_per_cond)
    per = tokens_per_cond // tm
    in_specs = [pl.BlockSpec((tm, d), lambda i: (i, 0)),
                pl.BlockSpec((None, 9, d), lambda i: (i // per, 0, 0))]
    args = [x, mod]
    for lhs, w, w_block, w_index in lhs_w:
        in_specs.append(pl.BlockSpec((tm, lhs.shape[1]), lambda i: (i, 0)))
        in_specs.append(_resident(w_block, w_index))
        args += [lhs, w]
    in_specs += [pl.BlockSpec((1, d), lambda i: (0, 0))] * 2
    args += [ln_g, ln_b]
    return pl.pallas_call(
        functools.partial(_out_kernel, n_lhs=len(lhs_w), alpha=alpha, rows=min(rows, tm)),
        out_shape=jax.ShapeDtypeStruct((t, d), F32),
        grid=(t // tm,),
        in_specs=in_specs,
        out_specs=pl.BlockSpec((tm, d), lambda i: (i, 0)),
        compiler_params=_params("parallel"),
        name="out_proj",
    )(*args)


def _rms(x, g):
    return x * lax.rsqrt(jnp.mean(x * x, axis=-1, keepdims=True) + RMS_EPS) * g


def _rotary_pair(t, tab):
    u = t * tab
    return u + pltpu.roll(u, u.shape[-1] // 2, 1)


def _rope_half_mask(shape):
    return lax.broadcasted_iota(jnp.int32, shape, 1) < shape[-1] // 2


def _head_store(ref, h, r, lane0, value):
    ref[h, r, lane0:lane0 + value.shape[-1]] = value.astype(ref.dtype)


def _store_values_with_ones(v_ref, r, v, dv):
    ones = jnp.ones((v.shape[0], dv), v_ref.dtype)
    for hd in range(v_ref.shape[0]):
        _head_store(v_ref, hd, r, 0, v[:, hd * dv:(hd + 1) * dv])
        _head_store(v_ref, hd, r, dv, ones)


def _mla_proj_kernel(x_ref, mod_ref, wd_ref, qn_ref, kvn_ref, wq_ref, wk_ref, wv_ref, tab_ref,
                     q_ref, k_ref, v_ref, *state_refs, n_heads, q_lora, kv_lora, nope, q_scale, rows):
    m = mod_ref[...]
    dq = 2 * nope
    for r0 in range(0, x_ref.shape[0], rows):
        r = slice(r0, r0 + rows)
        h = _modulated(x_ref[r, :], m, 1).astype(BF16)
        down = _dot(h, wd_ref[...])
        cq = _rms(down[:, :q_lora], qn_ref[...])
        ckv = _rms(down[:, q_lora:q_lora + kv_lora], kvn_ref[...])
        pe = down[:, q_lora + kv_lora:]
        tab = tab_ref[r, :]
        if state_refs:
            state_refs[0][r, :] = ckv
            state_refs[1][r, :] = pe[:, :pe.shape[-1] // 2]
        kpe = jnp.where(_rope_half_mask(pe.shape), _rotary_pair(pe, tab), 0.0).astype(BF16)
        q = _dot(cq.astype(BF16), wq_ref[...]) * q_scale
        ckv_b = ckv.astype(BF16)
        kn = _dot(ckv_b, wk_ref[...])
        _store_values_with_ones(v_ref, r, _dot(ckv_b, wv_ref[...]), nope)
        for hd in range(n_heads):
            _head_store(q_ref, hd, r, 0, q[:, hd * dq:hd * dq + nope])
            _head_store(q_ref, hd, r, nope, _rotary_pair(q[:, hd * dq + nope:(hd + 1) * dq], tab))
            _head_store(k_ref, hd, r, 0, kn[:, hd * nope:(hd + 1) * nope])
            _head_store(k_ref, hd, r, nope, kpe)


def _mla_proj_call(x, mod, tokens_per_cond, wd, qn, kvn, wq, wk, wv, tab, i_odd, dims, with_state,
                   tm=512, rows=256):
    n_heads, q_lora, kv_lora, nope, rope, q_scale = dims
    t, d = x.shape
    tm = min(tm, tokens_per_cond, tab.shape[0])
    per = tokens_per_cond // tm
    tab_blocks = tab.shape[0] // tm
    dq = 2 * nope
    out_shape = [jax.ShapeDtypeStruct((n_heads, t, dq), BF16)] * 3
    out_specs = [pl.BlockSpec((n_heads, tm, dq), lambda i: (0, i, 0))] * 3
    if with_state:
        out_shape += [jax.ShapeDtypeStruct((t, kv_lora), F32), jax.ShapeDtypeStruct((t, rope), F32)]
        out_specs += [pl.BlockSpec((tm, kv_lora), lambda i: (i, 0)),
                      pl.BlockSpec((tm, rope), lambda i: (i, 0))]
    whole = lambda a: _resident((None,) + a.shape[1:], (i_odd,) + (0,) * (a.ndim - 1))
    return pl.pallas_call(
        functools.partial(_mla_proj_kernel, n_heads=n_heads, q_lora=q_lora, kv_lora=kv_lora,
                          nope=nope, q_scale=q_scale, rows=min(rows, tm)),
        out_shape=out_shape,
        grid=(t // tm,),
        in_specs=[pl.BlockSpec((tm, d), lambda i: (i, 0)),
                  pl.BlockSpec((None, 9, d), lambda i: (i // per, 0, 0)),
                  whole(wd), whole(qn), whole(kvn), whole(wq), whole(wk), whole(wv),
                  pl.BlockSpec((tm, tab.shape[1]), lambda i: (i % tab_blocks, 0))],
        out_specs=out_specs,
        compiler_params=_params("parallel"),
        name="mla_proj",
    )(x, mod, wd, qn, kvn, wq, wk, wv, tab)


def _mla_expand_kernel(ckv_ref, kpe_ref, wk_ref, wv_ref, k_ref, v_ref, *, n_heads, nope):
    ckv = ckv_ref[...].astype(BF16)
    kn = _dot(ckv, wk_ref[...])
    r = slice(None)
    _store_values_with_ones(v_ref, r, _dot(ckv, wv_ref[...]), nope)
    kpe = kpe_ref[...].astype(BF16)
    for hd in range(n_heads):
        _head_store(k_ref, hd, r, 0, kn[:, hd * nope:(hd + 1) * nope])
        _head_store(k_ref, hd, r, nope, kpe)


def _mla_expand_call(ckv, kpe_pad, wk, wv, i_odd, n_heads, nope, tm=256):
    t, kv_lora = ckv.shape
    tm = min(tm, t)
    dq = 2 * nope
    whole = lambda a: pl.BlockSpec((None,) + a.shape[1:], lambda i: (i_odd,) + (0,) * (a.ndim - 1))
    return pl.pallas_call(
        functools.partial(_mla_expand_kernel, n_heads=n_heads, nope=nope),
        out_shape=[jax.ShapeDtypeStruct((n_heads, t, dq), BF16)] * 2,
        grid=(t // tm,),
        in_specs=[pl.BlockSpec((tm, kv_lora), lambda i: (i, 0)),
                  pl.BlockSpec((tm, kpe_pad.shape[1]), lambda i: (i, 0)),
                  whole(wk), whole(wv)],
        out_specs=[pl.BlockSpec((n_heads, tm, dq), lambda i: (0, i, 0))] * 2,
        compiler_params=_params("parallel"),
        name="mla_expand",
    )(ckv, kpe_pad, wk, wv)


def _rotate_half_columns(w, rope):
    q = rope // 4
    parts = [w[..., i * q:(i + 1) * q] for i in range(4)]
    return jnp.concatenate([-parts[1], parts[0], -parts[3], parts[2]], axis=-1)


def _rope_table(n_tokens, rope):
    axis = rope // 2
    t = jnp.arange(n_tokens)
    inv = ROPE_BASE ** (-jnp.arange(0, axis, 2, dtype=F32) / axis)
    ang_r = (t // GRID_W).astype(F32)[:, None] * inv[None, :]
    ang_c = (t % GRID_W).astype(F32)[:, None] * inv[None, :]
    ang = jnp.concatenate([ang_r, ang_r, ang_c, ang_c], axis=-1)
    return jnp.concatenate([jnp.cos(ang), jnp.sin(ang)], axis=-1)


def _identity_rope_table(n_tokens, rope):
    return jnp.concatenate([jnp.ones((n_tokens, rope), F32), jnp.zeros((n_tokens, rope), F32)], axis=-1)


def kernel(x_prompt, x_sample, cache_na_k, cache_na_v, cache_mla_ckv, cache_mla_kpe, c, c_ctx,
           w_mod, b_mod, ln_g, ln_b, ffn_w1, ffn_w3, ffn_w2,
           na_w_in, mix0_w_out, na_rpb, pool_w, pool_scale,
           mla_w_down, mla_q_norm, mla_w_uq, mla_kv_norm, mla_w_ukv, mla_w_out):
    depth, d, _ = w_mod.shape
    alpha = (2 * depth) ** 0.25
    batch, seq, _ = x_prompt.shape
    dec_batch, dec_seq, _ = x_sample.shape
    na_heads, na_dh = cache_na_k.shape[-2:]
    na_width = na_heads * na_dh
    pool_groups, pool_ch = pool_w.shape[1:3]
    assert pool_groups * pool_ch == na_width and pool_ch % 128 == 0
    q_lora, kv_lora = mla_q_norm.shape[-1], mla_kv_norm.shape[-1]
    rope = cache_mla_kpe.shape[-1]
    uq_w, ukv_w, out_rows = mla_w_uq.shape[-1], mla_w_ukv.shape[-1], mla_w_out.shape[1]
    mla_heads = (uq_w - ukv_w + out_rows) // rope
    nope = uq_w // mla_heads - rope
    v_dim = out_rows // mla_heads
    assert nope == v_dim and 2 * rope == nope and nope % 128 == 0
    mla_dims = (mla_heads, q_lora, kv_lora, nope, rope, float((nope + rope) ** -0.5) * LOG2E)

    w13, w2 = _ffn_chunked(ffn_w1, ffn_w3, FFN_CHUNK), ffn_w2.astype(BF16)
    w_in = na_w_in.astype(BF16)
    w_out0 = mix0_w_out.astype(BF16)
    w_pool = pool_w.astype(BF16)
    p_scale = pool_scale.reshape(pool_scale.shape[0], 1, -1)
    kpe_cols = mla_w_down[..., q_lora + kv_lora:]
    wd = jnp.concatenate([mla_w_down, _rotate_half_columns(kpe_cols, rope)], axis=-1).astype(BF16)
    n_odd = mla_w_uq.shape[0]
    uq = mla_w_uq.reshape(n_odd, q_lora, mla_heads, nope + rope)
    wq = jnp.concatenate([uq, _rotate_half_columns(uq[..., nope:], rope)], axis=-1)
    wq = wq.reshape(n_odd, q_lora, mla_heads * 2 * nope).astype(BF16)
    ukv = mla_w_ukv.reshape(n_odd, kv_lora, mla_heads, nope + v_dim)
    wk = ukv[..., :nope].reshape(n_odd, kv_lora, mla_heads * nope).astype(BF16)
    wv = ukv[..., nope:].reshape(n_odd, kv_lora, mla_heads * v_dim).astype(BF16)
    w_out1 = mla_w_out.astype(BF16)
    qn = mla_q_norm.reshape(n_odd, 1, q_lora)
    kvn = mla_kv_norm.reshape(n_odd, 1, kv_lora)

    n_cond = 1 + dec_batch
    cond = jnp.concatenate([c_ctx[None, :], c, jnp.zeros((-n_cond % 16, d), F32)], axis=0)
    mod = _mod_call(cond, w_mod, b_mod).reshape(depth, cond.shape[0], 9, d)

    rope_tab = _rope_table(dec_seq, rope)
    ident_tab = _identity_rope_table(min(256, seq), rope)

    def trunk(x3, mod_rows, per_cond, caches):
        bsz, l, _ = x3.shape
        x = x3.reshape(bsz * l, d)
        states = []
        for layer in range(depth):
            m = mod_rows[layer]
            g = lambda s: ln_g[layer, s].reshape(1, d)
            bb = lambda s: ln_b[layer, s].reshape(1, d)
            x = _ffn_call(x, m, per_cond, w13, w2, layer, 0, g(0), bb(0), 0, alpha)
            i = layer // 2
            if layer % 2 == 0:
                outs = _na_proj_call(x, m, per_cond, w_in, i, na_heads, na_dh, pool_groups, pool_ch,
                                     float(na_dh ** -0.5) * LOG2E, caches is None)
                qkv = outs[0].reshape(3 * na_heads, bsz, l, na_dh)
                u = outs[1].reshape(pool_groups, bsz, l, pool_ch)
                if caches is None:
                    states.append((outs[2], outs[3]))
                    a = _attn_call(qkv, 0, [(qkv, na_heads, qkv, 2 * na_heads)], na_heads, na_dh,
                                   256, 512, "na_dense_attn", heads=4)
                else:
                    kc = caches[0][:, i].reshape(bsz, -1, na_width).astype(BF16)
                    vc = caches[1][:, i].reshape(bsz, -1, na_width).astype(BF16)
                    bias = _na_bias_table(na_rpb[i], l // GRID_W)
                    a = _na_call(qkv, kc, vc, bias, (na_rpb.shape[2] + 1) // 2)
                pooled = _pool_call(u, w_pool, p_scale, i)
                x = _out_call(x, m, per_cond,
                              [(a.reshape(bsz * l, na_width), w_out0, (None, na_width, d), (i, 0, 0)),
                               (pooled.reshape(bsz * l, na_width), w_out0, (None, na_width, d), (i, 1, 0))],
                              g(1), bb(1), alpha)
            else:
                tab = ident_tab if caches is None else rope_tab
                outs = _mla_proj_call(x, m, per_cond, wd, qn, kvn, wq, wk, wv, tab, i, mla_dims,
                                      caches is None)
                q, k, v = (o.reshape(mla_heads, bsz, l, 2 * nope) for o in outs[:3])
                segments = [(k, 0, v, 0)]
                if caches is None:
                    states.append((outs[3], outs[4]))
                else:
                    ckv_c = caches[2][:, i].reshape(-1, kv_lora)
                    kpe_c = caches[3][:, i].reshape(-1, rope)
                    kpe_c = jnp.concatenate([kpe_c, jnp.zeros_like(kpe_c)], axis=-1)
                    k_c, v_c = _mla_expand_call(ckv_c, kpe_c, wk, wv, i, mla_heads, nope)
                    segments.append((k_c.reshape(mla_heads, bsz, -1, 2 * nope), 0,
                                     v_c.reshape(mla_heads, bsz, -1, 2 * nope), 0))
                o = _attn_call(q, 0, segments, mla_heads, v_dim, 1024, 256, "mla_attn",
                               heads=1 if l > 1024 else 4)
                x = _out_call(x, m, per_cond,
                              [(o.reshape(bsz * l, -1), w_out1, (None,) + w_out1.shape[1:], (i, 0, 0))],
                              g(1), bb(1), alpha)
            x = _ffn_call(x, m, per_cond, w13, w2, layer, 1, g(2), bb(2), 2, alpha)
        return x.reshape(bsz, l, d), states

    y_prompt, st = trunk(x_prompt, mod[:, 0:1], batch * seq, None)
    y_sample, _ = trunk(x_sample, mod[:, 1:n_cond], dec_seq,
                        (cache_na_k, cache_na_v, cache_mla_ckv, cache_mla_kpe))
    even = [s for layer, s in enumerate(st) if layer % 2 == 0]
    odd = [s for layer, s in enumerate(st) if layer % 2 == 1]
    new_na_k = jnp.stack([s[0].reshape(batch, seq, na_heads, na_dh) for s in even], axis=1)
    new_na_v = jnp.stack([s[1].reshape(batch, seq, na_heads, na_dh) for s in even], axis=1)
    new_mla_ckv = jnp.stack([s[0].reshape(batch, seq, kv_lora) for s in odd], axis=1)
    new_mla_kpe = jnp.stack([s[1].reshape(batch, seq, rope) for s in odd], axis=1)
    return (y_prompt, y_sample, new_na_k, new_na_v, new_mla_ckv, new_mla_kpe)
```

```python
import functools
import math

import numpy as np
import jax
import jax.numpy as jnp
from jax import lax
from jax.experimental import pallas as pl
from jax.experimental.pallas import tpu as pltpu

GRID_W = 64
LN_EPS = 1e-5
RMS_EPS = 1e-6
POOL_WINDOWS = (2, 4, 8, 16)
ROPE_BASE = 10000.0
MASKED = -1e30
LOG2E = math.log2(math.e)
FFN_CHUNK = 512
ATTN_ROWS = 256
NA_GROUP_ROWS = 4
NA_WINDOW_ROWS = 12
V7X_VMEM_LIMIT_BYTES = 56 * 1024 * 1024

F32 = jnp.float32
BF16 = jnp.bfloat16
_NT = (((1,), (1,)), ((), ()))


def _params(*semantics):
    return pltpu.CompilerParams(dimension_semantics=semantics,
                                vmem_limit_bytes=V7X_VMEM_LIMIT_BYTES)


def _dot(a, b):
    return jnp.dot(a, b, preferred_element_type=F32)


def _post_norm(x, update, g, b, alpha):
    z = alpha * x + update
    mu = jnp.mean(z, axis=-1, keepdims=True)
    zc = z - mu
    var = jnp.mean(zc * zc, axis=-1, keepdims=True)
    return zc * lax.rsqrt(var + LN_EPS) * g + b


def _modulated(x, m, j):
    return x * (1.0 + m[3 * j + 1:3 * j + 2]) + m[3 * j:3 * j + 1]


def _mod_kernel(c_ref, w_ref, b_ref, o_ref):
    c = c_ref[...]
    h = (c * jax.nn.sigmoid(c)).astype(BF16)
    o_ref[...] = _dot(h, w_ref[...].astype(BF16)) + b_ref[...]


def _mod_call(cond, w_mod, b_mod, tn=1024):
    depth, d, n = w_mod.shape
    r = cond.shape[0]
    return pl.pallas_call(
        _mod_kernel,
        out_shape=jax.ShapeDtypeStruct((depth, r, n), F32),
        grid=(depth, n // tn),
        in_specs=[pl.BlockSpec((r, d), lambda l, j: (0, 0)),
                  pl.BlockSpec((None, d, tn), lambda l, j: (l, 0, j)),
                  pl.BlockSpec((None, 1, tn), lambda l, j: (l, 0, j))],
        out_specs=pl.BlockSpec((None, r, tn), lambda l, j: (l, 0, j)),
        compiler_params=_params("parallel", "parallel"),
        name="mod_proj",
    )(cond, w_mod, b_mod.reshape(depth, 1, n))


def _ffn_kernel(x_ref, mod_ref, w13_hbm, w2_hbm, g_ref, b_ref, o_ref, w13_buf, w2_buf, sem, h_ref, *,
                l, s, j, alpha, rows, mid_rows):
    i = pl.program_id(0)
    n_chunks = w13_hbm.shape[2]
    tm = x_ref.shape[0]
    tf = w2_buf.shape[1]
    first_slot = (i * n_chunks) % 2

    def chunk_copies(c, slot):
        return (pltpu.make_async_copy(w13_hbm.at[l, s, c], w13_buf.at[slot], sem.at[0, slot]),
                pltpu.make_async_copy(w2_hbm.at[l, s, pl.ds(c * tf, tf)], w2_buf.at[slot], sem.at[1, slot]))

    @pl.when(i == 0)
    def _():
        for cp in chunk_copies(0, first_slot):
            cp.start()

    m = mod_ref[...]
    half_gate = 0.5 * m[3 * j + 2:3 * j + 3]
    for c in range(n_chunks):
        slot = (first_slot + c) % 2
        for cp in chunk_copies((c + 1) % n_chunks, 1 - slot):
            cp.start()
        for cp in chunk_copies(c, slot):
            cp.wait()

        def chunk_update(h):
            ab = _dot(h, w13_buf[slot])
            a, b = ab[:, :tf], ab[:, tf:]
            act = (a * jax.nn.sigmoid(a) * b).astype(BF16)
            return _dot(act, w2_buf[slot])

        if c == 0:
            for r0 in range(0, tm, rows):
                h = _modulated(x_ref[r0:r0 + rows, :], m, j).astype(BF16)
                h_ref[r0:r0 + rows, :] = h
                o_ref[r0:r0 + rows, :] = chunk_update(h)
        elif c < n_chunks - 1:
            for r0 in range(0, tm, mid_rows):
                o_ref[r0:r0 + mid_rows, :] += chunk_update(h_ref[r0:r0 + mid_rows, :])
        else:
            for r0 in range(0, tm, rows):
                y = o_ref[r0:r0 + rows, :] + chunk_update(h_ref[r0:r0 + rows, :])
                o_ref[r0:r0 + rows, :] = _post_norm(x_ref[r0:r0 + rows, :], half_gate * y,
                                                    g_ref[...], b_ref[...], alpha)

    @pl.when(i == pl.num_programs(0) - 1)
    def _():
        for cp in chunk_copies(0, (first_slot + n_chunks) % 2):
            cp.wait()


def _ffn_pack_kernel(w1_ref, w3_ref, o_ref):
    tf = w1_ref.shape[-1]
    o_ref[:, :tf] = w1_ref[...].astype(BF16)
    o_ref[:, tf:] = w3_ref[...].astype(BF16)


def _ffn_chunked(w1, w3, tf):
    depth, n_sub, d, ff = w1.shape
    tf = min(tf, ff)
    assert ff % tf == 0 and ff // tf >= 2
    spec = pl.BlockSpec((None, None, d, tf), lambda l, s, f: (l, s, 0, f))
    return pl.pallas_call(
        _ffn_pack_kernel,
        out_shape=jax.ShapeDtypeStruct((depth, n_sub, ff // tf, d, 2 * tf), BF16),
        grid=(depth, n_sub, ff // tf),
        in_specs=[spec, spec],
        out_specs=pl.BlockSpec((None, None, None, d, 2 * tf), lambda l, s, f: (l, s, f, 0, 0)),
        compiler_params=_params("parallel", "parallel", "parallel"),
        name="ffn_pack",
    )(w1, w3)


def _ffn_call(x, mod, tokens_per_cond, w13, w2, l, s, ln_g, ln_b, j, alpha, tm=1024, rows=256,
              mid_rows=512):
    t, d = x.shape
    tf = w13.shape[-1] // 2
    tm = min(tm, tokens_per_cond)
    assert tokens_per_cond % tm == 0
    per = tokens_per_cond // tm
    return pl.pallas_call(
        functools.partial(_ffn_kernel, l=l, s=s, j=j, alpha=alpha, rows=min(rows, tm),
                          mid_rows=min(mid_rows, tm)),
        out_shape=jax.ShapeDtypeStruct((t, d), F32),
        grid=(t // tm,),
        in_specs=[pl.BlockSpec((tm, d), lambda i: (i, 0)),
                  pl.BlockSpec((None, 9, d), lambda i: (i // per, 0, 0)),
                  pl.BlockSpec(memory_space=pl.ANY),
                  pl.BlockSpec(memory_space=pl.ANY),
                  pl.BlockSpec((1, d), lambda i: (0, 0)),
                  pl.BlockSpec((1, d), lambda i: (0, 0))],
        out_specs=pl.BlockSpec((tm, d), lambda i: (i, 0)),
        scratch_shapes=[pltpu.VMEM((2, d, 2 * tf), BF16),
                        pltpu.VMEM((2, tf, d), BF16),
                        pltpu.SemaphoreType.DMA((2, 2)),
                        pltpu.VMEM((tm, d), BF16)],
        compiler_params=_params("arbitrary"),
        name="ffn",
    )(x, mod, w13, w2, ln_g, ln_b)


def _na_proj_kernel(x_ref, mod_ref, w_ref, qkv_ref, u_ref, *kv_refs, q_scale, rows):
    m = mod_ref[...]
    n_heads, dh = qkv_ref.shape[0] // 3, qkv_ref.shape[-1]
    n_pool, ch = u_ref.shape[0], u_ref.shape[-1]
    width = n_heads * dh
    for r0 in range(0, x_ref.shape[0], rows):
        r = slice(r0, r0 + rows)
        h = _modulated(x_ref[r, :], m, 1).astype(BF16)
        for n in range(3):
            y = _dot(h, w_ref[:, n * width:(n + 1) * width])
            if kv_refs and n > 0:
                kv_refs[n - 1][r, :] = y
            y = (y * q_scale if n == 0 else y).astype(BF16)
            for hd in range(n_heads):
                qkv_ref[n * n_heads + hd, r, :] = y[:, hd * dh:(hd + 1) * dh]
        y = _dot(h, w_ref[:, 3 * width:]).astype(BF16)
        for g in range(n_pool):
            u_ref[g, r, :] = y[:, g * ch:(g + 1) * ch]


def _resident(block_shape, index):
    return pl.BlockSpec(block_shape, lambda *_: index, pipeline_mode=pl.Buffered(1))


def _na_proj_call(x, mod, tokens_per_cond, w_in, i_even, n_heads, dh, n_pool, ch, q_scale, with_kv,
                  tm=512, rows=256):
    t, d = x.shape
    n_in = w_in.shape[-1]
    na_width = n_heads * dh
    assert n_in == 3 * na_width + n_pool * ch
    tm = min(tm, tokens_per_cond)
    per = tokens_per_cond // tm
    out_shape = [jax.ShapeDtypeStruct((3 * n_heads, t, dh), BF16),
                 jax.ShapeDtypeStruct((n_pool, t, ch), BF16)]
    out_specs = [pl.BlockSpec((3 * n_heads, tm, dh), lambda i: (0, i, 0)),
                 pl.BlockSpec((n_pool, tm, ch), lambda i: (0, i, 0))]
    if with_kv:
        out_shape += [jax.ShapeDtypeStruct((t, na_width), F32)] * 2
        out_specs += [pl.BlockSpec((tm, na_width), lambda i: (i, 0))] * 2
    return pl.pallas_call(
        functools.partial(_na_proj_kernel, q_scale=q_scale, rows=min(rows, tm)),
        out_shape=out_shape,
        grid=(t // tm,),
        in_specs=[pl.BlockSpec((tm, d), lambda i: (i, 0)),
                  pl.BlockSpec((None, 9, d), lambda i: (i // per, 0, 0)),
                  _resident((None, d, n_in), (i_even, 0, 0))],
        out_specs=out_specs,
        compiler_params=_params("parallel"),
        name="na_proj",
    )(x, mod, w_in)


def _attn_rows(q, kv_refs, chunk, dv):
    rows = q.shape[0]
    den_in_v = kv_refs[1].shape[-1] == 2 * dv
    m = jnp.full((rows, 1), MASKED, F32)
    acc = jnp.zeros((rows, kv_refs[1].shape[-1]), F32)
    den = jnp.zeros((rows, 1), F32)
    for k_ref, v_ref in zip(kv_refs[0::2], kv_refs[1::2]):
        lk = k_ref.shape[0]
        step = min(chunk, lk)
        for c0 in range(0, lk, step):
            s = lax.dot_general(q, k_ref[c0:c0 + step, :], _NT, preferred_element_type=F32)
            m_new = jnp.maximum(m, jnp.max(s, axis=-1, keepdims=True))
            rescale = jnp.exp2(m - m_new)
            p = jnp.exp2(s - m_new)
            if not den_in_v:
                den = rescale * den + jnp.sum(p, axis=-1, keepdims=True)
            acc = rescale * acc + _dot(p.astype(BF16), v_ref[c0:c0 + step, :])
            m = m_new
    return acc[:, :dv] / (acc[:, dv:] if den_in_v else den)


def _attn_kernel(*refs, chunk, dv, rows):
    q_ref, o_ref = refs[0], refs[-1]
    for hd in range(q_ref.shape[0]):
        kv_refs = [ref.at[hd] for ref in refs[1:-1]]
        for r0 in range(0, q_ref.shape[1], rows):
            o = _attn_rows(q_ref[hd, r0:r0 + rows, :], kv_refs, chunk, dv)
            o_ref[r0:r0 + rows, hd * dv:(hd + 1) * dv] = o.astype(o_ref.dtype)


def _attn_call(q, q_head0, segments, n_heads, dv, tq, chunk, name, heads=1):
    _, b, l, dq = q.shape
    tq = min(tq, l)
    heads = math.gcd(heads, n_heads, q_head0, *[h0 for _, k0, _, v0 in segments for h0 in (k0, v0)])
    in_specs = [pl.BlockSpec((heads, None, tq, dq),
                             lambda bi, h, qi: (q_head0 // heads + h, bi, qi, 0))]
    args = [q]
    for k, k0, v, v0 in segments:
        lk = k.shape[2]
        assert lk % min(chunk, lk) == 0
        in_specs.append(pl.BlockSpec((heads, None, lk, dq),
                                     lambda bi, h, qi, k0=k0: (k0 // heads + h, bi, 0, 0)))
        in_specs.append(pl.BlockSpec((heads, None, lk, v.shape[-1]),
                                     lambda bi, h, qi, v0=v0: (v0 // heads + h, bi, 0, 0)))
        args += [k, v]
    return pl.pallas_call(
        functools.partial(_attn_kernel, chunk=chunk, dv=dv, rows=min(ATTN_ROWS, tq)),
        out_shape=jax.ShapeDtypeStruct((b, l, n_heads * dv), BF16),
        grid=(b, n_heads // heads, l // tq),
        in_specs=in_specs,
        out_specs=pl.BlockSpec((None, tq, heads * dv), lambda bi, h, qi: (bi, qi, h)),
        compiler_params=_params("parallel", "parallel", "arbitrary"),
        name=name,
    )(*args)


def _na_kernel(q_ref, k_ref, v_ref, kc_ref, vc_ref, bias_ref, o_ref, *, n_rows, half_rows, groups):
    n_groups = n_rows // NA_GROUP_ROWS
    tq = NA_GROUP_ROWS * GRID_W
    win = NA_WINDOW_ROWS * GRID_W
    for c in range(groups):
        g = pl.program_id(2) * groups + c
        base_row = jnp.clip(g * NA_GROUP_ROWS - half_rows, 0, n_rows - NA_WINDOW_ROWS)
        base = pl.multiple_of(base_row * GRID_W, GRID_W)
        variant = jnp.where(g == 0, 0, jnp.where(g == n_groups - 1, 2, 1))
        q = q_ref[c * tq:(c + 1) * tq, :]
        kw = k_ref[pl.ds(base, win), :]
        vw = v_ref[pl.ds(base, win), :]
        s_loc = lax.dot_general(q, kw, _NT, preferred_element_type=F32) + bias_ref[variant]
        s_ctx = lax.dot_general(q, kc_ref[...], _NT, preferred_element_type=F32)
        m = jnp.maximum(jnp.max(s_loc, axis=-1, keepdims=True),
                        jnp.max(s_ctx, axis=-1, keepdims=True))
        p_loc = jnp.exp2(s_loc - m)
        p_ctx = jnp.exp2(s_ctx - m)
        den = jnp.sum(p_loc, axis=-1, keepdims=True) + jnp.sum(p_ctx, axis=-1, keepdims=True)
        o = _dot(p_loc.astype(BF16), vw) + _dot(p_ctx.astype(BF16), vc_ref[...])
        o_ref[c * tq:(c + 1) * tq, :] = (o / den).astype(o_ref.dtype)


def _na_group_rows(g, n_rows, na_rows):
    kr = min(na_rows, n_rows)
    base = int(np.clip(g * NA_GROUP_ROWS - kr // 2, 0, n_rows - NA_WINDOW_ROWS))
    rq = (g * NA_GROUP_ROWS + np.arange(NA_GROUP_ROWS))[:, None]
    rk = (base + np.arange(NA_WINDOW_ROWS))[None, :]
    rs = np.clip(rq - kr // 2, 0, n_rows - kr)
    ok = (rk >= rs) & (rk < rs + kr)
    assert (ok.sum(axis=1) == kr).all(), "key window misses part of a neighbourhood"
    return np.where(ok, rk - rq + na_rows - 1, 0), ok


def _na_bias_table(rpb, n_rows):
    n_heads, nr2, nc2 = rpb.shape
    na_rows, na_cols = (nr2 + 1) // 2, (nc2 + 1) // 2
    n_groups = n_rows // NA_GROUP_ROWS
    per_group = [_na_group_rows(g, n_rows, na_rows) for g in range(n_groups)]
    for g in range(2, n_groups - 1):
        assert all((a == b).all() for a, b in zip(per_group[g], per_group[1])), "interior groups differ"
    idx_r, row_ok = (np.stack([per_group[g][i] for g in (0, 1, n_groups - 1)]) for i in range(2))
    w = GRID_W
    e = jnp.pad(rpb, ((0, 0), (0, 0), (w - na_cols, 2 * w - (w - na_cols) - nc2)))
    band = jnp.tile(e, (1, 1, w))[..., :w * (2 * w - 1)].reshape(n_heads, nr2, w, 2 * w - 1)[..., w - 1:]
    cq, ck = np.arange(w)[:, None], np.arange(w)[None, :]
    cs = np.clip(cq - na_cols // 2, 0, w - na_cols)
    col_ok = (ck >= cs) & (ck < cs + na_cols)
    blocks = jnp.take(band, jnp.asarray(idx_r.reshape(-1)), axis=1)
    blocks = blocks.reshape((n_heads,) + idx_r.shape + (w, w))
    ok = row_ok[:, :, :, None, None] & col_ok[None, None, None]
    table = jnp.where(ok[None], blocks * LOG2E, MASKED).transpose(0, 1, 2, 4, 3, 5)
    return table.reshape(n_heads, 3, NA_GROUP_ROWS * w, NA_WINDOW_ROWS * w).astype(F32)


def _na_call(qkv, kc, vc, bias, na_rows, groups=4):
    n_heads, b, l, dh = qkv.shape[0] // 3, *qkv.shape[1:]
    n_rows = l // GRID_W
    assert n_rows % NA_GROUP_ROWS == 0 and n_rows >= NA_WINDOW_ROWS + NA_GROUP_ROWS
    assert NA_GROUP_ROWS - 1 + min(na_rows, n_rows) <= NA_WINDOW_ROWS
    n_groups = n_rows // NA_GROUP_ROWS
    groups = math.gcd(n_groups, groups)
    tq = groups * NA_GROUP_ROWS * GRID_W
    lc = kc.shape[1]
    return pl.pallas_call(
        functools.partial(_na_kernel, n_rows=n_rows, half_rows=min(na_rows, n_rows) // 2,
                          groups=groups),
        out_shape=jax.ShapeDtypeStruct((b, l, n_heads * dh), BF16),
        grid=(b, n_heads, n_groups // groups),
        in_specs=[pl.BlockSpec((None, None, tq, dh), lambda bi, h, g: (h, bi, g, 0)),
                  pl.BlockSpec((None, None, l, dh), lambda bi, h, g: (n_heads + h, bi, 0, 0)),
                  pl.BlockSpec((None, None, l, dh), lambda bi, h, g: (2 * n_heads + h, bi, 0, 0)),
                  pl.BlockSpec((None, lc, dh), lambda bi, h, g: (bi, 0, h)),
                  pl.BlockSpec((None, lc, dh), lambda bi, h, g: (bi, 0, h)),
                  pl.BlockSpec((None,) + bias.shape[1:], lambda bi, h, g: (h, 0, 0, 0))],
        out_specs=pl.BlockSpec((None, tq, dh), lambda bi, h, g: (bi, g, h)),
        compiler_params=_params("parallel", "parallel", "arbitrary"),
        name="na_attn",
    )(qkv, qkv, qkv, kc, vc, bias)


def _pool_kernel(u_ref, w_ref, s_ref, o_ref, *, chunk, window):
    l = u_ref.shape[0]
    half = lax.shift_left(jnp.int32(1), pl.program_id(1).astype(jnp.int32))
    w = w_ref[...]
    scale = s_ref[...]

    def body(c, carry):
        t0 = pl.multiple_of(c * chunk, chunk)
        s0 = pl.multiple_of(jnp.clip(t0 - (window - chunk) // 2, 0, l - window), 16)
        t = t0 + lax.broadcasted_iota(jnp.int32, (chunk, 1), 0)
        lo = jnp.maximum(t - half, 0)
        hi = jnp.minimum(t + half, l)
        pos = s0 + lax.broadcasted_iota(jnp.int32, (1, window), 1)
        band = jnp.where((pos >= lo) & (pos < hi), 1.0, 0.0).astype(BF16)
        sums = _dot(band, u_ref[pl.ds(s0, window), :])
        mean = sums / (hi - lo).astype(F32)
        d = (mean - u_ref[pl.ds(t0, chunk), :].astype(F32)).astype(BF16)
        o_ref[pl.ds(t0, chunk), :] = (_dot(d, w) * scale).astype(o_ref.dtype)
        return carry

    lax.fori_loop(0, l // chunk, body, 0, unroll=min(4, l // chunk))


def _pool_call(u, w_pool, pool_scale, i_even):
    _, b, l, _ = u.shape
    n_groups, ch, _ = w_pool.shape[1:]
    assert n_groups == len(POOL_WINDOWS) and all(w == 2 << g for g, w in enumerate(POOL_WINDOWS))
    chunk = min(256, l)
    window = min(2 * chunk, l)
    assert l % chunk == 0 and (window == l or window - chunk >= max(POOL_WINDOWS))
    return pl.pallas_call(
        functools.partial(_pool_kernel, chunk=chunk, window=window),
        out_shape=jax.ShapeDtypeStruct((b, l, n_groups * ch), BF16),
        grid=(b, n_groups),
        in_specs=[pl.BlockSpec((None, None, l, ch), lambda bi, g: (g, bi, 0, 0)),
                  pl.BlockSpec((None, None, ch, ch), lambda bi, g: (i_even, g, 0, 0)),
                  pl.BlockSpec((None, 1, ch), lambda bi, g: (i_even, 0, g))],
        out_specs=pl.BlockSpec((None, l, ch), lambda bi, g: (bi, 0, g)),
        compiler_params=_params("parallel", "parallel"),
        name="pool",
    )(u, w_pool, pool_scale)


def _out_kernel(*refs, n_lhs, alpha, rows):
    x_ref, mod_ref = refs[0], refs[1]
    g_ref, b_ref, o_ref = refs[-3], refs[-2], refs[-1]
    gate = mod_ref[...][5:6]
    for r0 in range(0, x_ref.shape[0], rows):
        r = slice(r0, r0 + rows)
        y = functools.reduce(jnp.add, [_dot(refs[2 + 2 * i][r, :], refs[3 + 2 * i][...])
                                       for i in range(n_lhs)])
        o_ref[r, :] = _post_norm(x_ref[r, :], gate * y, g_ref[...], b_ref[...], alpha)


def _out_call(x, mod, tokens_per_cond, lhs_w, ln_g, ln_b, alpha, tm=1024, rows=256):
    t, d = x.shape
    tm = min(tm, tokens_per_cond)
    per = tokens_per_cond // tm
    in_specs = [pl.BlockSpec((tm, d), lambda i: (i, 0)),
                pl.BlockSpec((None, 9, d), lambda i: (i // per, 0, 0))]
    args = [x, mod]
    for lhs, w, w_block, w_index in lhs_w:
        in_specs.append(pl.BlockSpec((tm, lhs.shape[1]), lambda i: (i, 0)))
        in_specs.append(_resident(w_block, w_index))
        args += [lhs, w]
    in_specs += [pl.BlockSpec((1, d), lambda i: (0, 0))] * 2
    args += [ln_g, ln_b]
    return pl.pallas_call(
        functools.partial(_out_kernel, n_lhs=len(lhs_w), alpha=alpha, rows=min(rows, tm)),
        out_shape=jax.ShapeDtypeStruct((t, d), F32),
        grid=(t // tm,),
        in_specs=in_specs,
        out_specs=pl.BlockSpec((tm, d), lambda i: (i, 0)),
        compiler_params=_params("parallel"),
        name="out_proj",
    )(*args)


def _rms(x, g):
    return x * lax.rsqrt(jnp.mean(x * x, axis=-1, keepdims=True) + RMS_EPS) * g


def _rotary_pair(t, tab):
    u = t * tab
    return u + pltpu.roll(u, u.shape[-1] // 2, 1)


def _rope_half_mask(shape):
    return lax.broadcasted_iota(jnp.int32, shape, 1) < shape[-1] // 2


def _head_store(ref, h, r, lane0, value):
    ref[h, r, lane0:lane0 + value.shape[-1]] = value.astype(ref.dtype)


def _store_values_with_ones(v_ref, r, v, dv):
    ones = jnp.ones((v.shape[0], dv), v_ref.dtype)
    for hd in range(v_ref.shape[0]):
        _head_store(v_ref, hd, r, 0, v[:, hd * dv:(hd + 1) * dv])
        _head_store(v_ref, hd, r, dv, ones)


def _mla_proj_kernel(x_ref, mod_ref, wd_ref, qn_ref, kvn_ref, wq_ref, wk_ref, wv_ref, tab_ref,
                     q_ref, k_ref, v_ref, *state_refs, n_heads, q_lora, kv_lora, nope, q_scale, rows):
    m = mod_ref[...]
    dq = 2 * nope
    for r0 in range(0, x_ref.shape[0], rows):
        r = slice(r0, r0 + rows)
        h = _modulated(x_ref[r, :], m, 1).astype(BF16)
        down = _dot(h, wd_ref[...])
        cq = _rms(down[:, :q_lora], qn_ref[...])
        ckv = _rms(down[:, q_lora:q_lora + kv_lora], kvn_ref[...])
        pe = down[:, q_lora + kv_lora:]
        tab = tab_ref[r, :]
        if state_refs:
            state_refs[0][r, :] = ckv
            state_refs[1][r, :] = pe[:, :pe.shape[-1] // 2]
        kpe = jnp.where(_rope_half_mask(pe.shape), _rotary_pair(pe, tab), 0.0).astype(BF16)
        q = _dot(cq.astype(BF16), wq_ref[...]) * q_scale
        ckv_b = ckv.astype(BF16)
        kn = _dot(ckv_b, wk_ref[...])
        _store_values_with_ones(v_ref, r, _dot(ckv_b, wv_ref[...]), nope)
        for hd in range(n_heads):
            _head_store(q_ref, hd, r, 0, q[:, hd * dq:hd * dq + nope])
            _head_store(q_ref, hd, r, nope, _rotary_pair(q[:, hd * dq + nope:(hd + 1) * dq], tab))
            _head_store(k_ref, hd, r, 0, kn[:, hd * nope:(hd + 1) * nope])
            _head_store(k_ref, hd, r, nope, kpe)


def _mla_proj_call(x, mod, tokens_per_cond, wd, qn, kvn, wq, wk, wv, tab, i_odd, dims, with_state,
                   tm=512, rows=256):
    n_heads, q_lora, kv_lora, nope, rope, q_scale = dims
    t, d = x.shape
    tm = min(tm, tokens_per_cond, tab.shape[0])
    per = tokens_per_cond // tm
    tab_blocks = tab.shape[0] // tm
    dq = 2 * nope
    out_shape = [jax.ShapeDtypeStruct((n_heads, t, dq), BF16)] * 3
    out_specs = [pl.BlockSpec((n_heads, tm, dq), lambda i: (0, i, 0))] * 3
    if with_state:
        out_shape += [jax.ShapeDtypeStruct((t, kv_lora), F32), jax.ShapeDtypeStruct((t, rope), F32)]
        out_specs += [pl.BlockSpec((tm, kv_lora), lambda i: (i, 0)),
                      pl.BlockSpec((tm, rope), lambda i: (i, 0))]
    whole = lambda a: _resident((None,) + a.shape[1:], (i_odd,) + (0,) * (a.ndim - 1))
    return pl.pallas_call(
        functools.partial(_mla_proj_kernel, n_heads=n_heads, q_lora=q_lora, kv_lora=kv_lora,
                          nope=nope, q_scale=q_scale, rows=min(rows, tm)),
        out_shape=out_shape,
        grid=(t // tm,),
        in_specs=[pl.BlockSpec((tm, d), lambda i: (i, 0)),
                  pl.BlockSpec((None, 9, d), lambda i: (i // per, 0, 0)),
                  whole(wd), whole(qn), whole(kvn), whole(wq), whole(wk), whole(wv),
                  pl.BlockSpec((tm, tab.shape[1]), lambda i: (i % tab_blocks, 0))],
        out_specs=out_specs,
        compiler_params=_params("parallel"),
        name="mla_proj",
    )(x, mod, wd, qn, kvn, wq, wk, wv, tab)


def _mla_expand_kernel(ckv_ref, kpe_ref, wk_ref, wv_ref, k_ref, v_ref, *, n_heads, nope):
    ckv = ckv_ref[...].astype(BF16)
    kn = _dot(ckv, wk_ref[...])
    r = slice(None)
    _store_values_with_ones(v_ref, r, _dot(ckv, wv_ref[...]), nope)
    kpe = kpe_ref[...].astype(BF16)
    for hd in range(n_heads):
        _head_store(k_ref, hd, r, 0, kn[:, hd * nope:(hd + 1) * nope])
        _head_store(k_ref, hd, r, nope, kpe)


def _mla_expand_call(ckv, kpe_pad, wk, wv, i_odd, n_heads, nope, tm=256):
    t, kv_lora = ckv.shape
    tm = min(tm, t)
    dq = 2 * nope
    whole = lambda a: pl.BlockSpec((None,) + a.shape[1:], lambda i: (i_odd,) + (0,) * (a.ndim - 1))
    return pl.pallas_call(
        functools.partial(_mla_expand_kernel, n_heads=n_heads, nope=nope),
        out_shape=[jax.ShapeDtypeStruct((n_heads, t, dq), BF16)] * 2,
        grid=(t // tm,),
        in_specs=[pl.BlockSpec((tm, kv_lora), lambda i: (i, 0)),
                  pl.BlockSpec((tm, kpe_pad.shape[1]), lambda i: (i, 0)),
                  whole(wk), whole(wv)],
        out_specs=[pl.BlockSpec((n_heads, tm, dq), lambda i: (0, i, 0))] * 2,
        compiler_params=_params("parallel"),
        name="mla_expand",
    )(ckv, kpe_pad, wk, wv)


def _rotate_half_columns(w, rope):
    q = rope // 4
    parts = [w[..., i * q:(i + 1) * q] for i in range(4)]
    return jnp.concatenate([-parts[1], parts[0], -parts[3], parts[2]], axis=-1)


def _rope_table(n_tokens, rope):
    axis = rope // 2
    t = jnp.arange(n_tokens)
    inv = ROPE_BASE ** (-jnp.arange(0, axis, 2, dtype=F32) / axis)
    ang_r = (t // GRID_W).astype(F32)[:, None] * inv[None, :]
    ang_c = (t % GRID_W).astype(F32)[:, None] * inv[None, :]
    ang = jnp.concatenate([ang_r, ang_r, ang_c, ang_c], axis=-1)
    return jnp.concatenate([jnp.cos(ang), jnp.sin(ang)], axis=-1)


def _identity_rope_table(n_tokens, rope):
    return jnp.concatenate([jnp.ones((n_tokens, rope), F32), jnp.zeros((n_tokens, rope), F32)], axis=-1)


def kernel(x_prompt, x_sample, cache_na_k, cache_na_v, cache_mla_ckv, cache_mla_kpe, c, c_ctx,
           w_mod, b_mod, ln_g, ln_b, ffn_w1, ffn_w3, ffn_w2,
           na_w_in, mix0_w_out, na_rpb, pool_w, pool_scale,
           mla_w_down, mla_q_norm, mla_w_uq, mla_kv_norm, mla_w_ukv, mla_w_out):
    depth, d, _ = w_mod.shape
    alpha = (2 * depth) ** 0.25
    batch, seq, _ = x_prompt.shape
    dec_batch, dec_seq, _ = x_sample.shape
    na_heads, na_dh = cache_na_k.shape[-2:]
    na_width = na_heads * na_dh
    pool_groups, pool_ch = pool_w.shape[1:3]
    assert pool_groups * pool_ch == na_width and pool_ch % 128 == 0
    q_lora, kv_lora = mla_q_norm.shape[-1], mla_kv_norm.shape[-1]
    rope = cache_mla_kpe.shape[-1]
    uq_w, ukv_w, out_rows = mla_w_uq.shape[-1], mla_w_ukv.shape[-1], mla_w_out.shape[1]
    mla_heads = (uq_w - ukv_w + out_rows) // rope
    nope = uq_w // mla_heads - rope
    v_dim = out_rows // mla_heads
    assert nope == v_dim and 2 * rope == nope and nope % 128 == 0
    mla_dims = (mla_heads, q_lora, kv_lora, nope, rope, float((nope + rope) ** -0.5) * LOG2E)

    w13, w2 = _ffn_chunked(ffn_w1, ffn_w3, FFN_CHUNK), ffn_w2.astype(BF16)
    w_in = na_w_in.astype(BF16)
    w_out0 = mix0_w_out.astype(BF16)
    w_pool = pool_w.astype(BF16)
    p_scale = pool_scale.reshape(pool_scale.shape[0], 1, -1)
    kpe_cols = mla_w_down[..., q_lora + kv_lora:]
    wd = jnp.concatenate([mla_w_down, _rotate_half_columns(kpe_cols, rope)], axis=-1).astype(BF16)
    n_odd = mla_w_uq.shape[0]
    uq = mla_w_uq.reshape(n_odd, q_lora, mla_heads, nope + rope)
    wq = jnp.concatenate([uq, _rotate_half_columns(uq[..., nope:], rope)], axis=-1)
    wq = wq.reshape(n_odd, q_lora, mla_heads * 2 * nope).astype(BF16)
    ukv = mla_w_ukv.reshape(n_odd, kv_lora, mla_heads, nope + v_dim)
    wk = ukv[..., :nope].reshape(n_odd, kv_lora, mla_heads * nope).astype(BF16)
    wv = ukv[..., nope:].reshape(n_odd, kv_lora, mla_heads * v_dim).astype(BF16)
    w_out1 = mla_w_out.astype(BF16)
    qn = mla_q_norm.reshape(n_odd, 1, q_lora)
    kvn = mla_kv_norm.reshape(n_odd, 1, kv_lora)

    n_cond = 1 + dec_batch
    cond = jnp.concatenate([c_ctx[None, :], c, jnp.zeros((-n_cond % 16, d), F32)], axis=0)
    mod = _mod_call(cond, w_mod, b_mod).reshape(depth, cond.shape[0], 9, d)

    rope_tab = _rope_table(dec_seq, rope)
    ident_tab = _identity_rope_table(min(256, seq), rope)

    def trunk(x3, mod_rows, per_cond, caches):
        bsz, l, _ = x3.shape
        x = x3.reshape(bsz * l, d)
        states = []
        for layer in range(depth):
            m = mod_rows[layer]
            g = lambda s: ln_g[layer, s].reshape(1, d)
            bb = lambda s: ln_b[layer, s].reshape(1, d)
            x = _ffn_call(x, m, per_cond, w13, w2, layer, 0, g(0), bb(0), 0, alpha)
            i = layer // 2
            if layer % 2 == 0:
                outs = _na_proj_call(x, m, per_cond, w_in, i, na_heads, na_dh, pool_groups, pool_ch,
                                     float(na_dh ** -0.5) * LOG2E, caches is None)
                qkv = outs[0].reshape(3 * na_heads, bsz, l, na_dh)
                u = outs[1].reshape(pool_groups, bsz, l, pool_ch)
                if caches is None:
                    states.append((outs[2], outs[3]))
                    a = _attn_call(qkv, 0, [(qkv, na_heads, qkv, 2 * na_heads)], na_heads, na_dh,
                                   256, 512, "na_dense_attn", heads=4)
                else:
                    kc = caches[0][:, i].reshape(bsz, -1, na_width).astype(BF16)
                    vc = caches[1][:, i].reshape(bsz, -1, na_width).astype(BF16)
                    bias = _na_bias_table(na_rpb[i], l // GRID_W)
                    a = _na_call(qkv, kc, vc, bias, (na_rpb.shape[2] + 1) // 2)
                pooled = _pool_call(u, w_pool, p_scale, i)
                x = _out_call(x, m, per_cond,
                              [(a.reshape(bsz * l, na_width), w_out0, (None, na_width, d), (i, 0, 0)),
                               (pooled.reshape(bsz * l, na_width), w_out0, (None, na_width, d), (i, 1, 0))],
                              g(1), bb(1), alpha)
            else:
                tab = ident_tab if caches is None else rope_tab
                outs = _mla_proj_call(x, m, per_cond, wd, qn, kvn, wq, wk, wv, tab, i, mla_dims,
                                      caches is None)
                q, k, v = (o.reshape(mla_heads, bsz, l, 2 * nope) for o in outs[:3])
                segments = [(k, 0, v, 0)]
                if caches is None:
                    states.append((outs[3], outs[4]))
                else:
                    ckv_c = caches[2][:, i].reshape(-1, kv_lora)
                    kpe_c = caches[3][:, i].reshape(-1, rope)
                    kpe_c = jnp.concatenate([kpe_c, jnp.zeros_like(kpe_c)], axis=-1)
                    k_c, v_c = _mla_expand_call(ckv_c, kpe_c, wk, wv, i, mla_heads, nope)
                    segments.append((k_c.reshape(mla_heads, bsz, -1, 2 * nope), 0,
                                     v_c.reshape(mla_heads, bsz, -1, 2 * nope), 0))
                o = _attn_call(q, 0, segments, mla_heads, v_dim, 1024, 256, "mla_attn",
                               heads=1 if l > 1024 else 4)
                x = _out_call(x, m, per_cond,
                              [(o.reshape(bsz * l, -1), w_out1, (None,) + w_out1.shape[1:], (i, 0, 0))],
                              g(1), bb(1), alpha)
            x = _ffn_call(x, m, per_cond, w13, w2, layer, 1, g(2), bb(2), 2, alpha)
        return x.reshape(bsz, l, d), states

    y_prompt, st = trunk(x_prompt, mod[:, 0:1], batch * seq, None)
    y_sample, _ = trunk(x_sample, mod[:, 1:n_cond], dec_seq,
                        (cache_na_k, cache_na_v, cache_mla_ckv, cache_mla_kpe))
    even = [s for layer, s in enumerate(st) if layer % 2 == 0]
    odd = [s for layer, s in enumerate(st) if layer % 2 == 1]
    new_na_k = jnp.stack([s[0].reshape(batch, seq, na_heads, na_dh) for s in even], axis=1)
    new_na_v = jnp.stack([s[1].reshape(batch, seq, na_heads, na_dh) for s in even], axis=1)
    new_mla_ckv = jnp.stack([s[0].reshape(batch, seq, kv_lora) for s in odd], axis=1)
    new_mla_kpe = jnp.stack([s[1].reshape(batch, seq, rope) for s in odd], axis=1)
    return (y_prompt, y_sample, new_na_k, new_na_v, new_mla_ckv, new_mla_kpe)
```

```python
import functools
import math

import numpy as np
import jax
import jax.numpy as jnp
from jax import lax
from jax.experimental import pallas as pl
from jax.experimental.pallas import tpu as pltpu

GRID_W = 64
LN_EPS = 1e-5
RMS_EPS = 1e-6
POOL_WINDOWS = (2, 4, 8, 16)
ROPE_BASE = 10000.0
MASKED = -1e30
LOG2E = math.log2(math.e)
FFN_CHUNK = 512
ATTN_ROWS = 256
NA_GROUP_ROWS = 4
NA_WINDOW_ROWS = 12
NA_KEY_CHUNK = 256
V7X_VMEM_LIMIT_BYTES = 56 * 1024 * 1024

F32 = jnp.float32
BF16 = jnp.bfloat16
_NT = (((1,), (1,)), ((), ()))


def _params(*semantics):
    return pltpu.CompilerParams(dimension_semantics=semantics,
                                vmem_limit_bytes=V7X_VMEM_LIMIT_BYTES)


def _dot(a, b):
    return jnp.dot(a, b, preferred_element_type=F32)


def _post_norm(x, update, g, b, alpha):
    z = alpha * x + update
    mu = jnp.mean(z, axis=-1, keepdims=True)
    zc = z - mu
    var = jnp.mean(zc * zc, axis=-1, keepdims=True)
    return zc * lax.rsqrt(var + LN_EPS) * g + b


def _modulated(x, m, j):
    return x * (1.0 + m[3 * j + 1:3 * j + 2]) + m[3 * j:3 * j + 1]


def _mod_kernel(c_ref, w_ref, b_ref, o_ref):
    c = c_ref[...]
    h = (c * jax.nn.sigmoid(c)).astype(BF16)
    o_ref[...] = _dot(h, w_ref[...].astype(BF16)) + b_ref[...]


def _mod_call(cond, w_mod, b_mod, tn=1024):
    depth, d, n = w_mod.shape
    r = cond.shape[0]
    return pl.pallas_call(
        _mod_kernel,
        out_shape=jax.ShapeDtypeStruct((depth, r, n), F32),
        grid=(depth, n // tn),
        in_specs=[pl.BlockSpec((r, d), lambda l, j: (0, 0)),
                  pl.BlockSpec((None, d, tn), lambda l, j: (l, 0, j)),
                  pl.BlockSpec((None, 1, tn), lambda l, j: (l, 0, j))],
        out_specs=pl.BlockSpec((None, r, tn), lambda l, j: (l, 0, j)),
        compiler_params=_params("parallel", "parallel"),
        name="mod_proj",
    )(cond, w_mod, b_mod.reshape(depth, 1, n))


def _ffn_kernel(x_ref, mod_ref, w13_ref, w2_ref, g_ref, b_ref, o_ref, h_ref, *, j, alpha, rows, mid_rows):
    f = pl.program_id(1)
    last = pl.num_programs(1) - 1
    tm = x_ref.shape[0]
    tf = w2_ref.shape[0]

    def chunk_update(h):
        ab = _dot(h, w13_ref[...])
        a, b = ab[:, :tf], ab[:, tf:]
        act = (a * jax.nn.sigmoid(a) * b).astype(BF16)
        return _dot(act, w2_ref[...])

    @pl.when(f == 0)
    def _():
        m = mod_ref[...]
        for r0 in range(0, tm, rows):
            h = _modulated(x_ref[r0:r0 + rows, :], m, j).astype(BF16)
            h_ref[r0:r0 + rows, :] = h
            o_ref[r0:r0 + rows, :] = chunk_update(h)

    @pl.when((f > 0) & (f < last))
    def _():
        for r0 in range(0, tm, mid_rows):
            o_ref[r0:r0 + mid_rows, :] += chunk_update(h_ref[r0:r0 + mid_rows, :])

    @pl.when(f == last)
    def _():
        half_gate = 0.5 * mod_ref[...][3 * j + 2:3 * j + 3]
        for r0 in range(0, tm, rows):
            y = o_ref[r0:r0 + rows, :] + chunk_update(h_ref[r0:r0 + rows, :])
            o_ref[r0:r0 + rows, :] = _post_norm(x_ref[r0:r0 + rows, :], half_gate * y,
                                                g_ref[...], b_ref[...], alpha)


def _ffn_pack_kernel(w1_ref, w3_ref, o_ref):
    tf = w1_ref.shape[-1]
    o_ref[:, :tf] = w1_ref[...].astype(BF16)
    o_ref[:, tf:] = w3_ref[...].astype(BF16)


def _ffn_chunked(w1, w3, tf):
    depth, n_sub, d, ff = w1.shape
    tf = min(tf, ff)
    assert ff % tf == 0 and ff // tf >= 2
    spec = pl.BlockSpec((None, None, d, tf), lambda l, s, f: (l, s, 0, f))
    return pl.pallas_call(
        _ffn_pack_kernel,
        out_shape=jax.ShapeDtypeStruct((depth, n_sub, ff // tf, d, 2 * tf), BF16),
        grid=(depth, n_sub, ff // tf),
        in_specs=[spec, spec],
        out_specs=pl.BlockSpec((None, None, None, d, 2 * tf), lambda l, s, f: (l, s, f, 0, 0)),
        compiler_params=_params("parallel", "parallel", "parallel"),
        name="ffn_pack",
    )(w1, w3)


def _ffn_call(x, mod, tokens_per_cond, w13, w2, l, s, ln_g, ln_b, j, alpha, tm=1024, rows=256,
              mid_rows=512):
    t, d = x.shape
    n_chunks = w13.shape[2]
    tf = w13.shape[-1] // 2
    tm = min(tm, tokens_per_cond)
    assert tokens_per_cond % tm == 0
    per = tokens_per_cond // tm
    return pl.pallas_call(
        functools.partial(_ffn_kernel, j=j, alpha=alpha, rows=min(rows, tm), mid_rows=min(mid_rows, tm)),
        out_shape=jax.ShapeDtypeStruct((t, d), F32),
        grid=(t // tm, n_chunks),
        in_specs=[pl.BlockSpec((tm, d), lambda i, f: (i, 0)),
                  pl.BlockSpec((None, 9, d), lambda i, f: (i // per, 0, 0)),
                  pl.BlockSpec((None, None, None, d, 2 * tf), lambda i, f: (l, s, f, 0, 0)),
                  pl.BlockSpec((None, None, tf, d), lambda i, f: (l, s, f, 0)),
                  pl.BlockSpec((1, d), lambda i, f: (0, 0)),
                  pl.BlockSpec((1, d), lambda i, f: (0, 0))],
        out_specs=pl.BlockSpec((tm, d), lambda i, f: (i, 0)),
        scratch_shapes=[pltpu.VMEM((tm, d), BF16)],
        compiler_params=_params("parallel", "arbitrary"),
        name="ffn",
    )(x, mod, w13, w2, ln_g, ln_b)


def _na_proj_kernel(x_ref, mod_ref, w_ref, qkv_ref, u_ref, *kv_refs, q_scale, rows):
    m = mod_ref[...]
    n_heads, dh = qkv_ref.shape[0] // 3, qkv_ref.shape[-1]
    n_pool, ch = u_ref.shape[0], u_ref.shape[-1]
    width = n_heads * dh
    for r0 in range(0, x_ref.shape[0], rows):
        r = slice(r0, r0 + rows)
        h = _modulated(x_ref[r, :], m, 1).astype(BF16)
        for n in range(3):
            y = _dot(h, w_ref[:, n * width:(n + 1) * width])
            if kv_refs and n > 0:
                kv_refs[n - 1][r, :] = y
            y = (y * q_scale if n == 0 else y).astype(BF16)
            for hd in range(n_heads):
                qkv_ref[n * n_heads + hd, r, :] = y[:, hd * dh:(hd + 1) * dh]
        y = _dot(h, w_ref[:, 3 * width:]).astype(BF16)
        for g in range(n_pool):
            u_ref[g, r, :] = y[:, g * ch:(g + 1) * ch]


def _resident(block_shape, index):
    return pl.BlockSpec(block_shape, lambda *_: index, pipeline_mode=pl.Buffered(1))


def _na_proj_call(x, mod, tokens_per_cond, w_in, i_even, n_heads, dh, n_pool, ch, q_scale, with_kv,
                  tm=512, rows=256):
    t, d = x.shape
    n_in = w_in.shape[-1]
    na_width = n_heads * dh
    assert n_in == 3 * na_width + n_pool * ch
    tm = min(tm, tokens_per_cond)
    per = tokens_per_cond // tm
    out_shape = [jax.ShapeDtypeStruct((3 * n_heads, t, dh), BF16),
                 jax.ShapeDtypeStruct((n_pool, t, ch), BF16)]
    out_specs = [pl.BlockSpec((3 * n_heads, tm, dh), lambda i: (0, i, 0)),
                 pl.BlockSpec((n_pool, tm, ch), lambda i: (0, i, 0))]
    if with_kv:
        out_shape += [jax.ShapeDtypeStruct((t, na_width), F32)] * 2
        out_specs += [pl.BlockSpec((tm, na_width), lambda i: (i, 0))] * 2
    return pl.pallas_call(
        functools.partial(_na_proj_kernel, q_scale=q_scale, rows=min(rows, tm)),
        out_shape=out_shape,
        grid=(t // tm,),
        in_specs=[pl.BlockSpec((tm, d), lambda i: (i, 0)),
                  pl.BlockSpec((None, 9, d), lambda i: (i // per, 0, 0)),
                  _resident((None, d, n_in), (i_even, 0, 0))],
        out_specs=out_specs,
        compiler_params=_params("parallel"),
        name="na_proj",
    )(x, mod, w_in)


def _attn_rows(q, kv_refs, chunk, dv):
    rows = q.shape[0]
    den_in_v = kv_refs[1].shape[-1] == 2 * dv
    m = jnp.full((rows, 1), MASKED, F32)
    acc = jnp.zeros((rows, kv_refs[1].shape[-1]), F32)
    den = jnp.zeros((rows, 1), F32)
    for k_ref, v_ref in zip(kv_refs[0::2], kv_refs[1::2]):
        lk = k_ref.shape[0]
        step = min(chunk, lk)
        for c0 in range(0, lk, step):
            s = lax.dot_general(q, k_ref[c0:c0 + step, :], _NT, preferred_element_type=F32)
            m_new = jnp.maximum(m, jnp.max(s, axis=-1, keepdims=True))
            rescale = jnp.exp2(m - m_new)
            p = jnp.exp2(s - m_new)
            if not den_in_v:
                den = rescale * den + jnp.sum(p, axis=-1, keepdims=True)
            acc = rescale * acc + _dot(p.astype(BF16), v_ref[c0:c0 + step, :])
            m = m_new
    return acc[:, :dv] / (acc[:, dv:] if den_in_v else den)


def _attn_kernel(*refs, chunk, dv, rows):
    q_ref, o_ref = refs[0], refs[-1]
    for hd in range(q_ref.shape[0]):
        kv_refs = [ref.at[hd] for ref in refs[1:-1]]
        for r0 in range(0, q_ref.shape[1], rows):
            o = _attn_rows(q_ref[hd, r0:r0 + rows, :], kv_refs, chunk, dv)
            o_ref[r0:r0 + rows, hd * dv:(hd + 1) * dv] = o.astype(o_ref.dtype)


def _attn_call(q, q_head0, segments, n_heads, dv, tq, chunk, name, heads=1):
    _, b, l, dq = q.shape
    tq = min(tq, l)
    heads = math.gcd(heads, n_heads, q_head0, *[h0 for _, k0, _, v0 in segments for h0 in (k0, v0)])
    in_specs = [pl.BlockSpec((heads, None, tq, dq),
                             lambda bi, h, qi: (q_head0 // heads + h, bi, qi, 0))]
    args = [q]
    for k, k0, v, v0 in segments:
        lk = k.shape[2]
        assert lk % min(chunk, lk) == 0
        in_specs.append(pl.BlockSpec((heads, None, lk, dq),
                                     lambda bi, h, qi, k0=k0: (k0 // heads + h, bi, 0, 0)))
        in_specs.append(pl.BlockSpec((heads, None, lk, v.shape[-1]),
                                     lambda bi, h, qi, v0=v0: (v0 // heads + h, bi, 0, 0)))
        args += [k, v]
    return pl.pallas_call(
        functools.partial(_attn_kernel, chunk=chunk, dv=dv, rows=min(ATTN_ROWS, tq)),
        out_shape=jax.ShapeDtypeStruct((b, l, n_heads * dv), BF16),
        grid=(b, n_heads // heads, l // tq),
        in_specs=in_specs,
        out_specs=pl.BlockSpec((None, tq, heads * dv), lambda bi, h, qi: (bi, qi, h)),
        compiler_params=_params("parallel", "parallel", "arbitrary"),
        name=name,
    )(*args)


def _na_kernel(q_ref, k_ref, v_ref, kc_ref, vc_ref, bias_ref, o_ref, *, n_rows, half_rows, groups):
    n_groups = n_rows // NA_GROUP_ROWS
    tq = NA_GROUP_ROWS * GRID_W
    win = NA_WINDOW_ROWS * GRID_W
    for c in range(groups):
        g = pl.program_id(2) * groups + c
        base_row = jnp.clip(g * NA_GROUP_ROWS - half_rows, 0, n_rows - NA_WINDOW_ROWS)
        base = pl.multiple_of(base_row * GRID_W, GRID_W)
        variant = jnp.where(g == 0, 0, jnp.where(g == n_groups - 1, 2, 1))
        q = q_ref[c * tq:(c + 1) * tq, :]
        m = jnp.full((tq, 1), MASKED, F32)
        acc = jnp.zeros((tq, o_ref.shape[-1]), F32)
        den = jnp.zeros((tq, 1), F32)
        chunks = [(k_ref, v_ref, base + c0, c0) for c0 in range(0, win, NA_KEY_CHUNK)]
        chunks += [(kc_ref, vc_ref, c0, None) for c0 in range(0, kc_ref.shape[0], NA_KEY_CHUNK)]
        for kr, vr, k0, b0 in chunks:
            s = lax.dot_general(q, kr[pl.ds(k0, NA_KEY_CHUNK), :], _NT, preferred_element_type=F32)
            if b0 is not None:
                s = s + bias_ref[variant, :, b0:b0 + NA_KEY_CHUNK]
            m_new = jnp.maximum(m, jnp.max(s, axis=-1, keepdims=True))
            rescale = jnp.exp2(m - m_new)
            p = jnp.exp2(s - m_new)
            den = rescale * den + jnp.sum(p, axis=-1, keepdims=True)
            acc = rescale * acc + _dot(p.astype(BF16), vr[pl.ds(k0, NA_KEY_CHUNK), :])
            m = m_new
        o_ref[c * tq:(c + 1) * tq, :] = (acc / den).astype(o_ref.dtype)


def _na_group_rows(g, n_rows, na_rows):
    kr = min(na_rows, n_rows)
    base = int(np.clip(g * NA_GROUP_ROWS - kr // 2, 0, n_rows - NA_WINDOW_ROWS))
    rq = (g * NA_GROUP_ROWS + np.arange(NA_GROUP_ROWS))[:, None]
    rk = (base + np.arange(NA_WINDOW_ROWS))[None, :]
    rs = np.clip(rq - kr // 2, 0, n_rows - kr)
    ok = (rk >= rs) & (rk < rs + kr)
    assert (ok.sum(axis=1) == kr).all(), "key window misses part of a neighbourhood"
    return np.where(ok, rk - rq + na_rows - 1, 0), ok


def _na_bias_table(rpb, n_rows):
    n_heads, nr2, nc2 = rpb.shape
    na_rows, na_cols = (nr2 + 1) // 2, (nc2 + 1) // 2
    n_groups = n_rows // NA_GROUP_ROWS
    per_group = [_na_group_rows(g, n_rows, na_rows) for g in range(n_groups)]
    for g in range(2, n_groups - 1):
        assert all((a == b).all() for a, b in zip(per_group[g], per_group[1])), "interior groups differ"
    idx_r, row_ok = (np.stack([per_group[g][i] for g in (0, 1, n_groups - 1)]) for i in range(2))
    w = GRID_W
    e = jnp.pad(rpb, ((0, 0), (0, 0), (w - na_cols, 2 * w - (w - na_cols) - nc2)))
    band = jnp.tile(e, (1, 1, w))[..., :w * (2 * w - 1)].reshape(n_heads, nr2, w, 2 * w - 1)[..., w - 1:]
    cq, ck = np.arange(w)[:, None], np.arange(w)[None, :]
    cs = np.clip(cq - na_cols // 2, 0, w - na_cols)
    col_ok = (ck >= cs) & (ck < cs + na_cols)
    blocks = jnp.take(band, jnp.asarray(idx_r.reshape(-1)), axis=1)
    blocks = blocks.reshape((n_heads,) + idx_r.shape + (w, w))
    ok = row_ok[:, :, :, None, None] & col_ok[None, None, None]
    table = jnp.where(ok[None], blocks * LOG2E, MASKED).transpose(0, 1, 2, 4, 3, 5)
    return table.reshape(n_heads, 3, NA_GROUP_ROWS * w, NA_WINDOW_ROWS * w).astype(F32)


def _na_call(qkv, kc, vc, bias, na_rows, groups=8):
    n_heads, b, l, dh = qkv.shape[0] // 3, *qkv.shape[1:]
    n_rows = l // GRID_W
    assert n_rows % NA_GROUP_ROWS == 0 and n_rows >= NA_WINDOW_ROWS + NA_GROUP_ROWS
    assert NA_GROUP_ROWS - 1 + min(na_rows, n_rows) <= NA_WINDOW_ROWS
    n_groups = n_rows // NA_GROUP_ROWS
    groups = math.gcd(n_groups, groups)
    tq = groups * NA_GROUP_ROWS * GRID_W
    lc = kc.shape[1]
    return pl.pallas_call(
        functools.partial(_na_kernel, n_rows=n_rows, half_rows=min(na_rows, n_rows) // 2,
                          groups=groups),
        out_shape=jax.ShapeDtypeStruct((b, l, n_heads * dh), BF16),
        grid=(b, n_heads, n_groups // groups),
        in_specs=[pl.BlockSpec((None, None, tq, dh), lambda bi, h, g: (h, bi, g, 0)),
                  pl.BlockSpec((None, None, l, dh), lambda bi, h, g: (n_heads + h, bi, 0, 0)),
                  pl.BlockSpec((None, None, l, dh), lambda bi, h, g: (2 * n_heads + h, bi, 0, 0)),
                  pl.BlockSpec((None, lc, dh), lambda bi, h, g: (bi, 0, h)),
                  pl.BlockSpec((None, lc, dh), lambda bi, h, g: (bi, 0, h)),
                  pl.BlockSpec((None,) + bias.shape[1:], lambda bi, h, g: (h, 0, 0, 0))],
        out_specs=pl.BlockSpec((None, tq, dh), lambda bi, h, g: (bi, g, h)),
        compiler_params=_params("parallel", "parallel", "arbitrary"),
        name="na_attn",
    )(qkv, qkv, qkv, kc, vc, bias)


def _pool_kernel(u_ref, w_ref, s_ref, o_ref, *, chunk, window):
    l = u_ref.shape[0]
    half = lax.shift_left(jnp.int32(1), pl.program_id(1).astype(jnp.int32))
    w = w_ref[...]
    scale = s_ref[...]

    def body(c, carry):
        t0 = pl.multiple_of(c * chunk, chunk)
        s0 = pl.multiple_of(jnp.clip(t0 - (window - chunk) // 2, 0, l - window), 16)
        t = t0 + lax.broadcasted_iota(jnp.int32, (chunk, 1), 0)
        lo = jnp.maximum(t - half, 0)
        hi = jnp.minimum(t + half, l)
        pos = s0 + lax.broadcasted_iota(jnp.int32, (1, window), 1)
        band = jnp.where((pos >= lo) & (pos < hi), 1.0, 0.0).astype(BF16)
        sums = _dot(band, u_ref[pl.ds(s0, window), :])
        mean = sums / (hi - lo).astype(F32)
        d = (mean - u_ref[pl.ds(t0, chunk), :].astype(F32)).astype(BF16)
        o_ref[pl.ds(t0, chunk), :] = (_dot(d, w) * scale).astype(o_ref.dtype)
        return carry

    lax.fori_loop(0, l // chunk, body, 0, unroll=min(4, l // chunk))


def _pool_call(u, w_pool, pool_scale, i_even):
    _, b, l, _ = u.shape
    n_groups, ch, _ = w_pool.shape[1:]
    assert n_groups == len(POOL_WINDOWS) and all(w == 2 << g for g, w in enumerate(POOL_WINDOWS))
    chunk = min(256, l)
    window = min(2 * chunk, l)
    assert l % chunk == 0 and (window == l or window - chunk >= max(POOL_WINDOWS))
    return pl.pallas_call(
        functools.partial(_pool_kernel, chunk=chunk, window=window),
        out_shape=jax.ShapeDtypeStruct((b, l, n_groups * ch), BF16),
        grid=(b, n_groups),
        in_specs=[pl.BlockSpec((None, None, l, ch), lambda bi, g: (g, bi, 0, 0)),
                  pl.BlockSpec((None, None, ch, ch), lambda bi, g: (i_even, g, 0, 0)),
                  pl.BlockSpec((None, 1, ch), lambda bi, g: (i_even, 0, g))],
        out_specs=pl.BlockSpec((None, l, ch), lambda bi, g: (bi, 0, g)),
        compiler_params=_params("parallel", "parallel"),
        name="pool",
    )(u, w_pool, pool_scale)


def _out_kernel(*refs, n_lhs, alpha, rows):
    x_ref, mod_ref = refs[0], refs[1]
    g_ref, b_ref, o_ref = refs[-3], refs[-2], refs[-1]
    gate = mod_ref[...][5:6]
    for r0 in range(0, x_ref.shape[0], rows):
        r = slice(r0, r0 + rows)
        y = functools.reduce(jnp.add, [_dot(refs[2 + 2 * i][r, :], refs[3 + 2 * i][...])
                                       for i in range(n_lhs)])
        o_ref[r, :] = _post_norm(x_ref[r, :], gate * y, g_ref[...], b_ref[...], alpha)


def _out_call(x, mod, tokens_per_cond, lhs_w, ln_g, ln_b, alpha, tm=1024, rows=256):
    t, d = x.shape
    tm = min(tm, tokens_per_cond)
    per = tokens_per_cond // tm
    in_specs = [pl.BlockSpec((tm, d), lambda i: (i, 0)),
                pl.BlockSpec((None, 9, d), lambda i: (i // per, 0, 0))]
    args = [x, mod]
    for lhs, w, w_block, w_index in lhs_w:
        in_specs.append(pl.BlockSpec((tm, lhs.shape[1]), lambda i: (i, 0)))
        in_specs.append(_resident(w_block, w_index))
        args += [lhs, w]
    in_specs += [pl.BlockSpec((1, d), lambda i: (0, 0))] * 2
    args += [ln_g, ln_b]
    return pl.pallas_call(
        functools.partial(_out_kernel, n_lhs=len(lhs_w), alpha=alpha, rows=min(rows, tm)),
        out_shape=jax.ShapeDtypeStruct((t, d), F32),
        grid=(t // tm,),
        in_specs=in_specs,
        out_specs=pl.BlockSpec((tm, d), lambda i: (i, 0)),
        compiler_params=_params("parallel"),
        name="out_proj",
    )(*args)


def _rms(x, g):
    return x * lax.rsqrt(jnp.mean(x * x, axis=-1, keepdims=True) + RMS_EPS) * g


def _rotary_pair(t, tab):
    u = t * tab
    return u + pltpu.roll(u, u.shape[-1] // 2, 1)


def _rope_half_mask(shape):
    return lax.broadcasted_iota(jnp.int32, shape, 1) < shape[-1] // 2


def _head_store(ref, h, r, lane0, value):
    ref[h, r, lane0:lane0 + value.shape[-1]] = value.astype(ref.dtype)


def _store_values_with_ones(v_ref, r, v, dv):
    ones = jnp.ones((v.shape[0], dv), v_ref.dtype)
    for hd in range(v_ref.shape[0]):
        _head_store(v_ref, hd, r, 0, v[:, hd * dv:(hd + 1) * dv])
        _head_store(v_ref, hd, r, dv, ones)


def _mla_proj_kernel(x_ref, mod_ref, wd_ref, qn_ref, kvn_ref, wq_ref, wk_ref, wv_ref, tab_ref,
                     q_ref, k_ref, v_ref, *state_refs, n_heads, q_lora, kv_lora, nope, q_scale, rows):
    m = mod_ref[...]
    dq = 2 * nope
    for r0 in range(0, x_ref.shape[0], rows):
        r = slice(r0, r0 + rows)
        h = _modulated(x_ref[r, :], m, 1).astype(BF16)
        down = _dot(h, wd_ref[...])
        cq = _rms(down[:, :q_lora], qn_ref[...])
        ckv = _rms(down[:, q_lora:q_lora + kv_lora], kvn_ref[...])
        pe = down[:, q_lora + kv_lora:]
        tab = tab_ref[r, :]
        if state_refs:
            state_refs[0][r, :] = ckv
            state_refs[1][r, :] = pe[:, :pe.shape[-1] // 2]
        kpe = jnp.where(_rope_half_mask(pe.shape), _rotary_pair(pe, tab), 0.0).astype(BF16)
        q = _dot(cq.astype(BF16), wq_ref[...]) * q_scale
        ckv_b = ckv.astype(BF16)
        kn = _dot(ckv_b, wk_ref[...])
        _store_values_with_ones(v_ref, r, _dot(ckv_b, wv_ref[...]), nope)
        for hd in range(n_heads):
            _head_store(q_ref, hd, r, 0, q[:, hd * dq:hd * dq + nope])
            _head_store(q_ref, hd, r, nope, _rotary_pair(q[:, hd * dq + nope:(hd + 1) * dq], tab))
            _head_store(k_ref, hd, r, 0, kn[:, hd * nope:(hd + 1) * nope])
            _head_store(k_ref, hd, r, nope, kpe)


def _mla_proj_call(x, mod, tokens_per_cond, wd, qn, kvn, wq, wk, wv, tab, i_odd, dims, with_state,
                   tm=512, rows=256):
    n_heads, q_lora, kv_lora, nope, rope, q_scale = dims
    t, d = x.shape
    tm = min(tm, tokens_per_cond, tab.shape[0])
    per = tokens_per_cond // tm
    tab_blocks = tab.shape[0] // tm
    dq = 2 * nope
    out_shape = [jax.ShapeDtypeStruct((n_heads, t, dq), BF16)] * 3
    out_specs = [pl.BlockSpec((n_heads, tm, dq), lambda i: (0, i, 0))] * 3
    if with_state:
        out_shape += [jax.ShapeDtypeStruct((t, kv_lora), F32), jax.ShapeDtypeStruct((t, rope), F32)]
        out_specs += [pl.BlockSpec((tm, kv_lora), lambda i: (i, 0)),
                      pl.BlockSpec((tm, rope), lambda i: (i, 0))]
    whole = lambda a: _resident((None,) + a.shape[1:], (i_odd,) + (0,) * (a.ndim - 1))
    return pl.pallas_call(
        functools.partial(_mla_proj_kernel, n_heads=n_heads, q_lora=q_lora, kv_lora=kv_lora,
                          nope=nope, q_scale=q_scale, rows=min(rows, tm)),
        out_shape=out_shape,
        grid=(t // tm,),
        in_specs=[pl.BlockSpec((tm, d), lambda i: (i, 0)),
                  pl.BlockSpec((None, 9, d), lambda i: (i // per, 0, 0)),
                  whole(wd), whole(qn), whole(kvn), whole(wq), whole(wk), whole(wv),
                  pl.BlockSpec((tm, tab.shape[1]), lambda i: (i % tab_blocks, 0))],
        out_specs=out_specs,
        compiler_params=_params("parallel"),
        name="mla_proj",
    )(x, mod, wd, qn, kvn, wq, wk, wv, tab)


def _mla_expand_kernel(ckv_ref, kpe_ref, wk_ref, wv_ref, k_ref, v_ref, *, n_heads, nope):
    ckv = ckv_ref[...].astype(BF16)
    kn = _dot(ckv, wk_ref[...])
    r = slice(None)
    _store_values_with_ones(v_ref, r, _dot(ckv, wv_ref[...]), nope)
    kpe = kpe_ref[...].astype(BF16)
    for hd in range(n_heads):
        _head_store(k_ref, hd, r, 0, kn[:, hd * nope:(hd + 1) * nope])
        _head_store(k_ref, hd, r, nope, kpe)


def _mla_expand_call(ckv, kpe_pad, wk, wv, i_odd, n_heads, nope, tm=256):
    t, kv_lora = ckv.shape
    tm = min(tm, t)
    dq = 2 * nope
    whole = lambda a: pl.BlockSpec((None,) + a.shape[1:], lambda i: (i_odd,) + (0,) * (a.ndim - 1))
    return pl.pallas_call(
        functools.partial(_mla_expand_kernel, n_heads=n_heads, nope=nope),
        out_shape=[jax.ShapeDtypeStruct((n_heads, t, dq), BF16)] * 2,
        grid=(t // tm,),
        in_specs=[pl.BlockSpec((tm, kv_lora), lambda i: (i, 0)),
                  pl.BlockSpec((tm, kpe_pad.shape[1]), lambda i: (i, 0)),
                  whole(wk), whole(wv)],
        out_specs=[pl.BlockSpec((n_heads, tm, dq), lambda i: (0, i, 0))] * 2,
        compiler_params=_params("parallel"),
        name="mla_expand",
    )(ckv, kpe_pad, wk, wv)


def _rotate_half_columns(w, rope):
    q = rope // 4
    parts = [w[..., i * q:(i + 1) * q] for i in range(4)]
    return jnp.concatenate([-parts[1], parts[0], -parts[3], parts[2]], axis=-1)


def _rope_table(n_tokens, rope):
    axis = rope // 2
    t = jnp.arange(n_tokens)
    inv = ROPE_BASE ** (-jnp.arange(0, axis, 2, dtype=F32) / axis)
    ang_r = (t // GRID_W).astype(F32)[:, None] * inv[None, :]
    ang_c = (t % GRID_W).astype(F32)[:, None] * inv[None, :]
    ang = jnp.concatenate([ang_r, ang_r, ang_c, ang_c], axis=-1)
    return jnp.concatenate([jnp.cos(ang), jnp.sin(ang)], axis=-1)


def _identity_rope_table(n_tokens, rope):
    return jnp.concatenate([jnp.ones((n_tokens, rope), F32), jnp.zeros((n_tokens, rope), F32)], axis=-1)


def kernel(x_prompt, x_sample, cache_na_k, cache_na_v, cache_mla_ckv, cache_mla_kpe, c, c_ctx,
           w_mod, b_mod, ln_g, ln_b, ffn_w1, ffn_w3, ffn_w2,
           na_w_in, mix0_w_out, na_rpb, pool_w, pool_scale,
           mla_w_down, mla_q_norm, mla_w_uq, mla_kv_norm, mla_w_ukv, mla_w_out):
    depth, d, _ = w_mod.shape
    alpha = (2 * depth) ** 0.25
    batch, seq, _ = x_prompt.shape
    dec_batch, dec_seq, _ = x_sample.shape
    na_heads, na_dh = cache_na_k.shape[-2:]
    na_width = na_heads * na_dh
    pool_groups, pool_ch = pool_w.shape[1:3]
    assert pool_groups * pool_ch == na_width and pool_ch % 128 == 0
    q_lora, kv_lora = mla_q_norm.shape[-1], mla_kv_norm.shape[-1]
    rope = cache_mla_kpe.shape[-1]
    uq_w, ukv_w, out_rows = mla_w_uq.shape[-1], mla_w_ukv.shape[-1], mla_w_out.shape[1]
    mla_heads = (uq_w - ukv_w + out_rows) // rope
    nope = uq_w // mla_heads - rope
    v_dim = out_rows // mla_heads
    assert nope == v_dim and 2 * rope == nope and nope % 128 == 0
    mla_dims = (mla_heads, q_lora, kv_lora, nope, rope, float((nope + rope) ** -0.5) * LOG2E)

    w13, w2 = _ffn_chunked(ffn_w1, ffn_w3, FFN_CHUNK), ffn_w2.astype(BF16)
    w_in = na_w_in.astype(BF16)
    w_out0 = mix0_w_out.astype(BF16)
    w_pool = pool_w.astype(BF16)
    p_scale = pool_scale.reshape(pool_scale.shape[0], 1, -1)
    kpe_cols = mla_w_down[..., q_lora + kv_lora:]
    wd = jnp.concatenate([mla_w_down, _rotate_half_columns(kpe_cols, rope)], axis=-1).astype(BF16)
    n_odd = mla_w_uq.shape[0]
    uq = mla_w_uq.reshape(n_odd, q_lora, mla_heads, nope + rope)
    wq = jnp.concatenate([uq, _rotate_half_columns(uq[..., nope:], rope)], axis=-1)
    wq = wq.reshape(n_odd, q_lora, mla_heads * 2 * nope).astype(BF16)
    ukv = mla_w_ukv.reshape(n_odd, kv_lora, mla_heads, nope + v_dim)
    wk = ukv[..., :nope].reshape(n_odd, kv_lora, mla_heads * nope).astype(BF16)
    wv = ukv[..., nope:].reshape(n_odd, kv_lora, mla_heads * v_dim).astype(BF16)
    w_out1 = mla_w_out.astype(BF16)
    qn = mla_q_norm.reshape(n_odd, 1, q_lora)
    kvn = mla_kv_norm.reshape(n_odd, 1, kv_lora)

    n_cond = 1 + dec_batch
    cond = jnp.concatenate([c_ctx[None, :], c, jnp.zeros((-n_cond % 16, d), F32)], axis=0)
    mod = _mod_call(cond, w_mod, b_mod).reshape(depth, cond.shape[0], 9, d)

    rope_tab = _rope_table(dec_seq, rope)
    ident_tab = _identity_rope_table(min(256, seq), rope)

    def trunk(x3, mod_rows, per_cond, caches):
        bsz, l, _ = x3.shape
        x = x3.reshape(bsz * l, d)
        states = []
        for layer in range(depth):
            m = mod_rows[layer]
            g = lambda s: ln_g[layer, s].reshape(1, d)
            bb = lambda s: ln_b[layer, s].reshape(1, d)
            x = _ffn_call(x, m, per_cond, w13, w2, layer, 0, g(0), bb(0), 0, alpha)
            i = layer // 2
            if layer % 2 == 0:
                outs = _na_proj_call(x, m, per_cond, w_in, i, na_heads, na_dh, pool_groups, pool_ch,
                                     float(na_dh ** -0.5) * LOG2E, caches is None)
                qkv = outs[0].reshape(3 * na_heads, bsz, l, na_dh)
                u = outs[1].reshape(pool_groups, bsz, l, pool_ch)
                if caches is None:
                    states.append((outs[2], outs[3]))
                    a = _attn_call(qkv, 0, [(qkv, na_heads, qkv, 2 * na_heads)], na_heads, na_dh,
                                   256, 512, "na_dense_attn", heads=4)
                else:
                    kc = caches[0][:, i].reshape(bsz, -1, na_width).astype(BF16)
                    vc = caches[1][:, i].reshape(bsz, -1, na_width).astype(BF16)
                    bias = _na_bias_table(na_rpb[i], l // GRID_W)
                    a = _na_call(qkv, kc, vc, bias, (na_rpb.shape[2] + 1) // 2)
                pooled = _pool_call(u, w_pool, p_scale, i)
                x = _out_call(x, m, per_cond,
                              [(a.reshape(bsz * l, na_width), w_out0, (None, na_width, d), (i, 0, 0)),
                               (pooled.reshape(bsz * l, na_width), w_out0, (None, na_width, d), (i, 1, 0))],
                              g(1), bb(1), alpha)
            else:
                tab = ident_tab if caches is None else rope_tab
                outs = _mla_proj_call(x, m, per_cond, wd, qn, kvn, wq, wk, wv, tab, i, mla_dims,
                                      caches is None)
                q, k, v = (o.reshape(mla_heads, bsz, l, 2 * nope) for o in outs[:3])
                segments = [(k, 0, v, 0)]
                if caches is None:
                    states.append((outs[3], outs[4]))
                else:
                    ckv_c = caches[2][:, i].reshape(-1, kv_lora)
                    kpe_c = caches[3][:, i].reshape(-1, rope)
                    kpe_c = jnp.concatenate([kpe_c, jnp.zeros_like(kpe_c)], axis=-1)
                    k_c, v_c = _mla_expand_call(ckv_c, kpe_c, wk, wv, i, mla_heads, nope)
                    segments.append((k_c.reshape(mla_heads, bsz, -1, 2 * nope), 0,
                                     v_c.reshape(mla_heads, bsz, -1, 2 * nope), 0))
                o = _attn_call(q, 0, segments, mla_heads, v_dim, 1024, 256, "mla_attn",
                               heads=1 if l > 1024 else 4)
                x = _out_call(x, m, per_cond,
                              [(o.reshape(bsz * l, -1), w_out1, (None,) + w_out1.shape[1:], (i, 0, 0))],
                              g(1), bb(1), alpha)
            x = _ffn_call(x, m, per_cond, w13, w2, layer, 1, g(2), bb(2), 2, alpha)
        return x.reshape(bsz, l, d), states

    y_prompt, st = trunk(x_prompt, mod[:, 0:1], batch * seq, None)
    y_sample, _ = trunk(x_sample, mod[:, 1:n_cond], dec_seq,
                        (cache_na_k, cache_na_v, cache_mla_ckv, cache_mla_kpe))
    even = [s for layer, s in enumerate(st) if layer % 2 == 0]
    odd = [s for layer, s in enumerate(st) if layer % 2 == 1]
    new_na_k = jnp.stack([s[0].reshape(batch, seq, na_heads, na_dh) for s in even], axis=1)
    new_na_v = jnp.stack([s[1].reshape(batch, seq, na_heads, na_dh) for s in even], axis=1)
    new_mla_ckv = jnp.stack([s[0].reshape(batch, seq, kv_lora) for s in odd], axis=1)
    new_mla_kpe = jnp.stack([s[1].reshape(batch, seq, rope) for s in odd], axis=1)
    return (y_prompt, y_sample, new_na_k, new_na_v, new_mla_ckv, new_mla_kpe)
```

```python
import functools
import math

import numpy as np
import jax
import jax.numpy as jnp
from jax import lax
from jax.experimental import pallas as pl
from jax.experimental.pallas import tpu as pltpu

GRID_W = 64
LN_EPS = 1e-5
RMS_EPS = 1e-6
POOL_WINDOWS = (2, 4, 8, 16)
ROPE_BASE = 10000.0
MASKED = -1e30
LOG2E = math.log2(math.e)
FFN_CHUNK = 512
ATTN_ROWS = 512
NA_GROUP_ROWS = 4
NA_WINDOW_ROWS = 12
NA_KEY_CHUNK = 256
V7X_VMEM_LIMIT_BYTES = 56 * 1024 * 1024

F32 = jnp.float32
BF16 = jnp.bfloat16
_NT = (((1,), (1,)), ((), ()))


def _params(*semantics):
    return pltpu.CompilerParams(dimension_semantics=semantics,
                                vmem_limit_bytes=V7X_VMEM_LIMIT_BYTES)


def _dot(a, b):
    return jnp.dot(a, b, preferred_element_type=F32)


def _post_norm(x, update, g, b, alpha):
    z = alpha * x + update
    mu = jnp.mean(z, axis=-1, keepdims=True)
    zc = z - mu
    var = jnp.mean(zc * zc, axis=-1, keepdims=True)
    return zc * lax.rsqrt(var + LN_EPS) * g + b


def _modulated(x, m, j):
    return x * (1.0 + m[3 * j + 1:3 * j + 2]) + m[3 * j:3 * j + 1]


def _mod_kernel(c_ref, w_ref, b_ref, o_ref):
    c = c_ref[...]
    h = (c * jax.nn.sigmoid(c)).astype(BF16)
    o_ref[...] = _dot(h, w_ref[...].astype(BF16)) + b_ref[...]


def _mod_call(cond, w_mod, b_mod, tn=1024):
    depth, d, n = w_mod.shape
    r = cond.shape[0]
    return pl.pallas_call(
        _mod_kernel,
        out_shape=jax.ShapeDtypeStruct((depth, r, n), F32),
        grid=(depth, n // tn),
        in_specs=[pl.BlockSpec((r, d), lambda l, j: (0, 0)),
                  pl.BlockSpec((None, d, tn), lambda l, j: (l, 0, j)),
                  pl.BlockSpec((None, 1, tn), lambda l, j: (l, 0, j))],
        out_specs=pl.BlockSpec((None, r, tn), lambda l, j: (l, 0, j)),
        compiler_params=_params("parallel", "parallel"),
        name="mod_proj",
    )(cond, w_mod, b_mod.reshape(depth, 1, n))


def _ffn_kernel(x_ref, mod_ref, w13_ref, w2_ref, g_ref, b_ref, o_ref, h_ref, *, j, alpha, rows, mid_rows):
    f = pl.program_id(1)
    last = pl.num_programs(1) - 1
    tm = x_ref.shape[0]
    tf = w2_ref.shape[0]

    def chunk_update(h):
        ab = _dot(h, w13_ref[...])
        a, b = ab[:, :tf], ab[:, tf:]
        act = (a * jax.nn.sigmoid(a) * b).astype(BF16)
        return _dot(act, w2_ref[...])

    @pl.when(f == 0)
    def _():
        m = mod_ref[...]
        for r0 in range(0, tm, rows):
            h = _modulated(x_ref[r0:r0 + rows, :], m, j).astype(BF16)
            h_ref[r0:r0 + rows, :] = h
            o_ref[r0:r0 + rows, :] = chunk_update(h)

    @pl.when((f > 0) & (f < last))
    def _():
        for r0 in range(0, tm, mid_rows):
            o_ref[r0:r0 + mid_rows, :] += chunk_update(h_ref[r0:r0 + mid_rows, :])

    @pl.when(f == last)
    def _():
        half_gate = 0.5 * mod_ref[...][3 * j + 2:3 * j + 3]
        for r0 in range(0, tm, rows):
            y = o_ref[r0:r0 + rows, :] + chunk_update(h_ref[r0:r0 + rows, :])
            o_ref[r0:r0 + rows, :] = _post_norm(x_ref[r0:r0 + rows, :], half_gate * y,
                                                g_ref[...], b_ref[...], alpha)


def _ffn_pack_kernel(w1_ref, w3_ref, o_ref):
    tf = w1_ref.shape[-1]
    o_ref[:, :tf] = w1_ref[...].astype(BF16)
    o_ref[:, tf:] = w3_ref[...].astype(BF16)


def _ffn_chunked(w1, w3, tf):
    depth, n_sub, d, ff = w1.shape
    tf = min(tf, ff)
    assert ff % tf == 0 and ff // tf >= 2
    spec = pl.BlockSpec((None, None, d, tf), lambda l, s, f: (l, s, 0, f))
    return pl.pallas_call(
        _ffn_pack_kernel,
        out_shape=jax.ShapeDtypeStruct((depth, n_sub, ff // tf, d, 2 * tf), BF16),
        grid=(depth, n_sub, ff // tf),
        in_specs=[spec, spec],
        out_specs=pl.BlockSpec((None, None, None, d, 2 * tf), lambda l, s, f: (l, s, f, 0, 0)),
        compiler_params=_params("parallel", "parallel", "parallel"),
        name="ffn_pack",
    )(w1, w3)


def _ffn_call(x, mod, tokens_per_cond, w13, w2, l, s, ln_g, ln_b, j, alpha, tm=1024, rows=256,
              mid_rows=512):
    t, d = x.shape
    n_chunks = w13.shape[2]
    tf = w13.shape[-1] // 2
    tm = min(tm, tokens_per_cond)
    assert tokens_per_cond % tm == 0
    per = tokens_per_cond // tm
    return pl.pallas_call(
        functools.partial(_ffn_kernel, j=j, alpha=alpha, rows=min(rows, tm), mid_rows=min(mid_rows, tm)),
        out_shape=jax.ShapeDtypeStruct((t, d), F32),
        grid=(t // tm, n_chunks),
        in_specs=[pl.BlockSpec((tm, d), lambda i, f: (i, 0)),
                  pl.BlockSpec((None, 9, d), lambda i, f: (i // per, 0, 0)),
                  pl.BlockSpec((None, None, None, d, 2 * tf), lambda i, f: (l, s, f, 0, 0)),
                  pl.BlockSpec((None, None, tf, d), lambda i, f: (l, s, f, 0)),
                  pl.BlockSpec((1, d), lambda i, f: (0, 0)),
                  pl.BlockSpec((1, d), lambda i, f: (0, 0))],
        out_specs=pl.BlockSpec((tm, d), lambda i, f: (i, 0)),
        scratch_shapes=[pltpu.VMEM((tm, d), BF16)],
        compiler_params=_params("parallel", "arbitrary"),
        name="ffn",
    )(x, mod, w13, w2, ln_g, ln_b)


def _na_proj_kernel(x_ref, mod_ref, w_ref, qkv_ref, u_ref, *kv_refs, q_scale, rows):
    m = mod_ref[...]
    n_heads, dh = qkv_ref.shape[0] // 3, qkv_ref.shape[-1]
    n_pool, ch = u_ref.shape[0], u_ref.shape[-1]
    width = n_heads * dh
    for r0 in range(0, x_ref.shape[0], rows):
        r = slice(r0, r0 + rows)
        h = _modulated(x_ref[r, :], m, 1).astype(BF16)
        for n in range(3):
            y = _dot(h, w_ref[:, n * width:(n + 1) * width])
            if kv_refs and n > 0:
                kv_refs[n - 1][r, :] = y
            y = (y * q_scale if n == 0 else y).astype(BF16)
            for hd in range(n_heads):
                qkv_ref[n * n_heads + hd, r, :] = y[:, hd * dh:(hd + 1) * dh]
        y = _dot(h, w_ref[:, 3 * width:]).astype(BF16)
        for g in range(n_pool):
            u_ref[g, r, :] = y[:, g * ch:(g + 1) * ch]


def _resident(block_shape, index):
    return pl.BlockSpec(block_shape, lambda *_: index, pipeline_mode=pl.Buffered(1))


def _na_proj_call(x, mod, tokens_per_cond, w_in, i_even, n_heads, dh, n_pool, ch, q_scale, with_kv,
                  tm=512, rows=256):
    t, d = x.shape
    n_in = w_in.shape[-1]
    na_width = n_heads * dh
    assert n_in == 3 * na_width + n_pool * ch
    tm = min(tm, tokens_per_cond)
    per = tokens_per_cond // tm
    out_shape = [jax.ShapeDtypeStruct((3 * n_heads, t, dh), BF16),
                 jax.ShapeDtypeStruct((n_pool, t, ch), BF16)]
    out_specs = [pl.BlockSpec((3 * n_heads, tm, dh), lambda i: (0, i, 0)),
                 pl.BlockSpec((n_pool, tm, ch), lambda i: (0, i, 0))]
    if with_kv:
        out_shape += [jax.ShapeDtypeStruct((t, na_width), F32)] * 2
        out_specs += [pl.BlockSpec((tm, na_width), lambda i: (i, 0))] * 2
    return pl.pallas_call(
        functools.partial(_na_proj_kernel, q_scale=q_scale, rows=min(rows, tm)),
        out_shape=out_shape,
        grid=(t // tm,),
        in_specs=[pl.BlockSpec((tm, d), lambda i: (i, 0)),
                  pl.BlockSpec((None, 9, d), lambda i: (i // per, 0, 0)),
                  _resident((None, d, n_in), (i_even, 0, 0))],
        out_specs=out_specs,
        compiler_params=_params("parallel"),
        name="na_proj",
    )(x, mod, w_in)


def _attn_rows(q, kv_refs, chunk, dv):
    rows = q.shape[0]
    den_in_v = kv_refs[1].shape[-1] == 2 * dv
    m = jnp.full((rows, 1), MASKED, F32)
    acc = jnp.zeros((rows, kv_refs[1].shape[-1]), F32)
    den = jnp.zeros((rows, 1), F32)
    for k_ref, v_ref in zip(kv_refs[0::2], kv_refs[1::2]):
        lk = k_ref.shape[0]
        step = min(chunk, lk)
        for c0 in range(0, lk, step):
            s = lax.dot_general(q, k_ref[c0:c0 + step, :], _NT, preferred_element_type=F32)
            m_new = jnp.maximum(m, jnp.max(s, axis=-1, keepdims=True))
            rescale = jnp.exp2(m - m_new)
            p = jnp.exp2(s - m_new)
            if not den_in_v:
                den = rescale * den + jnp.sum(p, axis=-1, keepdims=True)
            acc = rescale * acc + _dot(p.astype(BF16), v_ref[c0:c0 + step, :])
            m = m_new
    return acc[:, :dv] / (acc[:, dv:] if den_in_v else den)


def _attn_kernel(*refs, chunk, dv, rows):
    q_ref, o_ref = refs[0], refs[-1]
    for hd in range(q_ref.shape[0]):
        kv_refs = [ref.at[hd] for ref in refs[1:-1]]
        for r0 in range(0, q_ref.shape[1], rows):
            o = _attn_rows(q_ref[hd, r0:r0 + rows, :], kv_refs, chunk, dv)
            o_ref[r0:r0 + rows, hd * dv:(hd + 1) * dv] = o.astype(o_ref.dtype)


def _attn_call(q, q_head0, segments, n_heads, dv, tq, chunk, name, heads=1):
    _, b, l, dq = q.shape
    tq = min(tq, l)
    heads = math.gcd(heads, n_heads, q_head0, *[h0 for _, k0, _, v0 in segments for h0 in (k0, v0)])
    in_specs = [pl.BlockSpec((heads, None, tq, dq),
                             lambda bi, h, qi: (q_head0 // heads + h, bi, qi, 0))]
    args = [q]
    for k, k0, v, v0 in segments:
        lk = k.shape[2]
        assert lk % min(chunk, lk) == 0
        in_specs.append(pl.BlockSpec((heads, None, lk, dq),
                                     lambda bi, h, qi, k0=k0: (k0 // heads + h, bi, 0, 0)))
        in_specs.append(pl.BlockSpec((heads, None, lk, v.shape[-1]),
                                     lambda bi, h, qi, v0=v0: (v0 // heads + h, bi, 0, 0)))
        args += [k, v]
    return pl.pallas_call(
        functools.partial(_attn_kernel, chunk=chunk, dv=dv, rows=min(ATTN_ROWS, tq)),
        out_shape=jax.ShapeDtypeStruct((b, l, n_heads * dv), BF16),
        grid=(b, n_heads // heads, l // tq),
        in_specs=in_specs,
        out_specs=pl.BlockSpec((None, tq, heads * dv), lambda bi, h, qi: (bi, qi, h)),
        compiler_params=_params("parallel", "parallel", "arbitrary"),
        name=name,
    )(*args)


def _na_kernel(q_ref, k_ref, v_ref, kc_ref, vc_ref, bias_ref, o_ref, *, n_rows, half_rows, groups):
    n_groups = n_rows // NA_GROUP_ROWS
    tq = NA_GROUP_ROWS * GRID_W
    win = NA_WINDOW_ROWS * GRID_W
    for c in range(groups):
        g = pl.program_id(2) * groups + c
        base_row = jnp.clip(g * NA_GROUP_ROWS - half_rows, 0, n_rows - NA_WINDOW_ROWS)
        base = pl.multiple_of(base_row * GRID_W, GRID_W)
        variant = jnp.where(g == 0, 0, jnp.where(g == n_groups - 1, 2, 1))
        q = q_ref[c * tq:(c + 1) * tq, :]
        m = jnp.full((tq, 1), MASKED, F32)
        acc = jnp.zeros((tq, o_ref.shape[-1]), F32)
        den = jnp.zeros((tq, 1), F32)
        chunks = [(k_ref, v_ref, base + c0, c0) for c0 in range(0, win, NA_KEY_CHUNK)]
        chunks += [(kc_ref, vc_ref, c0, None) for c0 in range(0, kc_ref.shape[0], NA_KEY_CHUNK)]
        for kr, vr, k0, b0 in chunks:
            s = lax.dot_general(q, kr[pl.ds(k0, NA_KEY_CHUNK), :], _NT, preferred_element_type=F32)
            if b0 is not None:
                s = s + bias_ref[variant, :, b0:b0 + NA_KEY_CHUNK]
            m_new = jnp.maximum(m, jnp.max(s, axis=-1, keepdims=True))
            rescale = jnp.exp2(m - m_new)
            p = jnp.exp2(s - m_new)
            den = rescale * den + jnp.sum(p, axis=-1, keepdims=True)
            acc = rescale * acc + _dot(p.astype(BF16), vr[pl.ds(k0, NA_KEY_CHUNK), :])
            m = m_new
        o_ref[c * tq:(c + 1) * tq, :] = (acc / den).astype(o_ref.dtype)


def _na_group_rows(g, n_rows, na_rows):
    kr = min(na_rows, n_rows)
    base = int(np.clip(g * NA_GROUP_ROWS - kr // 2, 0, n_rows - NA_WINDOW_ROWS))
    rq = (g * NA_GROUP_ROWS + np.arange(NA_GROUP_ROWS))[:, None]
    rk = (base + np.arange(NA_WINDOW_ROWS))[None, :]
    rs = np.clip(rq - kr // 2, 0, n_rows - kr)
    ok = (rk >= rs) & (rk < rs + kr)
    assert (ok.sum(axis=1) == kr).all(), "key window misses part of a neighbourhood"
    return np.where(ok, rk - rq + na_rows - 1, 0), ok


def _na_bias_table(rpb, n_rows):
    n_heads, nr2, nc2 = rpb.shape
    na_rows, na_cols = (nr2 + 1) // 2, (nc2 + 1) // 2
    n_groups = n_rows // NA_GROUP_ROWS
    per_group = [_na_group_rows(g, n_rows, na_rows) for g in range(n_groups)]
    for g in range(2, n_groups - 1):
        assert all((a == b).all() for a, b in zip(per_group[g], per_group[1])), "interior groups differ"
    idx_r, row_ok = (np.stack([per_group[g][i] for g in (0, 1, n_groups - 1)]) for i in range(2))
    w = GRID_W
    e = jnp.pad(rpb, ((0, 0), (0, 0), (w - na_cols, 2 * w - (w - na_cols) - nc2)))
    band = jnp.tile(e, (1, 1, w))[..., :w * (2 * w - 1)].reshape(n_heads, nr2, w, 2 * w - 1)[..., w - 1:]
    cq, ck = np.arange(w)[:, None], np.arange(w)[None, :]
    cs = np.clip(cq - na_cols // 2, 0, w - na_cols)
    col_ok = (ck >= cs) & (ck < cs + na_cols)
    blocks = jnp.take(band, jnp.asarray(idx_r.reshape(-1)), axis=1)
    blocks = blocks.reshape((n_heads,) + idx_r.shape + (w, w))
    ok = row_ok[:, :, :, None, None] & col_ok[None, None, None]
    table = jnp.where(ok[None], blocks * LOG2E, MASKED).transpose(0, 1, 2, 4, 3, 5)
    return table.reshape(n_heads, 3, NA_GROUP_ROWS * w, NA_WINDOW_ROWS * w).astype(F32)


def _na_call(qkv, kc, vc, bias, na_rows, groups=8):
    n_heads, b, l, dh = qkv.shape[0] // 3, *qkv.shape[1:]
    n_rows = l // GRID_W
    assert n_rows % NA_GROUP_ROWS == 0 and n_rows >= NA_WINDOW_ROWS + NA_GROUP_ROWS
    assert NA_GROUP_ROWS - 1 + min(na_rows, n_rows) <= NA_WINDOW_ROWS
    assert (NA_WINDOW_ROWS * GRID_W) % NA_KEY_CHUNK == 0 and kc.shape[1] % NA_KEY_CHUNK == 0
    n_groups = n_rows // NA_GROUP_ROWS
    groups = math.gcd(n_groups, groups)
    tq = groups * NA_GROUP_ROWS * GRID_W
    lc = kc.shape[1]
    return pl.pallas_call(
        functools.partial(_na_kernel, n_rows=n_rows, half_rows=min(na_rows, n_rows) // 2,
                          groups=groups),
        out_shape=jax.ShapeDtypeStruct((b, l, n_heads * dh), BF16),
        grid=(b, n_heads, n_groups // groups),
        in_specs=[pl.BlockSpec((None, None, tq, dh), lambda bi, h, g: (h, bi, g, 0)),
                  pl.BlockSpec((None, None, l, dh), lambda bi, h, g: (n_heads + h, bi, 0, 0)),
                  pl.BlockSpec((None, None, l, dh), lambda bi, h, g: (2 * n_heads + h, bi, 0, 0)),
                  pl.BlockSpec((None, lc, dh), lambda bi, h, g: (bi, 0, h)),
                  pl.BlockSpec((None, lc, dh), lambda bi, h, g: (bi, 0, h)),
                  pl.BlockSpec((None,) + bias.shape[1:], lambda bi, h, g: (h, 0, 0, 0))],
        out_specs=pl.BlockSpec((None, tq, dh), lambda bi, h, g: (bi, g, h)),
        compiler_params=_params("parallel", "parallel", "arbitrary"),
        name="na_attn",
    )(qkv, qkv, qkv, kc, vc, bias)


def _pool_kernel(u_ref, w_ref, s_ref, o_ref, *, chunk, window):
    l = u_ref.shape[0]
    half = lax.shift_left(jnp.int32(1), pl.program_id(1).astype(jnp.int32))
    w = w_ref[...]
    scale = s_ref[...]

    def body(c, carry):
        t0 = pl.multiple_of(c * chunk, chunk)
        s0 = pl.multiple_of(jnp.clip(t0 - (window - chunk) // 2, 0, l - window), 16)
        t = t0 + lax.broadcasted_iota(jnp.int32, (chunk, 1), 0)
        lo = jnp.maximum(t - half, 0)
        hi = jnp.minimum(t + half, l)
        pos = s0 + lax.broadcasted_iota(jnp.int32, (1, window), 1)
        band = jnp.where((pos >= lo) & (pos < hi), 1.0, 0.0).astype(BF16)
        sums = _dot(band, u_ref[pl.ds(s0, window), :])
        mean = sums / (hi - lo).astype(F32)
        d = (mean - u_ref[pl.ds(t0, chunk), :].astype(F32)).astype(BF16)
        o_ref[pl.ds(t0, chunk), :] = (_dot(d, w) * scale).astype(o_ref.dtype)
        return carry

    lax.fori_loop(0, l // chunk, body, 0, unroll=min(4, l // chunk))


def _pool_call(u, w_pool, pool_scale, i_even):
    _, b, l, _ = u.shape
    n_groups, ch, _ = w_pool.shape[1:]
    assert n_groups == len(POOL_WINDOWS) and all(w == 2 << g for g, w in enumerate(POOL_WINDOWS))
    chunk = min(256, l)
    window = min(2 * chunk, l)
    assert l % chunk == 0 and (window == l or window - chunk >= max(POOL_WINDOWS))
    return pl.pallas_call(
        functools.partial(_pool_kernel, chunk=chunk, window=window),
        out_shape=jax.ShapeDtypeStruct((b, l, n_groups * ch), BF16),
        grid=(b, n_groups),
        in_specs=[pl.BlockSpec((None, None, l, ch), lambda bi, g: (g, bi, 0, 0)),
                  pl.BlockSpec((None, None, ch, ch), lambda bi, g: (i_even, g, 0, 0)),
                  pl.BlockSpec((None, 1, ch), lambda bi, g: (i_even, 0, g))],
        out_specs=pl.BlockSpec((None, l, ch), lambda bi, g: (bi, 0, g)),
        compiler_params=_params("parallel", "parallel"),
        name="pool",
    )(u, w_pool, pool_scale)


def _out_kernel(*refs, n_lhs, alpha, rows):
    x_ref, mod_ref = refs[0], refs[1]
    g_ref, b_ref, o_ref = refs[-3], refs[-2], refs[-1]
    gate = mod_ref[...][5:6]
    for r0 in range(0, x_ref.shape[0], rows):
        r = slice(r0, r0 + rows)
        y = functools.reduce(jnp.add, [_dot(refs[2 + 2 * i][r, :], refs[3 + 2 * i][...])
                                       for i in range(n_lhs)])
        o_ref[r, :] = _post_norm(x_ref[r, :], gate * y, g_ref[...], b_ref[...], alpha)


def _out_call(x, mod, tokens_per_cond, lhs_w, ln_g, ln_b, alpha, tm=1024, rows=256):
    t, d = x.shape
    tm = min(tm, tokens_per_cond)
    per = tokens_per_cond // tm
    in_specs = [pl.BlockSpec((tm, d), lambda i: (i, 0)),
                pl.BlockSpec((None, 9, d), lambda i: (i // per, 0, 0))]
    args = [x, mod]
    for lhs, w, w_block, w_index in lhs_w:
        in_specs.append(pl.BlockSpec((tm, lhs.shape[1]), lambda i: (i, 0)))
        in_specs.append(_resident(w_block, w_index))
        args += [lhs, w]
    in_specs += [pl.BlockSpec((1, d), lambda i: (0, 0))] * 2
    args += [ln_g, ln_b]
    return pl.pallas_call(
        functools.partial(_out_kernel, n_lhs=len(lhs_w), alpha=alpha, rows=min(rows, tm)),
        out_shape=jax.ShapeDtypeStruct((t, d), F32),
        grid=(t // tm,),
        in_specs=in_specs,
        out_specs=pl.BlockSpec((tm, d), lambda i: (i, 0)),
        compiler_params=_params("parallel"),
        name="out_proj",
    )(*args)


def _rms(x, g):
    return x * lax.rsqrt(jnp.mean(x * x, axis=-1, keepdims=True) + RMS_EPS) * g


def _rotary_pair(t, tab):
    u = t * tab
    return u + pltpu.roll(u, u.shape[-1] // 2, 1)


def _rope_half_mask(shape):
    return lax.broadcasted_iota(jnp.int32, shape, 1) < shape[-1] // 2


def _head_store(ref, h, r, lane0, value):
    ref[h, r, lane0:lane0 + value.shape[-1]] = value.astype(ref.dtype)


def _store_values_with_ones(v_ref, r, v, dv):
    ones = jnp.ones((v.shape[0], dv), v_ref.dtype)
    for hd in range(v_ref.shape[0]):
        _head_store(v_ref, hd, r, 0, v[:, hd * dv:(hd + 1) * dv])
        _head_store(v_ref, hd, r, dv, ones)


def _mla_proj_kernel(x_ref, mod_ref, wd_ref, qn_ref, kvn_ref, wq_ref, wk_ref, wv_ref, tab_ref,
                     q_ref, k_ref, v_ref, *state_refs, n_heads, q_lora, kv_lora, nope, q_scale, rows):
    m = mod_ref[...]
    dq = 2 * nope
    for r0 in range(0, x_ref.shape[0], rows):
        r = slice(r0, r0 + rows)
        h = _modulated(x_ref[r, :], m, 1).astype(BF16)
        down = _dot(h, wd_ref[...])
        cq = _rms(down[:, :q_lora], qn_ref[...])
        ckv = _rms(down[:, q_lora:q_lora + kv_lora], kvn_ref[...])
        pe = down[:, q_lora + kv_lora:]
        tab = tab_ref[r, :]
        if state_refs:
            state_refs[0][r, :] = ckv
            state_refs[1][r, :] = pe[:, :pe.shape[-1] // 2]
        kpe = jnp.where(_rope_half_mask(pe.shape), _rotary_pair(pe, tab), 0.0).astype(BF16)
        q = _dot(cq.astype(BF16), wq_ref[...]) * q_scale
        ckv_b = ckv.astype(BF16)
        kn = _dot(ckv_b, wk_ref[...])
        _store_values_with_ones(v_ref, r, _dot(ckv_b, wv_ref[...]), nope)
        for hd in range(n_heads):
            _head_store(q_ref, hd, r, 0, q[:, hd * dq:hd * dq + nope])
            _head_store(q_ref, hd, r, nope, _rotary_pair(q[:, hd * dq + nope:(hd + 1) * dq], tab))
            _head_store(k_ref, hd, r, 0, kn[:, hd * nope:(hd + 1) * nope])
            _head_store(k_ref, hd, r, nope, kpe)


def _mla_proj_call(x, mod, tokens_per_cond, wd, qn, kvn, wq, wk, wv, tab, i_odd, dims, with_state,
                   tm=512, rows=256):
    n_heads, q_lora, kv_lora, nope, rope, q_scale = dims
    t, d = x.shape
    tm = min(tm, tokens_per_cond, tab.shape[0])
    per = tokens_per_cond // tm
    tab_blocks = tab.shape[0] // tm
    dq = 2 * nope
    out_shape = [jax.ShapeDtypeStruct((n_heads, t, dq), BF16)] * 3
    out_specs = [pl.BlockSpec((n_heads, tm, dq), lambda i: (0, i, 0))] * 3
    if with_state:
        out_shape += [jax.ShapeDtypeStruct((t, kv_lora), F32), jax.ShapeDtypeStruct((t, rope), F32)]
        out_specs += [pl.BlockSpec((tm, kv_lora), lambda i: (i, 0)),
                      pl.BlockSpec((tm, rope), lambda i: (i, 0))]
    whole = lambda a: _resident((None,) + a.shape[1:], (i_odd,) + (0,) * (a.ndim - 1))
    return pl.pallas_call(
        functools.partial(_mla_proj_kernel, n_heads=n_heads, q_lora=q_lora, kv_lora=kv_lora,
                          nope=nope, q_scale=q_scale, rows=min(rows, tm)),
        out_shape=out_shape,
        grid=(t // tm,),
        in_specs=[pl.BlockSpec((tm, d), lambda i: (i, 0)),
                  pl.BlockSpec((None, 9, d), lambda i: (i // per, 0, 0)),
                  whole(wd), whole(qn), whole(kvn), whole(wq), whole(wk), whole(wv),
                  pl.BlockSpec((tm, tab.shape[1]), lambda i: (i % tab_blocks, 0))],
        out_specs=out_specs,
        compiler_params=_params("parallel"),
        name="mla_proj",
    )(x, mod, wd, qn, kvn, wq, wk, wv, tab)


def _mla_expand_kernel(ckv_ref, kpe_ref, wk_ref, wv_ref, k_ref, v_ref, *, n_heads, nope):
    ckv = ckv_ref[...].astype(BF16)
    kn = _dot(ckv, wk_ref[...])
    r = slice(None)
    _store_values_with_ones(v_ref, r, _dot(ckv, wv_ref[...]), nope)
    kpe = kpe_ref[...].astype(BF16)
    for hd in range(n_heads):
        _head_store(k_ref, hd, r, 0, kn[:, hd * nope:(hd + 1) * nope])
        _head_store(k_ref, hd, r, nope, kpe)


def _mla_expand_call(ckv, kpe_pad, wk, wv, i_odd, n_heads, nope, tm=256):
    t, kv_lora = ckv.shape
    tm = min(tm, t)
    dq = 2 * nope
    whole = lambda a: pl.BlockSpec((None,) + a.shape[1:], lambda i: (i_odd,) + (0,) * (a.ndim - 1))
    return pl.pallas_call(
        functools.partial(_mla_expand_kernel, n_heads=n_heads, nope=nope),
        out_shape=[jax.ShapeDtypeStruct((n_heads, t, dq), BF16)] * 2,
        grid=(t // tm,),
        in_specs=[pl.BlockSpec((tm, kv_lora), lambda i: (i, 0)),
                  pl.BlockSpec((tm, kpe_pad.shape[1]), lambda i: (i, 0)),
                  whole(wk), whole(wv)],
        out_specs=[pl.BlockSpec((n_heads, tm, dq), lambda i: (0, i, 0))] * 2,
        compiler_params=_params("parallel"),
        name="mla_expand",
    )(ckv, kpe_pad, wk, wv)


def _rotate_half_columns(w, rope):
    q = rope // 4
    parts = [w[..., i * q:(i + 1) * q] for i in range(4)]
    return jnp.concatenate([-parts[1], parts[0], -parts[3], parts[2]], axis=-1)


def _rope_table(n_tokens, rope):
    axis = rope // 2
    t = jnp.arange(n_tokens)
    inv = ROPE_BASE ** (-jnp.arange(0, axis, 2, dtype=F32) / axis)
    ang_r = (t // GRID_W).astype(F32)[:, None] * inv[None, :]
    ang_c = (t % GRID_W).astype(F32)[:, None] * inv[None, :]
    ang = jnp.concatenate([ang_r, ang_r, ang_c, ang_c], axis=-1)
    return jnp.concatenate([jnp.cos(ang), jnp.sin(ang)], axis=-1)


def _identity_rope_table(n_tokens, rope):
    return jnp.concatenate([jnp.ones((n_tokens, rope), F32), jnp.zeros((n_tokens, rope), F32)], axis=-1)


def kernel(x_prompt, x_sample, cache_na_k, cache_na_v, cache_mla_ckv, cache_mla_kpe, c, c_ctx,
           w_mod, b_mod, ln_g, ln_b, ffn_w1, ffn_w3, ffn_w2,
           na_w_in, mix0_w_out, na_rpb, pool_w, pool_scale,
           mla_w_down, mla_q_norm, mla_w_uq, mla_kv_norm, mla_w_ukv, mla_w_out):
    depth, d, _ = w_mod.shape
    alpha = (2 * depth) ** 0.25
    batch, seq, _ = x_prompt.shape
    dec_batch, dec_seq, _ = x_sample.shape
    na_heads, na_dh = cache_na_k.shape[-2:]
    na_width = na_heads * na_dh
    pool_groups, pool_ch = pool_w.shape[1:3]
    assert pool_groups * pool_ch == na_width and pool_ch % 128 == 0
    q_lora, kv_lora = mla_q_norm.shape[-1], mla_kv_norm.shape[-1]
    rope = cache_mla_kpe.shape[-1]
    uq_w, ukv_w, out_rows = mla_w_uq.shape[-1], mla_w_ukv.shape[-1], mla_w_out.shape[1]
    mla_heads = (uq_w - ukv_w + out_rows) // rope
    nope = uq_w // mla_heads - rope
    v_dim = out_rows // mla_heads
    assert nope == v_dim and 2 * rope == nope and nope % 128 == 0
    mla_dims = (mla_heads, q_lora, kv_lora, nope, rope, float((nope + rope) ** -0.5) * LOG2E)

    w13, w2 = _ffn_chunked(ffn_w1, ffn_w3, FFN_CHUNK), ffn_w2.astype(BF16)
    w_in = na_w_in.astype(BF16)
    w_out0 = mix0_w_out.astype(BF16)
    w_pool = pool_w.astype(BF16)
    p_scale = pool_scale.reshape(pool_scale.shape[0], 1, -1)
    kpe_cols = mla_w_down[..., q_lora + kv_lora:]
    wd = jnp.concatenate([mla_w_down, _rotate_half_columns(kpe_cols, rope)], axis=-1).astype(BF16)
    n_odd = mla_w_uq.shape[0]
    uq = mla_w_uq.reshape(n_odd, q_lora, mla_heads, nope + rope)
    wq = jnp.concatenate([uq, _rotate_half_columns(uq[..., nope:], rope)], axis=-1)
    wq = wq.reshape(n_odd, q_lora, mla_heads * 2 * nope).astype(BF16)
    ukv = mla_w_ukv.reshape(n_odd, kv_lora, mla_heads, nope + v_dim)
    wk = ukv[..., :nope].reshape(n_odd, kv_lora, mla_heads * nope).astype(BF16)
    wv = ukv[..., nope:].reshape(n_odd, kv_lora, mla_heads * v_dim).astype(BF16)
    w_out1 = mla_w_out.astype(BF16)
    qn = mla_q_norm.reshape(n_odd, 1, q_lora)
    kvn = mla_kv_norm.reshape(n_odd, 1, kv_lora)

    n_cond = 1 + dec_batch
    cond = jnp.concatenate([c_ctx[None, :], c, jnp.zeros((-n_cond % 16, d), F32)], axis=0)
    mod = _mod_call(cond, w_mod, b_mod).reshape(depth, cond.shape[0], 9, d)

    rope_tab = _rope_table(dec_seq, rope)
    ident_tab = _identity_rope_table(min(256, seq), rope)

    def trunk(x3, mod_rows, per_cond, caches):
        bsz, l, _ = x3.shape
        x = x3.reshape(bsz * l, d)
        states = []
        for layer in range(depth):
            m = mod_rows[layer]
            g = lambda s: ln_g[layer, s].reshape(1, d)
            bb = lambda s: ln_b[layer, s].reshape(1, d)
            x = _ffn_call(x, m, per_cond, w13, w2, layer, 0, g(0), bb(0), 0, alpha)
            i = layer // 2
            if layer % 2 == 0:
                outs = _na_proj_call(x, m, per_cond, w_in, i, na_heads, na_dh, pool_groups, pool_ch,
                                     float(na_dh ** -0.5) * LOG2E, caches is None)
                qkv = outs[0].reshape(3 * na_heads, bsz, l, na_dh)
                u = outs[1].reshape(pool_groups, bsz, l, pool_ch)
                if caches is None:
                    states.append((outs[2], outs[3]))
                    a = _attn_call(qkv, 0, [(qkv, na_heads, qkv, 2 * na_heads)], na_heads, na_dh,
                                   256, 512, "na_dense_attn", heads=4)
                else:
                    kc = caches[0][:, i].reshape(bsz, -1, na_width).astype(BF16)
                    vc = caches[1][:, i].reshape(bsz, -1, na_width).astype(BF16)
                    bias = _na_bias_table(na_rpb[i], l // GRID_W)
                    a = _na_call(qkv, kc, vc, bias, (na_rpb.shape[2] + 1) // 2)
                pooled = _pool_call(u, w_pool, p_scale, i)
                x = _out_call(x, m, per_cond,
                              [(a.reshape(bsz * l, na_width), w_out0, (None, na_width, d), (i, 0, 0)),
                               (pooled.reshape(bsz * l, na_width), w_out0, (None, na_width, d), (i, 1, 0))],
                              g(1), bb(1), alpha)
            else:
                tab = ident_tab if caches is None else rope_tab
                outs = _mla_proj_call(x, m, per_cond, wd, qn, kvn, wq, wk, wv, tab, i, mla_dims,
                                      caches is None)
                q, k, v = (o.reshape(mla_heads, bsz, l, 2 * nope) for o in outs[:3])
                segments = [(k, 0, v, 0)]
                if caches is None:
                    states.append((outs[3], outs[4]))
                else:
                    ckv_c = caches[2][:, i].reshape(-1, kv_lora)
                    kpe_c = caches[3][:, i].reshape(-1, rope)
                    kpe_c = jnp.concatenate([kpe_c, jnp.zeros_like(kpe_c)], axis=-1)
                    k_c, v_c = _mla_expand_call(ckv_c, kpe_c, wk, wv, i, mla_heads, nope)
                    segments.append((k_c.reshape(mla_heads, bsz, -1, 2 * nope), 0,
                                     v_c.reshape(mla_heads, bsz, -1, 2 * nope), 0))
                o = _attn_call(q, 0, segments, mla_heads, v_dim, 1024, 256, "mla_attn",
                               heads=1 if l > 1024 else 4)
                x = _out_call(x, m, per_cond,
                              [(o.reshape(bsz * l, -1), w_out1, (None,) + w_out1.shape[1:], (i, 0, 0))],
                              g(1), bb(1), alpha)
            x = _ffn_call(x, m, per_cond, w13, w2, layer, 1, g(2), bb(2), 2, alpha)
        return x.reshape(bsz, l, d), states

    y_prompt, st = trunk(x_prompt, mod[:, 0:1], batch * seq, None)
    y_sample, _ = trunk(x_sample, mod[:, 1:n_cond], dec_seq,
                        (cache_na_k, cache_na_v, cache_mla_ckv, cache_mla_kpe))
    even = [s for layer, s in enumerate(st) if layer % 2 == 0]
    odd = [s for layer, s in enumerate(st) if layer % 2 == 1]
    new_na_k = jnp.stack([s[0].reshape(batch, seq, na_heads, na_dh) for s in even], axis=1)
    new_na_v = jnp.stack([s[1].reshape(batch, seq, na_heads, na_dh) for s in even], axis=1)
    new_mla_ckv = jnp.stack([s[0].reshape(batch, seq, kv_lora) for s in odd], axis=1)
    new_mla_kpe = jnp.stack([s[1].reshape(batch, seq, rope) for s in odd], axis=1)
    return (y_prompt, y_sample, new_na_k, new_na_v, new_mla_ckv, new_mla_kpe)
```

```python
import functools
import math

import numpy as np
import jax
import jax.numpy as jnp
from jax import lax
from jax.experimental import pallas as pl
from jax.experimental.pallas import tpu as pltpu

GRID_W = 64
LN_EPS = 1e-5
RMS_EPS = 1e-6
POOL_WINDOWS = (2, 4, 8, 16)
ROPE_BASE = 10000.0
MASKED = -1e30
LOG2E = math.log2(math.e)
FFN_CHUNK = 512
ATTN_ROWS = 512
NA_GROUP_ROWS = 4
NA_WINDOW_ROWS = 12
NA_KEY_CHUNK = 256
V7X_VMEM_LIMIT_BYTES = 56 * 1024 * 1024

F32 = jnp.float32
BF16 = jnp.bfloat16
_NT = (((1,), (1,)), ((), ()))


def _params(*semantics):
    return pltpu.CompilerParams(dimension_semantics=semantics,
                                vmem_limit_bytes=V7X_VMEM_LIMIT_BYTES)


def _dot(a, b):
    return jnp.dot(a, b, preferred_element_type=F32)


def _post_norm(x, y, row_scale, g, b, alpha):
    z = x + (row_scale * (1.0 / alpha)) * y
    mu = jnp.mean(z, axis=-1, keepdims=True)
    zc = z - mu
    var = jnp.mean(zc * zc, axis=-1, keepdims=True)
    return zc * lax.rsqrt(var + LN_EPS / (alpha * alpha)) * g + b


def _modulated(x, m, j):
    return x * (1.0 + m[3 * j + 1:3 * j + 2]) + m[3 * j:3 * j + 1]


def _mod_kernel(c_ref, w_ref, b_ref, o_ref):
    c = c_ref[...]
    h = (c * jax.nn.sigmoid(c)).astype(BF16)
    o_ref[...] = _dot(h, w_ref[...].astype(BF16)) + b_ref[...]


def _mod_call(cond, w_mod, b_mod, tn=1024):
    depth, d, n = w_mod.shape
    r = cond.shape[0]
    return pl.pallas_call(
        _mod_kernel,
        out_shape=jax.ShapeDtypeStruct((depth, r, n), F32),
        grid=(depth, n // tn),
        in_specs=[pl.BlockSpec((r, d), lambda l, j: (0, 0)),
                  pl.BlockSpec((None, d, tn), lambda l, j: (l, 0, j)),
                  pl.BlockSpec((None, 1, tn), lambda l, j: (l, 0, j))],
        out_specs=pl.BlockSpec((None, r, tn), lambda l, j: (l, 0, j)),
        compiler_params=_params("parallel", "parallel"),
        name="mod_proj",
    )(cond, w_mod, b_mod.reshape(depth, 1, n))


def _ffn_kernel(x_ref, mod_ref, w13_ref, w2_ref, g_ref, b_ref, o_ref, h_ref, *, j, alpha, rows, mid_rows):
    f = pl.program_id(1)
    last = pl.num_programs(1) - 1
    tm = x_ref.shape[0]
    tf = w2_ref.shape[0]

    def chunk_update(h):
        ab = _dot(h, w13_ref[...])
        a, b = ab[:, :tf], ab[:, tf:]
        act = (a * jax.nn.sigmoid(a) * b).astype(BF16)
        return _dot(act, w2_ref[...])

    @pl.when(f == 0)
    def _():
        m = mod_ref[...]
        for r0 in range(0, tm, rows):
            h = _modulated(x_ref[r0:r0 + rows, :], m, j).astype(BF16)
            h_ref[r0:r0 + rows, :] = h
            o_ref[r0:r0 + rows, :] = chunk_update(h)

    @pl.when((f > 0) & (f < last))
    def _():
        for r0 in range(0, tm, mid_rows):
            o_ref[r0:r0 + mid_rows, :] += chunk_update(h_ref[r0:r0 + mid_rows, :])

    @pl.when(f == last)
    def _():
        half_gate = 0.5 * mod_ref[...][3 * j + 2:3 * j + 3]
        for r0 in range(0, tm, rows):
            y = o_ref[r0:r0 + rows, :] + chunk_update(h_ref[r0:r0 + rows, :])
            o_ref[r0:r0 + rows, :] = _post_norm(x_ref[r0:r0 + rows, :], y, half_gate,
                                                g_ref[...], b_ref[...], alpha)


def _ffn_pack_kernel(w1_ref, w3_ref, o_ref):
    tf = w1_ref.shape[-1]
    o_ref[:, :tf] = w1_ref[...].astype(BF16)
    o_ref[:, tf:] = w3_ref[...].astype(BF16)


def _ffn_chunked(w1, w3, tf):
    depth, n_sub, d, ff = w1.shape
    tf = min(tf, ff)
    assert ff % tf == 0 and ff // tf >= 2
    spec = pl.BlockSpec((None, None, d, tf), lambda l, s, f: (l, s, 0, f))
    return pl.pallas_call(
        _ffn_pack_kernel,
        out_shape=jax.ShapeDtypeStruct((depth, n_sub, ff // tf, d, 2 * tf), BF16),
        grid=(depth, n_sub, ff // tf),
        in_specs=[spec, spec],
        out_specs=pl.BlockSpec((None, None, None, d, 2 * tf), lambda l, s, f: (l, s, f, 0, 0)),
        compiler_params=_params("parallel", "parallel", "parallel"),
        name="ffn_pack",
    )(w1, w3)


def _ffn_call(x, mod, tokens_per_cond, w13, w2, l, s, ln_g, ln_b, j, alpha, tm=1024, rows=256,
              mid_rows=512):
    t, d = x.shape
    n_chunks = w13.shape[2]
    tf = w13.shape[-1] // 2
    tm = min(tm, tokens_per_cond)
    assert tokens_per_cond % tm == 0
    per = tokens_per_cond // tm
    return pl.pallas_call(
        functools.partial(_ffn_kernel, j=j, alpha=alpha, rows=min(rows, tm), mid_rows=min(mid_rows, tm)),
        out_shape=jax.ShapeDtypeStruct((t, d), F32),
        grid=(t // tm, n_chunks),
        in_specs=[pl.BlockSpec((tm, d), lambda i, f: (i, 0)),
                  pl.BlockSpec((None, 9, d), lambda i, f: (i // per, 0, 0)),
                  pl.BlockSpec((None, None, None, d, 2 * tf), lambda i, f: (l, s, f, 0, 0)),
                  pl.BlockSpec((None, None, tf, d), lambda i, f: (l, s, f, 0)),
                  pl.BlockSpec((1, d), lambda i, f: (0, 0)),
                  pl.BlockSpec((1, d), lambda i, f: (0, 0))],
        out_specs=pl.BlockSpec((tm, d), lambda i, f: (i, 0)),
        scratch_shapes=[pltpu.VMEM((tm, d), BF16)],
        compiler_params=_params("parallel", "arbitrary"),
        name="ffn",
    )(x, mod, w13, w2, ln_g, ln_b)


def _na_proj_kernel(x_ref, mod_ref, w_ref, qkv_ref, u_ref, *kv_refs, q_scale, rows):
    m = mod_ref[...]
    n_heads, dh = qkv_ref.shape[0] // 3, qkv_ref.shape[-1]
    n_pool, ch = u_ref.shape[0], u_ref.shape[-1]
    width = n_heads * dh
    for r0 in range(0, x_ref.shape[0], rows):
        r = slice(r0, r0 + rows)
        h = _modulated(x_ref[r, :], m, 1).astype(BF16)
        for n in range(3):
            y = _dot(h, w_ref[:, n * width:(n + 1) * width])
            if kv_refs and n > 0:
                kv_refs[n - 1][r, :] = y
            y = (y * q_scale if n == 0 else y).astype(BF16)
            for hd in range(n_heads):
                qkv_ref[n * n_heads + hd, r, :] = y[:, hd * dh:(hd + 1) * dh]
        y = _dot(h, w_ref[:, 3 * width:]).astype(BF16)
        for g in range(n_pool):
            u_ref[g, r, :] = y[:, g * ch:(g + 1) * ch]


def _resident(block_shape, index):
    return pl.BlockSpec(block_shape, lambda *_: index, pipeline_mode=pl.Buffered(1))


def _na_proj_call(x, mod, tokens_per_cond, w_in, i_even, n_heads, dh, n_pool, ch, q_scale, with_kv,
                  tm=512, rows=256):
    t, d = x.shape
    n_in = w_in.shape[-1]
    na_width = n_heads * dh
    assert n_in == 3 * na_width + n_pool * ch
    tm = min(tm, tokens_per_cond)
    per = tokens_per_cond // tm
    out_shape = [jax.ShapeDtypeStruct((3 * n_heads, t, dh), BF16),
                 jax.ShapeDtypeStruct((n_pool, t, ch), BF16)]
    out_specs = [pl.BlockSpec((3 * n_heads, tm, dh), lambda i: (0, i, 0)),
                 pl.BlockSpec((n_pool, tm, ch), lambda i: (0, i, 0))]
    if with_kv:
        out_shape += [jax.ShapeDtypeStruct((t, na_width), F32)] * 2
        out_specs += [pl.BlockSpec((tm, na_width), lambda i: (i, 0))] * 2
    return pl.pallas_call(
        functools.partial(_na_proj_kernel, q_scale=q_scale, rows=min(rows, tm)),
        out_shape=out_shape,
        grid=(t // tm,),
        in_specs=[pl.BlockSpec((tm, d), lambda i: (i, 0)),
                  pl.BlockSpec((None, 9, d), lambda i: (i // per, 0, 0)),
                  _resident((None, d, n_in), (i_even, 0, 0))],
        out_specs=out_specs,
        compiler_params=_params("parallel"),
        name="na_proj",
    )(x, mod, w_in)


def _attn_rows(q, kv_refs, chunk, dv):
    rows = q.shape[0]
    den_in_v = kv_refs[1].shape[-1] == 2 * dv
    m = jnp.full((rows, 1), MASKED, F32)
    acc = jnp.zeros((rows, kv_refs[1].shape[-1]), F32)
    den = jnp.zeros((rows, 1), F32)
    for k_ref, v_ref in zip(kv_refs[0::2], kv_refs[1::2]):
        lk = k_ref.shape[0]
        step = min(chunk, lk)
        for c0 in range(0, lk, step):
            s = lax.dot_general(q, k_ref[c0:c0 + step, :], _NT, preferred_element_type=F32)
            m_new = jnp.maximum(m, jnp.max(s, axis=-1, keepdims=True))
            rescale = jnp.exp2(m - m_new)
            p = jnp.exp2(s - m_new)
            if not den_in_v:
                den = rescale * den + jnp.sum(p, axis=-1, keepdims=True)
            acc = rescale * acc + _dot(p.astype(BF16), v_ref[c0:c0 + step, :])
            m = m_new
    return acc[:, :dv] / (acc[:, dv:] if den_in_v else den)


def _attn_kernel(*refs, chunk, dv, rows):
    q_ref, o_ref = refs[0], refs[-1]
    for hd in range(q_ref.shape[0]):
        kv_refs = [ref.at[hd] for ref in refs[1:-1]]
        for r0 in range(0, q_ref.shape[1], rows):
            o = _attn_rows(q_ref[hd, r0:r0 + rows, :], kv_refs, chunk, dv)
            o_ref[r0:r0 + rows, hd * dv:(hd + 1) * dv] = o.astype(o_ref.dtype)


def _attn_call(q, q_head0, segments, n_heads, dv, tq, chunk, name, heads=1):
    _, b, l, dq = q.shape
    tq = min(tq, l)
    heads = math.gcd(heads, n_heads, q_head0, *[h0 for _, k0, _, v0 in segments for h0 in (k0, v0)])
    in_specs = [pl.BlockSpec((heads, None, tq, dq),
                             lambda bi, h, qi: (q_head0 // heads + h, bi, qi, 0))]
    args = [q]
    for k, k0, v, v0 in segments:
        lk = k.shape[2]
        assert lk % min(chunk, lk) == 0
        in_specs.append(pl.BlockSpec((heads, None, lk, dq),
                                     lambda bi, h, qi, k0=k0: (k0 // heads + h, bi, 0, 0)))
        in_specs.append(pl.BlockSpec((heads, None, lk, v.shape[-1]),
                                     lambda bi, h, qi, v0=v0: (v0 // heads + h, bi, 0, 0)))
        args += [k, v]
    return pl.pallas_call(
        functools.partial(_attn_kernel, chunk=chunk, dv=dv, rows=min(ATTN_ROWS, tq)),
        out_shape=jax.ShapeDtypeStruct((b, l, n_heads * dv), BF16),
        grid=(b, n_heads // heads, l // tq),
        in_specs=in_specs,
        out_specs=pl.BlockSpec((None, tq, heads * dv), lambda bi, h, qi: (bi, qi, h)),
        compiler_params=_params("parallel", "parallel", "arbitrary"),
        name=name,
    )(*args)


def _na_kernel(q_ref, k_ref, v_ref, kc_ref, vc_ref, bias_ref, o_ref, *, n_rows, half_rows, groups):
    n_groups = n_rows // NA_GROUP_ROWS
    tq = NA_GROUP_ROWS * GRID_W
    win = NA_WINDOW_ROWS * GRID_W
    for c in range(groups):
        g = pl.program_id(2) * groups + c
        base_row = jnp.clip(g * NA_GROUP_ROWS - half_rows, 0, n_rows - NA_WINDOW_ROWS)
        base = pl.multiple_of(base_row * GRID_W, GRID_W)
        variant = jnp.where(g == 0, 0, jnp.where(g == n_groups - 1, 2, 1))
        q = q_ref[c * tq:(c + 1) * tq, :]
        m = jnp.full((tq, 1), MASKED, F32)
        acc = jnp.zeros((tq, o_ref.shape[-1]), F32)
        den = jnp.zeros((tq, 1), F32)
        chunks = [(k_ref, v_ref, base + c0, c0) for c0 in range(0, win, NA_KEY_CHUNK)]
        chunks += [(kc_ref, vc_ref, c0, None) for c0 in range(0, kc_ref.shape[0], NA_KEY_CHUNK)]
        for kr, vr, k0, b0 in chunks:
            s = lax.dot_general(q, kr[pl.ds(k0, NA_KEY_CHUNK), :], _NT, preferred_element_type=F32)
            if b0 is not None:
                s = s + bias_ref[variant, :, b0:b0 + NA_KEY_CHUNK]
            m_new = jnp.maximum(m, jnp.max(s, axis=-1, keepdims=True))
            rescale = jnp.exp2(m - m_new)
            p = jnp.exp2(s - m_new)
            den = rescale * den + jnp.sum(p, axis=-1, keepdims=True)
            acc = rescale * acc + _dot(p.astype(BF16), vr[pl.ds(k0, NA_KEY_CHUNK), :])
            m = m_new
        o_ref[c * tq:(c + 1) * tq, :] = (acc / den).astype(o_ref.dtype)


def _na_group_rows(g, n_rows, na_rows):
    kr = min(na_rows, n_rows)
    base = int(np.clip(g * NA_GROUP_ROWS - kr // 2, 0, n_rows - NA_WINDOW_ROWS))
    rq = (g * NA_GROUP_ROWS + np.arange(NA_GROUP_ROWS))[:, None]
    rk = (base + np.arange(NA_WINDOW_ROWS))[None, :]
    rs = np.clip(rq - kr // 2, 0, n_rows - kr)
    ok = (rk >= rs) & (rk < rs + kr)
    assert (ok.sum(axis=1) == kr).all(), "key window misses part of a neighbourhood"
    return np.where(ok, rk - rq + na_rows - 1, 0), ok


def _na_bias_table(rpb, n_rows):
    n_heads, nr2, nc2 = rpb.shape
    na_rows, na_cols = (nr2 + 1) // 2, (nc2 + 1) // 2
    n_groups = n_rows // NA_GROUP_ROWS
    per_group = [_na_group_rows(g, n_rows, na_rows) for g in range(n_groups)]
    for g in range(2, n_groups - 1):
        assert all((a == b).all() for a, b in zip(per_group[g], per_group[1])), "interior groups differ"
    idx_r, row_ok = (np.stack([per_group[g][i] for g in (0, 1, n_groups - 1)]) for i in range(2))
    w = GRID_W
    e = jnp.pad(rpb, ((0, 0), (0, 0), (w - na_cols, 2 * w - (w - na_cols) - nc2)))
    band = jnp.tile(e, (1, 1, w))[..., :w * (2 * w - 1)].reshape(n_heads, nr2, w, 2 * w - 1)[..., w - 1:]
    cq, ck = np.arange(w)[:, None], np.arange(w)[None, :]
    cs = np.clip(cq - na_cols // 2, 0, w - na_cols)
    col_ok = (ck >= cs) & (ck < cs + na_cols)
    blocks = jnp.take(band, jnp.asarray(idx_r.reshape(-1)), axis=1)
    blocks = blocks.reshape((n_heads,) + idx_r.shape + (w, w))
    ok = row_ok[:, :, :, None, None] & col_ok[None, None, None]
    table = jnp.where(ok[None], blocks * LOG2E, MASKED).transpose(0, 1, 2, 4, 3, 5)
    return table.reshape(n_heads, 3, NA_GROUP_ROWS * w, NA_WINDOW_ROWS * w).astype(F32)


def _na_call(qkv, kc, vc, bias, na_rows, groups=16):
    n_heads, b, l, dh = qkv.shape[0] // 3, *qkv.shape[1:]
    n_rows = l // GRID_W
    assert n_rows % NA_GROUP_ROWS == 0 and n_rows >= NA_WINDOW_ROWS + NA_GROUP_ROWS
    assert NA_GROUP_ROWS - 1 + min(na_rows, n_rows) <= NA_WINDOW_ROWS
    assert (NA_WINDOW_ROWS * GRID_W) % NA_KEY_CHUNK == 0 and kc.shape[1] % NA_KEY_CHUNK == 0
    n_groups = n_rows // NA_GROUP_ROWS
    groups = math.gcd(n_groups, groups)
    tq = groups * NA_GROUP_ROWS * GRID_W
    lc = kc.shape[1]
    return pl.pallas_call(
        functools.partial(_na_kernel, n_rows=n_rows, half_rows=min(na_rows, n_rows) // 2,
                          groups=groups),
        out_shape=jax.ShapeDtypeStruct((b, l, n_heads * dh), BF16),
        grid=(b, n_heads, n_groups // groups),
        in_specs=[pl.BlockSpec((None, None, tq, dh), lambda bi, h, g: (h, bi, g, 0)),
                  pl.BlockSpec((None, None, l, dh), lambda bi, h, g: (n_heads + h, bi, 0, 0)),
                  pl.BlockSpec((None, None, l, dh), lambda bi, h, g: (2 * n_heads + h, bi, 0, 0)),
                  pl.BlockSpec((None, lc, dh), lambda bi, h, g: (bi, 0, h)),
                  pl.BlockSpec((None, lc, dh), lambda bi, h, g: (bi, 0, h)),
                  pl.BlockSpec((None,) + bias.shape[1:], lambda bi, h, g: (h, 0, 0, 0))],
        out_specs=pl.BlockSpec((None, tq, dh), lambda bi, h, g: (bi, g, h)),
        compiler_params=_params("parallel", "parallel", "arbitrary"),
        name="na_attn",
    )(qkv, qkv, qkv, kc, vc, bias)


def _pool_kernel(u_ref, w_ref, s_ref, o_ref, *, chunk, window):
    l = u_ref.shape[0]
    half = lax.shift_left(jnp.int32(1), pl.program_id(1).astype(jnp.int32))
    w = w_ref[...]
    scale = s_ref[...]

    def body(c, carry):
        t0 = pl.multiple_of(c * chunk, chunk)
        s0 = pl.multiple_of(jnp.clip(t0 - (window - chunk) // 2, 0, l - window), 16)
        t = t0 + lax.broadcasted_iota(jnp.int32, (chunk, 1), 0)
        lo = jnp.maximum(t - half, 0)
        hi = jnp.minimum(t + half, l)
        pos = s0 + lax.broadcasted_iota(jnp.int32, (1, window), 1)
        band = jnp.where((pos >= lo) & (pos < hi), 1.0, 0.0).astype(BF16)
        sums = _dot(band, u_ref[pl.ds(s0, window), :])
        mean = sums / (hi - lo).astype(F32)
        d = (mean - u_ref[pl.ds(t0, chunk), :].astype(F32)).astype(BF16)
        o_ref[pl.ds(t0, chunk), :] = (_dot(d, w) * scale).astype(o_ref.dtype)
        return carry

    lax.fori_loop(0, l // chunk, body, 0, unroll=min(4, l // chunk))


def _pool_call(u, w_pool, pool_scale, i_even):
    _, b, l, _ = u.shape
    n_groups, ch, _ = w_pool.shape[1:]
    assert n_groups == len(POOL_WINDOWS) and all(w == 2 << g for g, w in enumerate(POOL_WINDOWS))
    chunk = min(256, l)
    window = min(2 * chunk, l)
    assert l % chunk == 0 and (window == l or window - chunk >= max(POOL_WINDOWS))
    return pl.pallas_call(
        functools.partial(_pool_kernel, chunk=chunk, window=window),
        out_shape=jax.ShapeDtypeStruct((b, l, n_groups * ch), BF16),
        grid=(b, n_groups),
        in_specs=[pl.BlockSpec((None, None, l, ch), lambda bi, g: (g, bi, 0, 0)),
                  pl.BlockSpec((None, None, ch, ch), lambda bi, g: (i_even, g, 0, 0)),
                  pl.BlockSpec((None, 1, ch), lambda bi, g: (i_even, 0, g))],
        out_specs=pl.BlockSpec((None, l, ch), lambda bi, g: (bi, 0, g)),
        compiler_params=_params("parallel", "parallel"),
        name="pool",
    )(u, w_pool, pool_scale)


def _out_kernel(*refs, n_lhs, alpha, rows):
    x_ref, mod_ref = refs[0], refs[1]
    g_ref, b_ref, o_ref = refs[-3], refs[-2], refs[-1]
    gate = mod_ref[...][5:6]
    for r0 in range(0, x_ref.shape[0], rows):
        r = slice(r0, r0 + rows)
        y = functools.reduce(jnp.add, [_dot(refs[2 + 2 * i][r, :], refs[3 + 2 * i][...])
                                       for i in range(n_lhs)])
        o_ref[r, :] = _post_norm(x_ref[r, :], y, gate, g_ref[...], b_ref[...], alpha)


def _out_call(x, mod, tokens_per_cond, lhs_w, ln_g, ln_b, alpha, tm=1024, rows=256):
    t, d = x.shape
    tm = min(tm, tokens_per_cond)
    per = tokens_per_cond // tm
    in_specs = [pl.BlockSpec((tm, d), lambda i: (i, 0)),
                pl.BlockSpec((None, 9, d), lambda i: (i // per, 0, 0))]
    args = [x, mod]
    for lhs, w, w_block, w_index in lhs_w:
        in_specs.append(pl.BlockSpec((tm, lhs.shape[1]), lambda i: (i, 0)))
        in_specs.append(_resident(w_block, w_index))
        args += [lhs, w]
    in_specs += [pl.BlockSpec((1, d), lambda i: (0, 0))] * 2
    args += [ln_g, ln_b]
    return pl.pallas_call(
        functools.partial(_out_kernel, n_lhs=len(lhs_w), alpha=alpha, rows=min(rows, tm)),
        out_shape=jax.ShapeDtypeStruct((t, d), F32),
        grid=(t // tm,),
        in_specs=in_specs,
        out_specs=pl.BlockSpec((tm, d), lambda i: (i, 0)),
        compiler_params=_params("parallel"),
        name="out_proj",
    )(*args)


def _rms(x, g):
    return x * lax.rsqrt(jnp.mean(x * x, axis=-1, keepdims=True) + RMS_EPS) * g


def _rotary_pair(t, tab):
    u = t * tab
    return u + pltpu.roll(u, u.shape[-1] // 2, 1)


def _rope_half_mask(shape):
    return lax.broadcasted_iota(jnp.int32, shape, 1) < shape[-1] // 2


def _head_store(ref, h, r, lane0, value):
    ref[h, r, lane0:lane0 + value.shape[-1]] = value.astype(ref.dtype)


def _store_values_with_ones(v_ref, r, v, dv):
    ones = jnp.ones((v.shape[0], dv), v_ref.dtype)
    for hd in range(v_ref.shape[0]):
        _head_store(v_ref, hd, r, 0, v[:, hd * dv:(hd + 1) * dv])
        _head_store(v_ref, hd, r, dv, ones)


def _mla_proj_kernel(x_ref, mod_ref, wd_ref, qn_ref, kvn_ref, wq_ref, wk_ref, wv_ref, tab_ref,
                     q_ref, k_ref, v_ref, *state_refs, n_heads, q_lora, kv_lora, nope, q_scale, rows):
    m = mod_ref[...]
    dq = 2 * nope
    for r0 in range(0, x_ref.shape[0], rows):
        r = slice(r0, r0 + rows)
        h = _modulated(x_ref[r, :], m, 1).astype(BF16)
        down = _dot(h, wd_ref[...])
        cq = _rms(down[:, :q_lora], qn_ref[...])
        ckv = _rms(down[:, q_lora:q_lora + kv_lora], kvn_ref[...])
        pe = down[:, q_lora + kv_lora:]
        tab = tab_ref[r, :]
        if state_refs:
            state_refs[0][r, :] = ckv
            state_refs[1][r, :] = pe[:, :pe.shape[-1] // 2]
        kpe = jnp.where(_rope_half_mask(pe.shape), _rotary_pair(pe, tab), 0.0).astype(BF16)
        q = _dot(cq.astype(BF16), wq_ref[...]) * q_scale
        ckv_b = ckv.astype(BF16)
        kn = _dot(ckv_b, wk_ref[...])
        _store_values_with_ones(v_ref, r, _dot(ckv_b, wv_ref[...]), nope)
        for hd in range(n_heads):
            _head_store(q_ref, hd, r, 0, q[:, hd * dq:hd * dq + nope])
            _head_store(q_ref, hd, r, nope, _rotary_pair(q[:, hd * dq + nope:(hd + 1) * dq], tab))
            _head_store(k_ref, hd, r, 0, kn[:, hd * nope:(hd + 1) * nope])
            _head_store(k_ref, hd, r, nope, kpe)


def _mla_proj_call(x, mod, tokens_per_cond, wd, qn, kvn, wq, wk, wv, tab, i_odd, dims, with_state,
                   tm=512, rows=256):
    n_heads, q_lora, kv_lora, nope, rope, q_scale = dims
    t, d = x.shape
    tm = min(tm, tokens_per_cond, tab.shape[0])
    per = tokens_per_cond // tm
    tab_blocks = tab.shape[0] // tm
    dq = 2 * nope
    out_shape = [jax.ShapeDtypeStruct((n_heads, t, dq), BF16)] * 3
    out_specs = [pl.BlockSpec((n_heads, tm, dq), lambda i: (0, i, 0))] * 3
    if with_state:
        out_shape += [jax.ShapeDtypeStruct((t, kv_lora), F32), jax.ShapeDtypeStruct((t, rope), F32)]
        out_specs += [pl.BlockSpec((tm, kv_lora), lambda i: (i, 0)),
                      pl.BlockSpec((tm, rope), lambda i: (i, 0))]
    whole = lambda a: _resident((None,) + a.shape[1:], (i_odd,) + (0,) * (a.ndim - 1))
    return pl.pallas_call(
        functools.partial(_mla_proj_kernel, n_heads=n_heads, q_lora=q_lora, kv_lora=kv_lora,
                          nope=nope, q_scale=q_scale, rows=min(rows, tm)),
        out_shape=out_shape,
        grid=(t // tm,),
        in_specs=[pl.BlockSpec((tm, d), lambda i: (i, 0)),
                  pl.BlockSpec((None, 9, d), lambda i: (i // per, 0, 0)),
                  whole(wd), whole(qn), whole(kvn), whole(wq), whole(wk), whole(wv),
                  pl.BlockSpec((tm, tab.shape[1]), lambda i: (i % tab_blocks, 0))],
        out_specs=out_specs,
        compiler_params=_params("parallel"),
        name="mla_proj",
    )(x, mod, wd, qn, kvn, wq, wk, wv, tab)


def _mla_expand_kernel(ckv_ref, kpe_ref, wk_ref, wv_ref, k_ref, v_ref, *, n_heads, nope):
    ckv = ckv_ref[...].astype(BF16)
    kn = _dot(ckv, wk_ref[...])
    r = slice(None)
    _store_values_with_ones(v_ref, r, _dot(ckv, wv_ref[...]), nope)
    kpe = kpe_ref[...].astype(BF16)
    for hd in range(n_heads):
        _head_store(k_ref, hd, r, 0, kn[:, hd * nope:(hd + 1) * nope])
        _head_store(k_ref, hd, r, nope, kpe)


def _mla_expand_call(ckv, kpe_pad, wk, wv, i_odd, n_heads, nope, tm=256):
    t, kv_lora = ckv.shape
    tm = min(tm, t)
    dq = 2 * nope
    whole = lambda a: pl.BlockSpec((None,) + a.shape[1:], lambda i: (i_odd,) + (0,) * (a.ndim - 1))
    return pl.pallas_call(
        functools.partial(_mla_expand_kernel, n_heads=n_heads, nope=nope),
        out_shape=[jax.ShapeDtypeStruct((n_heads, t, dq), BF16)] * 2,
        grid=(t // tm,),
        in_specs=[pl.BlockSpec((tm, kv_lora), lambda i: (i, 0)),
                  pl.BlockSpec((tm, kpe_pad.shape[1]), lambda i: (i, 0)),
                  whole(wk), whole(wv)],
        out_specs=[pl.BlockSpec((n_heads, tm, dq), lambda i: (0, i, 0))] * 2,
        compiler_params=_params("parallel"),
        name="mla_expand",
    )(ckv, kpe_pad, wk, wv)


def _rotate_half_columns(w, rope):
    q = rope // 4
    parts = [w[..., i * q:(i + 1) * q] for i in range(4)]
    return jnp.concatenate([-parts[1], parts[0], -parts[3], parts[2]], axis=-1)


def _rope_table(n_tokens, rope):
    axis = rope // 2
    t = jnp.arange(n_tokens)
    inv = ROPE_BASE ** (-jnp.arange(0, axis, 2, dtype=F32) / axis)
    ang_r = (t // GRID_W).astype(F32)[:, None] * inv[None, :]
    ang_c = (t % GRID_W).astype(F32)[:, None] * inv[None, :]
    ang = jnp.concatenate([ang_r, ang_r, ang_c, ang_c], axis=-1)
    return jnp.concatenate([jnp.cos(ang), jnp.sin(ang)], axis=-1)


def _identity_rope_table(n_tokens, rope):
    return jnp.concatenate([jnp.ones((n_tokens, rope), F32), jnp.zeros((n_tokens, rope), F32)], axis=-1)


def kernel(x_prompt, x_sample, cache_na_k, cache_na_v, cache_mla_ckv, cache_mla_kpe, c, c_ctx,
           w_mod, b_mod, ln_g, ln_b, ffn_w1, ffn_w3, ffn_w2,
           na_w_in, mix0_w_out, na_rpb, pool_w, pool_scale,
           mla_w_down, mla_q_norm, mla_w_uq, mla_kv_norm, mla_w_ukv, mla_w_out):
    depth, d, _ = w_mod.shape
    alpha = (2 * depth) ** 0.25
    batch, seq, _ = x_prompt.shape
    dec_batch, dec_seq, _ = x_sample.shape
    na_heads, na_dh = cache_na_k.shape[-2:]
    na_width = na_heads * na_dh
    pool_groups, pool_ch = pool_w.shape[1:3]
    assert pool_groups * pool_ch == na_width and pool_ch % 128 == 0
    q_lora, kv_lora = mla_q_norm.shape[-1], mla_kv_norm.shape[-1]
    rope = cache_mla_kpe.shape[-1]
    uq_w, ukv_w, out_rows = mla_w_uq.shape[-1], mla_w_ukv.shape[-1], mla_w_out.shape[1]
    mla_heads = (uq_w - ukv_w + out_rows) // rope
    nope = uq_w // mla_heads - rope
    v_dim = out_rows // mla_heads
    assert nope == v_dim and 2 * rope == nope and nope % 128 == 0
    mla_dims = (mla_heads, q_lora, kv_lora, nope, rope, float((nope + rope) ** -0.5) * LOG2E)

    w13, w2 = _ffn_chunked(ffn_w1, ffn_w3, FFN_CHUNK), ffn_w2.astype(BF16)
    w_in = na_w_in.astype(BF16)
    w_out0 = mix0_w_out.astype(BF16)
    w_pool = pool_w.astype(BF16)
    p_scale = pool_scale.reshape(pool_scale.shape[0], 1, -1)
    kpe_cols = mla_w_down[..., q_lora + kv_lora:]
    wd = jnp.concatenate([mla_w_down, _rotate_half_columns(kpe_cols, rope)], axis=-1).astype(BF16)
    n_odd = mla_w_uq.shape[0]
    uq = mla_w_uq.reshape(n_odd, q_lora, mla_heads, nope + rope)
    wq = jnp.concatenate([uq, _rotate_half_columns(uq[..., nope:], rope)], axis=-1)
    wq = wq.reshape(n_odd, q_lora, mla_heads * 2 * nope).astype(BF16)
    ukv = mla_w_ukv.reshape(n_odd, kv_lora, mla_heads, nope + v_dim)
    wk = ukv[..., :nope].reshape(n_odd, kv_lora, mla_heads * nope).astype(BF16)
    wv = ukv[..., nope:].reshape(n_odd, kv_lora, mla_heads * v_dim).astype(BF16)
    w_out1 = mla_w_out.astype(BF16)
    qn = mla_q_norm.reshape(n_odd, 1, q_lora)
    kvn = mla_kv_norm.reshape(n_odd, 1, kv_lora)

    n_cond = 1 + dec_batch
    cond = jnp.concatenate([c_ctx[None, :], c, jnp.zeros((-n_cond % 16, d), F32)], axis=0)
    mod = _mod_call(cond, w_mod, b_mod).reshape(depth, cond.shape[0], 9, d)

    rope_tab = _rope_table(dec_seq, rope)
    ident_tab = _identity_rope_table(min(256, seq), rope)

    def trunk(x3, mod_rows, per_cond, caches):
        bsz, l, _ = x3.shape
        x = x3.reshape(bsz * l, d)
        states = []
        for layer in range(depth):
            m = mod_rows[layer]
            g = lambda s: ln_g[layer, s].reshape(1, d)
            bb = lambda s: ln_b[layer, s].reshape(1, d)
            x = _ffn_call(x, m, per_cond, w13, w2, layer, 0, g(0), bb(0), 0, alpha)
            i = layer // 2
            if layer % 2 == 0:
                outs = _na_proj_call(x, m, per_cond, w_in, i, na_heads, na_dh, pool_groups, pool_ch,
                                     float(na_dh ** -0.5) * LOG2E, caches is None)
                qkv = outs[0].reshape(3 * na_heads, bsz, l, na_dh)
                u = outs[1].reshape(pool_groups, bsz, l, pool_ch)
                if caches is None:
                    states.append((outs[2], outs[3]))
                    a = _attn_call(qkv, 0, [(qkv, na_heads, qkv, 2 * na_heads)], na_heads, na_dh,
                                   256, 512, "na_dense_attn", heads=4)
                else:
                    kc = caches[0][:, i].reshape(bsz, -1, na_width).astype(BF16)
                    vc = caches[1][:, i].reshape(bsz, -1, na_width).astype(BF16)
                    bias = _na_bias_table(na_rpb[i], l // GRID_W)
                    a = _na_call(qkv, kc, vc, bias, (na_rpb.shape[2] + 1) // 2)
                pooled = _pool_call(u, w_pool, p_scale, i)
                x = _out_call(x, m, per_cond,
                              [(a.reshape(bsz * l, na_width), w_out0, (None, na_width, d), (i, 0, 0)),
                               (pooled.reshape(bsz * l, na_width), w_out0, (None, na_width, d), (i, 1, 0))],
                              g(1), bb(1), alpha)
            else:
                tab = ident_tab if caches is None else rope_tab
                outs = _mla_proj_call(x, m, per_cond, wd, qn, kvn, wq, wk, wv, tab, i, mla_dims,
                                      caches is None)
                q, k, v = (o.reshape(mla_heads, bsz, l, 2 * nope) for o in outs[:3])
                segments = [(k, 0, v, 0)]
                if caches is None:
                    states.append((outs[3], outs[4]))
                else:
                    ckv_c = caches[2][:, i].reshape(-1, kv_lora)
                    kpe_c = caches[3][:, i].reshape(-1, rope)
                    kpe_c = jnp.concatenate([kpe_c, jnp.zeros_like(kpe_c)], axis=-1)
                    k_c, v_c = _mla_expand_call(ckv_c, kpe_c, wk, wv, i, mla_heads, nope)
                    segments.append((k_c.reshape(mla_heads, bsz, -1, 2 * nope), 0,
                                     v_c.reshape(mla_heads, bsz, -1, 2 * nope), 0))
                o = _attn_call(q, 0, segments, mla_heads, v_dim, 1024, 256, "mla_attn",
                               heads=1 if l > 1024 else 4)
                x = _out_call(x, m, per_cond,
                              [(o.reshape(bsz * l, -1), w_out1, (None,) + w_out1.shape[1:], (i, 0, 0))],
                              g(1), bb(1), alpha)
            x = _ffn_call(x, m, per_cond, w13, w2, layer, 1, g(2), bb(2), 2, alpha)
        return x.reshape(bsz, l, d), states

    y_prompt, st = trunk(x_prompt, mod[:, 0:1], batch * seq, None)
    y_sample, _ = trunk(x_sample, mod[:, 1:n_cond], dec_seq,
                        (cache_na_k, cache_na_v, cache_mla_ckv, cache_mla_kpe))
    even = [s for layer, s in enumerate(st) if layer % 2 == 0]
    odd = [s for layer, s in enumerate(st) if layer % 2 == 1]
    new_na_k = jnp.stack([s[0].reshape(batch, seq, na_heads, na_dh) for s in even], axis=1)
    new_na_v = jnp.stack([s[1].reshape(batch, seq, na_heads, na_dh) for s in even], axis=1)
    new_mla_ckv = jnp.stack([s[0].reshape(batch, seq, kv_lora) for s in odd], axis=1)
    new_mla_kpe = jnp.stack([s[1].reshape(batch, seq, rope) for s in odd], axis=1)
    return (y_prompt, y_sample, new_na_k, new_na_v, new_mla_ckv, new_mla_kpe)
```

```python
import functools
import math

import numpy as np
import jax
import jax.numpy as jnp
from jax import lax
from jax.experimental import pallas as pl
from jax.experimental.pallas import tpu as pltpu

GRID_W = 64
LN_EPS = 1e-5
RMS_EPS = 1e-6
POOL_WINDOWS = (2, 4, 8, 16)
ROPE_BASE = 10000.0
MASKED = -1e30
LOG2E = math.log2(math.e)
FFN_CHUNK = 512
ATTN_ROWS = 512
NA_GROUP_ROWS = 4
NA_WINDOW_ROWS = 12
NA_KEY_CHUNK = 256
V7X_VMEM_LIMIT_BYTES = 56 * 1024 * 1024

F32 = jnp.float32
BF16 = jnp.bfloat16
_NT = (((1,), (1,)), ((), ()))


def _params(*semantics):
    return pltpu.CompilerParams(dimension_semantics=semantics,
                                vmem_limit_bytes=V7X_VMEM_LIMIT_BYTES)


def _dot(a, b):
    return jnp.dot(a, b, preferred_element_type=F32)


def _post_norm(x, y, row_scale, g, b, alpha):
    z = x + (row_scale * (1.0 / alpha)) * y
    mu = jnp.mean(z, axis=-1, keepdims=True)
    zc = z - mu
    var = jnp.mean(zc * zc, axis=-1, keepdims=True)
    return zc * lax.rsqrt(var + LN_EPS / (alpha * alpha)) * g + b


def _modulated(x, m, j):
    return x * (1.0 + m[3 * j + 1:3 * j + 2]) + m[3 * j:3 * j + 1]


def _mod_kernel(c_ref, w_ref, b_ref, o_ref):
    c = c_ref[...]
    h = (c * jax.nn.sigmoid(c)).astype(BF16)
    o_ref[...] = _dot(h, w_ref[...].astype(BF16)) + b_ref[...]


def _mod_call(cond, w_mod, b_mod, tn=1024):
    depth, d, n = w_mod.shape
    r = cond.shape[0]
    return pl.pallas_call(
        _mod_kernel,
        out_shape=jax.ShapeDtypeStruct((depth, r, n), F32),
        grid=(depth, n // tn),
        in_specs=[pl.BlockSpec((r, d), lambda l, j: (0, 0)),
                  pl.BlockSpec((None, d, tn), lambda l, j: (l, 0, j)),
                  pl.BlockSpec((None, 1, tn), lambda l, j: (l, 0, j))],
        out_specs=pl.BlockSpec((None, r, tn), lambda l, j: (l, 0, j)),
        compiler_params=_params("parallel", "parallel"),
        name="mod_proj",
    )(cond, w_mod, b_mod.reshape(depth, 1, n))


def _ffn_kernel(x_ref, mod_ref, w13_ref, w2_ref, g_ref, b_ref, o_ref, h_ref, *, j, alpha, rows, mid_rows):
    f = pl.program_id(1)
    last = pl.num_programs(1) - 1
    tm = x_ref.shape[0]
    tf = w2_ref.shape[0]

    def chunk_update(h):
        ab = _dot(h, w13_ref[...])
        a, b = ab[:, :tf], ab[:, tf:]
        act = (a * jax.nn.sigmoid(a) * b).astype(BF16)
        return _dot(act, w2_ref[...])

    @pl.when(f == 0)
    def _():
        m = mod_ref[...]
        for r0 in range(0, tm, rows):
            h = _modulated(x_ref[r0:r0 + rows, :], m, j).astype(BF16)
            h_ref[r0:r0 + rows, :] = h
            o_ref[r0:r0 + rows, :] = chunk_update(h)

    @pl.when((f > 0) & (f < last))
    def _():
        for r0 in range(0, tm, mid_rows):
            o_ref[r0:r0 + mid_rows, :] += chunk_update(h_ref[r0:r0 + mid_rows, :])

    @pl.when(f == last)
    def _():
        half_gate = 0.5 * mod_ref[...][3 * j + 2:3 * j + 3]
        for r0 in range(0, tm, rows):
            y = o_ref[r0:r0 + rows, :] + chunk_update(h_ref[r0:r0 + rows, :])
            o_ref[r0:r0 + rows, :] = _post_norm(x_ref[r0:r0 + rows, :], y, half_gate,
                                                g_ref[...], b_ref[...], alpha)


def _ffn_pack_kernel(w1_ref, w3_ref, o_ref):
    tf = w1_ref.shape[-1]
    o_ref[:, :tf] = w1_ref[...].astype(BF16)
    o_ref[:, tf:] = w3_ref[...].astype(BF16)


def _ffn_chunked(w1, w3, tf):
    depth, n_sub, d, ff = w1.shape
    tf = min(tf, ff)
    assert ff % tf == 0 and ff // tf >= 2
    spec = pl.BlockSpec((None, None, d, tf), lambda l, s, f: (l, s, 0, f))
    return pl.pallas_call(
        _ffn_pack_kernel,
        out_shape=jax.ShapeDtypeStruct((depth, n_sub, ff // tf, d, 2 * tf), BF16),
        grid=(depth, n_sub, ff // tf),
        in_specs=[spec, spec],
        out_specs=pl.BlockSpec((None, None, None, d, 2 * tf), lambda l, s, f: (l, s, f, 0, 0)),
        compiler_params=_params("parallel", "parallel", "parallel"),
        name="ffn_pack",
    )(w1, w3)


def _ffn_call(x, mod, tokens_per_cond, w13, w2, l, s, ln_g, ln_b, j, alpha, tm=1024, rows=256,
              mid_rows=512):
    t, d = x.shape
    n_chunks = w13.shape[2]
    tf = w13.shape[-1] // 2
    tm = min(tm, tokens_per_cond)
    assert tokens_per_cond % tm == 0
    per = tokens_per_cond // tm
    return pl.pallas_call(
        functools.partial(_ffn_kernel, j=j, alpha=alpha, rows=min(rows, tm), mid_rows=min(mid_rows, tm)),
        out_shape=jax.ShapeDtypeStruct((t, d), F32),
        grid=(t // tm, n_chunks),
        in_specs=[pl.BlockSpec((tm, d), lambda i, f: (i, 0)),
                  pl.BlockSpec((None, 9, d), lambda i, f: (i // per, 0, 0)),
                  pl.BlockSpec((None, None, None, d, 2 * tf), lambda i, f: (l, s, f, 0, 0)),
                  pl.BlockSpec((None, None, tf, d), lambda i, f: (l, s, f, 0)),
                  pl.BlockSpec((1, d), lambda i, f: (0, 0)),
                  pl.BlockSpec((1, d), lambda i, f: (0, 0))],
        out_specs=pl.BlockSpec((tm, d), lambda i, f: (i, 0)),
        scratch_shapes=[pltpu.VMEM((tm, d), BF16)],
        compiler_params=_params("parallel", "arbitrary"),
        name="ffn",
    )(x, mod, w13, w2, ln_g, ln_b)


def _na_proj_kernel(x_ref, mod_ref, w_ref, qkv_ref, u_ref, *kv_refs, q_scale, rows):
    m = mod_ref[...]
    n_heads, dh = qkv_ref.shape[0] // 3, qkv_ref.shape[-1]
    n_pool, ch = u_ref.shape[0], u_ref.shape[-1]
    width = n_heads * dh
    for r0 in range(0, x_ref.shape[0], rows):
        r = slice(r0, r0 + rows)
        h = _modulated(x_ref[r, :], m, 1).astype(BF16)
        for n in range(3):
            y = _dot(h, w_ref[:, n * width:(n + 1) * width])
            if kv_refs and n > 0:
                kv_refs[n - 1][r, :] = y
            y = (y * q_scale if n == 0 else y).astype(BF16)
            for hd in range(n_heads):
                qkv_ref[n * n_heads + hd, r, :] = y[:, hd * dh:(hd + 1) * dh]
        y = _dot(h, w_ref[:, 3 * width:]).astype(BF16)
        for g in range(n_pool):
            u_ref[g, r, :] = y[:, g * ch:(g + 1) * ch]


def _resident(block_shape, index):
    return pl.BlockSpec(block_shape, lambda *_: index, pipeline_mode=pl.Buffered(1))


def _na_proj_call(x, mod, tokens_per_cond, w_in, i_even, n_heads, dh, n_pool, ch, q_scale, with_kv,
                  tm=512, rows=256):
    t, d = x.shape
    n_in = w_in.shape[-1]
    na_width = n_heads * dh
    assert n_in == 3 * na_width + n_pool * ch
    tm = min(tm, tokens_per_cond)
    per = tokens_per_cond // tm
    out_shape = [jax.ShapeDtypeStruct((3 * n_heads, t, dh), BF16),
                 jax.ShapeDtypeStruct((n_pool, t, ch), BF16)]
    out_specs = [pl.BlockSpec((3 * n_heads, tm, dh), lambda i: (0, i, 0)),
                 pl.BlockSpec((n_pool, tm, ch), lambda i: (0, i, 0))]
    if with_kv:
        out_shape += [jax.ShapeDtypeStruct((t, na_width), F32)] * 2
        out_specs += [pl.BlockSpec((tm, na_width), lambda i: (i, 0))] * 2
    return pl.pallas_call(
        functools.partial(_na_proj_kernel, q_scale=q_scale, rows=min(rows, tm)),
        out_shape=out_shape,
        grid=(t // tm,),
        in_specs=[pl.BlockSpec((tm, d), lambda i: (i, 0)),
                  pl.BlockSpec((None, 9, d), lambda i: (i // per, 0, 0)),
                  _resident((None, d, n_in), (i_even, 0, 0))],
        out_specs=out_specs,
        compiler_params=_params("parallel"),
        name="na_proj",
    )(x, mod, w_in)


def _attn_rows(q, kv_refs, chunk, dv):
    rows = q.shape[0]
    den_in_v = kv_refs[1].shape[-1] == 2 * dv
    m = jnp.full((rows, 1), MASKED, F32)
    acc = jnp.zeros((rows, kv_refs[1].shape[-1]), F32)
    den = jnp.zeros((rows, 1), F32)
    for k_ref, v_ref in zip(kv_refs[0::2], kv_refs[1::2]):
        lk = k_ref.shape[0]
        step = min(chunk, lk)
        for c0 in range(0, lk, step):
            s = lax.dot_general(q, k_ref[c0:c0 + step, :], _NT, preferred_element_type=F32)
            m_new = jnp.maximum(m, jnp.max(s, axis=-1, keepdims=True))
            rescale = jnp.exp2(m - m_new)
            p = jnp.exp2(s - m_new)
            if not den_in_v:
                den = rescale * den + jnp.sum(p, axis=-1, keepdims=True)
            acc = rescale * acc + _dot(p.astype(BF16), v_ref[c0:c0 + step, :])
            m = m_new
    return acc[:, :dv] / (acc[:, dv:] if den_in_v else den)


def _attn_kernel(*refs, chunk, dv, rows):
    q_ref, o_ref = refs[0], refs[-1]
    for hd in range(q_ref.shape[0]):
        kv_refs = [ref.at[hd] for ref in refs[1:-1]]
        for r0 in range(0, q_ref.shape[1], rows):
            o = _attn_rows(q_ref[hd, r0:r0 + rows, :], kv_refs, chunk, dv)
            o_ref[r0:r0 + rows, hd * dv:(hd + 1) * dv] = o.astype(o_ref.dtype)


def _attn_call(q, q_head0, segments, n_heads, dv, tq, chunk, name, heads=1):
    _, b, l, dq = q.shape
    tq = min(tq, l)
    heads = math.gcd(heads, n_heads, q_head0, *[h0 for _, k0, _, v0 in segments for h0 in (k0, v0)])
    in_specs = [pl.BlockSpec((heads, None, tq, dq),
                             lambda bi, h, qi: (q_head0 // heads + h, bi, qi, 0))]
    args = [q]
    for k, k0, v, v0 in segments:
        lk = k.shape[2]
        assert lk % min(chunk, lk) == 0
        in_specs.append(pl.BlockSpec((heads, None, lk, dq),
                                     lambda bi, h, qi, k0=k0: (k0 // heads + h, bi, 0, 0)))
        in_specs.append(pl.BlockSpec((heads, None, lk, v.shape[-1]),
                                     lambda bi, h, qi, v0=v0: (v0 // heads + h, bi, 0, 0)))
        args += [k, v]
    return pl.pallas_call(
        functools.partial(_attn_kernel, chunk=chunk, dv=dv, rows=min(ATTN_ROWS, tq)),
        out_shape=jax.ShapeDtypeStruct((b, l, n_heads * dv), BF16),
        grid=(b, n_heads // heads, l // tq),
        in_specs=in_specs,
        out_specs=pl.BlockSpec((None, tq, heads * dv), lambda bi, h, qi: (bi, qi, h)),
        compiler_params=_params("parallel", "parallel", "arbitrary"),
        name=name,
    )(*args)


def _na_kernel(q_ref, k_ref, v_ref, kc_ref, vc_ref, bias_ref, o_ref, *, n_rows, half_rows, groups):
    n_groups = n_rows // NA_GROUP_ROWS
    tq = NA_GROUP_ROWS * GRID_W
    win = NA_WINDOW_ROWS * GRID_W
    for c in range(groups):
        g = pl.program_id(2) * groups + c
        base_row = jnp.clip(g * NA_GROUP_ROWS - half_rows, 0, n_rows - NA_WINDOW_ROWS)
        base = pl.multiple_of(base_row * GRID_W, GRID_W)
        variant = jnp.where(g == 0, 0, jnp.where(g == n_groups - 1, 2, 1))
        q = q_ref[c * tq:(c + 1) * tq, :]
        m = jnp.full((tq, 1), MASKED, F32)
        acc = jnp.zeros((tq, o_ref.shape[-1]), F32)
        den = jnp.zeros((tq, 1), F32)
        chunks = [(k_ref, v_ref, base + c0, c0) for c0 in range(0, win, NA_KEY_CHUNK)]
        chunks += [(kc_ref, vc_ref, c0, None) for c0 in range(0, kc_ref.shape[0], NA_KEY_CHUNK)]
        for kr, vr, k0, b0 in chunks:
            s = lax.dot_general(q, kr[pl.ds(k0, NA_KEY_CHUNK), :], _NT, preferred_element_type=F32)
            if b0 is not None:
                s = s + bias_ref[variant, :, b0:b0 + NA_KEY_CHUNK]
            m_new = jnp.maximum(m, jnp.max(s, axis=-1, keepdims=True))
            rescale = jnp.exp2(m - m_new)
            p = jnp.exp2(s - m_new)
            den = rescale * den + jnp.sum(p, axis=-1, keepdims=True)
            acc = rescale * acc + _dot(p.astype(BF16), vr[pl.ds(k0, NA_KEY_CHUNK), :])
            m = m_new
        o_ref[c * tq:(c + 1) * tq, :] = (acc / den).astype(o_ref.dtype)


def _na_group_rows(g, n_rows, na_rows):
    kr = min(na_rows, n_rows)
    base = int(np.clip(g * NA_GROUP_ROWS - kr // 2, 0, n_rows - NA_WINDOW_ROWS))
    rq = (g * NA_GROUP_ROWS + np.arange(NA_GROUP_ROWS))[:, None]
    rk = (base + np.arange(NA_WINDOW_ROWS))[None, :]
    rs = np.clip(rq - kr // 2, 0, n_rows - kr)
    ok = (rk >= rs) & (rk < rs + kr)
    assert (ok.sum(axis=1) == kr).all(), "key window misses part of a neighbourhood"
    return np.where(ok, rk - rq + na_rows - 1, 0), ok


def _na_bias_table(rpb, n_rows):
    n_heads, nr2, nc2 = rpb.shape
    na_rows, na_cols = (nr2 + 1) // 2, (nc2 + 1) // 2
    n_groups = n_rows // NA_GROUP_ROWS
    per_group = [_na_group_rows(g, n_rows, na_rows) for g in range(n_groups)]
    for g in range(2, n_groups - 1):
        assert all((a == b).all() for a, b in zip(per_group[g], per_group[1])), "interior groups differ"
    idx_r, row_ok = (np.stack([per_group[g][i] for g in (0, 1, n_groups - 1)]) for i in range(2))
    w = GRID_W
    e = jnp.pad(rpb, ((0, 0), (0, 0), (w - na_cols, 2 * w - (w - na_cols) - nc2)))
    band = jnp.tile(e, (1, 1, w))[..., :w * (2 * w - 1)].reshape(n_heads, nr2, w, 2 * w - 1)[..., w - 1:]
    cq, ck = np.arange(w)[:, None], np.arange(w)[None, :]
    cs = np.clip(cq - na_cols // 2, 0, w - na_cols)
    col_ok = (ck >= cs) & (ck < cs + na_cols)
    blocks = jnp.take(band, jnp.asarray(idx_r.reshape(-1)), axis=1)
    blocks = blocks.reshape((n_heads,) + idx_r.shape + (w, w))
    ok = row_ok[:, :, :, None, None] & col_ok[None, None, None]
    table = jnp.where(ok[None], blocks * LOG2E, MASKED).transpose(0, 1, 2, 4, 3, 5)
    return table.reshape(n_heads, 3, NA_GROUP_ROWS * w, NA_WINDOW_ROWS * w).astype(F32)


def _na_call(qkv, kc, vc, bias, na_rows, groups=16):
    n_heads, b, l, dh = qkv.shape[0] // 3, *qkv.shape[1:]
    n_rows = l // GRID_W
    assert n_rows % NA_GROUP_ROWS == 0 and n_rows >= NA_WINDOW_ROWS + NA_GROUP_ROWS
    assert NA_GROUP_ROWS - 1 + min(na_rows, n_rows) <= NA_WINDOW_ROWS
    assert (NA_WINDOW_ROWS * GRID_W) % NA_KEY_CHUNK == 0 and kc.shape[1] % NA_KEY_CHUNK == 0
    n_groups = n_rows // NA_GROUP_ROWS
    groups = math.gcd(n_groups, groups)
    tq = groups * NA_GROUP_ROWS * GRID_W
    lc = kc.shape[1]
    return pl.pallas_call(
        functools.partial(_na_kernel, n_rows=n_rows, half_rows=min(na_rows, n_rows) // 2,
                          groups=groups),
        out_shape=jax.ShapeDtypeStruct((b, l, n_heads * dh), BF16),
        grid=(b, n_heads, n_groups // groups),
        in_specs=[pl.BlockSpec((None, None, tq, dh), lambda bi, h, g: (h, bi, g, 0)),
                  pl.BlockSpec((None, None, l, dh), lambda bi, h, g: (n_heads + h, bi, 0, 0)),
                  pl.BlockSpec((None, None, l, dh), lambda bi, h, g: (2 * n_heads + h, bi, 0, 0)),
                  pl.BlockSpec((None, lc, dh), lambda bi, h, g: (bi, 0, h)),
                  pl.BlockSpec((None, lc, dh), lambda bi, h, g: (bi, 0, h)),
                  pl.BlockSpec((None,) + bias.shape[1:], lambda bi, h, g: (h, 0, 0, 0))],
        out_specs=pl.BlockSpec((None, tq, dh), lambda bi, h, g: (bi, g, h)),
        compiler_params=_params("parallel", "parallel", "arbitrary"),
        name="na_attn",
    )(qkv, qkv, qkv, kc, vc, bias)


def _pool_kernel(u_ref, w_ref, s_ref, o_ref, *, chunk, window):
    l = u_ref.shape[0]
    half = lax.shift_left(jnp.int32(1), pl.program_id(1).astype(jnp.int32))
    w = w_ref[...]
    scale = s_ref[...]

    def body(c, carry):
        t0 = pl.multiple_of(c * chunk, chunk)
        s0 = pl.multiple_of(jnp.clip(t0 - (window - chunk) // 2, 0, l - window), 16)
        t = t0 + lax.broadcasted_iota(jnp.int32, (chunk, 1), 0)
        lo = jnp.maximum(t - half, 0)
        hi = jnp.minimum(t + half, l)
        pos = s0 + lax.broadcasted_iota(jnp.int32, (1, window), 1)
        band = jnp.where((pos >= lo) & (pos < hi), 1.0, 0.0).astype(BF16)
        sums = _dot(band, u_ref[pl.ds(s0, window), :])
        mean = sums / (hi - lo).astype(F32)
        d = (mean - u_ref[pl.ds(t0, chunk), :].astype(F32)).astype(BF16)
        o_ref[pl.ds(t0, chunk), :] = (_dot(d, w) * scale).astype(o_ref.dtype)
        return carry

    lax.fori_loop(0, l // chunk, body, 0, unroll=min(4, l // chunk))


def _pool_call(u, w_pool, pool_scale, i_even):
    _, b, l, _ = u.shape
    n_groups, ch, _ = w_pool.shape[1:]
    assert n_groups == len(POOL_WINDOWS) and all(w == 2 << g for g, w in enumerate(POOL_WINDOWS))
    chunk = min(256, l)
    window = min(2 * chunk, l)
    assert l % chunk == 0 and (window == l or window - chunk >= max(POOL_WINDOWS))
    return pl.pallas_call(
        functools.partial(_pool_kernel, chunk=chunk, window=window),
        out_shape=jax.ShapeDtypeStruct((b, l, n_groups * ch), BF16),
        grid=(b, n_groups),
        in_specs=[pl.BlockSpec((None, None, l, ch), lambda bi, g: (g, bi, 0, 0)),
                  pl.BlockSpec((None, None, ch, ch), lambda bi, g: (i_even, g, 0, 0)),
                  pl.BlockSpec((None, 1, ch), lambda bi, g: (i_even, 0, g))],
        out_specs=pl.BlockSpec((None, l, ch), lambda bi, g: (bi, 0, g)),
        compiler_params=_params("parallel", "parallel"),
        name="pool",
    )(u, w_pool, pool_scale)


def _out_kernel(*refs, n_lhs, alpha, rows):
    x_ref, mod_ref = refs[0], refs[1]
    g_ref, b_ref, o_ref = refs[-3], refs[-2], refs[-1]
    gate = mod_ref[...][5:6]
    for r0 in range(0, x_ref.shape[0], rows):
        r = slice(r0, r0 + rows)
        y = functools.reduce(jnp.add, [_dot(refs[2 + 2 * i][r, :], refs[3 + 2 * i][...])
                                       for i in range(n_lhs)])
        o_ref[r, :] = _post_norm(x_ref[r, :], y, gate, g_ref[...], b_ref[...], alpha)


def _out_call(x, mod, tokens_per_cond, lhs_w, ln_g, ln_b, alpha, tm=1024, rows=256):
    t, d = x.shape
    tm = min(tm, tokens_per_cond)
    per = tokens_per_cond // tm
    in_specs = [pl.BlockSpec((tm, d), lambda i: (i, 0)),
                pl.BlockSpec((None, 9, d), lambda i: (i // per, 0, 0))]
    args = [x, mod]
    for lhs, w, w_block, w_index in lhs_w:
        in_specs.append(pl.BlockSpec((tm, lhs.shape[1]), lambda i: (i, 0)))
        in_specs.append(_resident(w_block, w_index))
        args += [lhs, w]
    in_specs += [pl.BlockSpec((1, d), lambda i: (0, 0))] * 2
    args += [ln_g, ln_b]
    return pl.pallas_call(
        functools.partial(_out_kernel, n_lhs=len(lhs_w), alpha=alpha, rows=min(rows, tm)),
        out_shape=jax.ShapeDtypeStruct((t, d), F32),
        grid=(t // tm,),
        in_specs=in_specs,
        out_specs=pl.BlockSpec((tm, d), lambda i: (i, 0)),
        compiler_params=_params("parallel"),
        name="out_proj",
    )(*args)


def _rms(x, g):
    return x * lax.rsqrt(jnp.mean(x * x, axis=-1, keepdims=True) + RMS_EPS) * g


def _rotary_pair(t, tab):
    u = t * tab
    return u + pltpu.roll(u, u.shape[-1] // 2, 1)


def _rope_half_mask(shape):
    return lax.broadcasted_iota(jnp.int32, shape, 1) < shape[-1] // 2


def _head_store(ref, h, r, lane0, value):
    ref[h, r, lane0:lane0 + value.shape[-1]] = value.astype(ref.dtype)


def _store_values_with_ones(v_ref, r, v, dv):
    ones = jnp.ones((v.shape[0], dv), v_ref.dtype)
    for hd in range(v_ref.shape[0]):
        _head_store(v_ref, hd, r, 0, v[:, hd * dv:(hd + 1) * dv])
        _head_store(v_ref, hd, r, dv, ones)


def _mla_proj_kernel(x_ref, mod_ref, wd_ref, qn_ref, kvn_ref, wq_ref, wk_ref, wv_ref, tab_ref,
                     q_ref, k_ref, v_ref, *state_refs, n_heads, q_lora, kv_lora, nope, q_scale, rows):
    m = mod_ref[...]
    dq = 2 * nope
    for r0 in range(0, x_ref.shape[0], rows):
        r = slice(r0, r0 + rows)
        h = _modulated(x_ref[r, :], m, 1).astype(BF16)
        down = _dot(h, wd_ref[...])
        cq = _rms(down[:, :q_lora], qn_ref[...])
        ckv = _rms(down[:, q_lora:q_lora + kv_lora], kvn_ref[...])
        pe = down[:, q_lora + kv_lora:]
        tab = tab_ref[r, :]
        if state_refs:
            state_refs[0][r, :] = ckv
            state_refs[1][r, :] = pe[:, :pe.shape[-1] // 2]
        kpe = jnp.where(_rope_half_mask(pe.shape), _rotary_pair(pe, tab), 0.0).astype(BF16)
        q = _dot(cq.astype(BF16), wq_ref[...]) * q_scale
        ckv_b = ckv.astype(BF16)
        kn = _dot(ckv_b, wk_ref[...])
        _store_values_with_ones(v_ref, r, _dot(ckv_b, wv_ref[...]), nope)
        for hd in range(n_heads):
            _head_store(q_ref, hd, r, 0, q[:, hd * dq:hd * dq + nope])
            _head_store(q_ref, hd, r, nope, _rotary_pair(q[:, hd * dq + nope:(hd + 1) * dq], tab))
            _head_store(k_ref, hd, r, 0, kn[:, hd * nope:(hd + 1) * nope])
            _head_store(k_ref, hd, r, nope, kpe)


def _mla_proj_call(x, mod, tokens_per_cond, wd, qn, kvn, wq, wk, wv, tab, i_odd, dims, with_state,
                   tm=512, rows=256):
    n_heads, q_lora, kv_lora, nope, rope, q_scale = dims
    t, d = x.shape
    tm = min(tm, tokens_per_cond, tab.shape[0])
    per = tokens_per_cond // tm
    tab_blocks = tab.shape[0] // tm
    dq = 2 * nope
    out_shape = [jax.ShapeDtypeStruct((n_heads, t, dq), BF16)] * 3
    out_specs = [pl.BlockSpec((n_heads, tm, dq), lambda i: (0, i, 0))] * 3
    if with_state:
        out_shape += [jax.ShapeDtypeStruct((t, kv_lora), F32), jax.ShapeDtypeStruct((t, rope), F32)]
        out_specs += [pl.BlockSpec((tm, kv_lora), lambda i: (i, 0)),
                      pl.BlockSpec((tm, rope), lambda i: (i, 0))]
    whole = lambda a: _resident((None,) + a.shape[1:], (i_odd,) + (0,) * (a.ndim - 1))
    return pl.pallas_call(
        functools.partial(_mla_proj_kernel, n_heads=n_heads, q_lora=q_lora, kv_lora=kv_lora,
                          nope=nope, q_scale=q_scale, rows=min(rows, tm)),
        out_shape=out_shape,
        grid=(t // tm,),
        in_specs=[pl.BlockSpec((tm, d), lambda i: (i, 0)),
                  pl.BlockSpec((None, 9, d), lambda i: (i // per, 0, 0)),
                  whole(wd), whole(qn), whole(kvn), whole(wq), whole(wk), whole(wv),
                  pl.BlockSpec((tm, tab.shape[1]), lambda i: (i % tab_blocks, 0))],
        out_specs=out_specs,
        compiler_params=_params("parallel"),
        name="mla_proj",
    )(x, mod, wd, qn, kvn, wq, wk, wv, tab)


def _mla_expand_kernel(ckv_ref, kpe_ref, wk_ref, wv_ref, k_ref, v_ref, *, n_heads, nope):
    ckv = ckv_ref[...].astype(BF16)
    kn = _dot(ckv, wk_ref[...])
    r = slice(None)
    _store_values_with_ones(v_ref, r, _dot(ckv, wv_ref[...]), nope)
    kpe = kpe_ref[...].astype(BF16)
    for hd in range(n_heads):
        _head_store(k_ref, hd, r, 0, kn[:, hd * nope:(hd + 1) * nope])
        _head_store(k_ref, hd, r, nope, kpe)


def _mla_expand_call(ckv, kpe_pad, wk, wv, i_odd, n_heads, nope, tm=256):
    t, kv_lora = ckv.shape
    tm = min(tm, t)
    dq = 2 * nope
    whole = lambda a: pl.BlockSpec((None,) + a.shape[1:], lambda i: (i_odd,) + (0,) * (a.ndim - 1))
    return pl.pallas_call(
        functools.partial(_mla_expand_kernel, n_heads=n_heads, nope=nope),
        out_shape=[jax.ShapeDtypeStruct((n_heads, t, dq), BF16)] * 2,
        grid=(t // tm,),
        in_specs=[pl.BlockSpec((tm, kv_lora), lambda i: (i, 0)),
                  pl.BlockSpec((tm, kpe_pad.shape[1]), lambda i: (i, 0)),
                  whole(wk), whole(wv)],
        out_specs=[pl.BlockSpec((n_heads, tm, dq), lambda i: (0, i, 0))] * 2,
        compiler_params=_params("parallel"),
        name="mla_expand",
    )(ckv, kpe_pad, wk, wv)


def _rotate_half_columns(w, rope):
    q = rope // 4
    parts = [w[..., i * q:(i + 1) * q] for i in range(4)]
    return jnp.concatenate([-parts[1], parts[0], -parts[3], parts[2]], axis=-1)


def _rope_table(n_tokens, rope):
    axis = rope // 2
    t = jnp.arange(n_tokens)
    inv = ROPE_BASE ** (-jnp.arange(0, axis, 2, dtype=F32) / axis)
    ang_r = (t // GRID_W).astype(F32)[:, None] * inv[None, :]
    ang_c = (t % GRID_W).astype(F32)[:, None] * inv[None, :]
    ang = jnp.concatenate([ang_r, ang_r, ang_c, ang_c], axis=-1)
    return jnp.concatenate([jnp.cos(ang), jnp.sin(ang)], axis=-1)


def _identity_rope_table(n_tokens, rope):
    return jnp.concatenate([jnp.ones((n_tokens, rope), F32), jnp.zeros((n_tokens, rope), F32)], axis=-1)


def kernel(x_prompt, x_sample, cache_na_k, cache_na_v, cache_mla_ckv, cache_mla_kpe, c, c_ctx,
           w_mod, b_mod, ln_g, ln_b, ffn_w1, ffn_w3, ffn_w2,
           na_w_in, mix0_w_out, na_rpb, pool_w, pool_scale,
           mla_w_down, mla_q_norm, mla_w_uq, mla_kv_norm, mla_w_ukv, mla_w_out):
    depth, d, _ = w_mod.shape
    alpha = (2 * depth) ** 0.25
    batch, seq, _ = x_prompt.shape
    dec_batch, dec_seq, _ = x_sample.shape
    na_heads, na_dh = cache_na_k.shape[-2:]
    na_width = na_heads * na_dh
    pool_groups, pool_ch = pool_w.shape[1:3]
    assert pool_groups * pool_ch == na_width and pool_ch % 128 == 0
    q_lora, kv_lora = mla_q_norm.shape[-1], mla_kv_norm.shape[-1]
    rope = cache_mla_kpe.shape[-1]
    uq_w, ukv_w, out_rows = mla_w_uq.shape[-1], mla_w_ukv.shape[-1], mla_w_out.shape[1]
    mla_heads = (uq_w - ukv_w + out_rows) // rope
    nope = uq_w // mla_heads - rope
    v_dim = out_rows // mla_heads
    assert nope == v_dim and 2 * rope == nope and nope % 128 == 0
    mla_dims = (mla_heads, q_lora, kv_lora, nope, rope, float((nope + rope) ** -0.5) * LOG2E)

    w13, w2 = _ffn_chunked(ffn_w1, ffn_w3, FFN_CHUNK), ffn_w2.astype(BF16)
    w_in = na_w_in.astype(BF16)
    w_out0 = mix0_w_out.astype(BF16)
    w_pool = pool_w.astype(BF16)
    p_scale = pool_scale.reshape(pool_scale.shape[0], 1, -1)
    kpe_cols = mla_w_down[..., q_lora + kv_lora:]
    wd = jnp.concatenate([mla_w_down, _rotate_half_columns(kpe_cols, rope)], axis=-1).astype(BF16)
    n_odd = mla_w_uq.shape[0]
    uq = mla_w_uq.reshape(n_odd, q_lora, mla_heads, nope + rope)
    wq = jnp.concatenate([uq, _rotate_half_columns(uq[..., nope:], rope)], axis=-1)
    wq = wq.reshape(n_odd, q_lora, mla_heads * 2 * nope).astype(BF16)
    ukv = mla_w_ukv.reshape(n_odd, kv_lora, mla_heads, nope + v_dim)
    wk = ukv[..., :nope].reshape(n_odd, kv_lora, mla_heads * nope).astype(BF16)
    wv = ukv[..., nope:].reshape(n_odd, kv_lora, mla_heads * v_dim).astype(BF16)
    w_out1 = mla_w_out.astype(BF16)
    qn = mla_q_norm.reshape(n_odd, 1, q_lora)
    kvn = mla_kv_norm.reshape(n_odd, 1, kv_lora)

    n_cond = 1 + dec_batch
    cond = jnp.concatenate([c_ctx[None, :], c, jnp.zeros((-n_cond % 16, d), F32)], axis=0)
    mod = _mod_call(cond, w_mod, b_mod).reshape(depth, cond.shape[0], 9, d)

    rope_tab = _rope_table(dec_seq, rope)
    ident_tab = _identity_rope_table(min(256, seq), rope)

    def trunk(x3, mod_rows, per_cond, caches):
        bsz, l, _ = x3.shape
        x = x3.reshape(bsz * l, d)
        states = []
        for layer in range(depth):
            m = mod_rows[layer]
            g = lambda s: ln_g[layer, s].reshape(1, d)
            bb = lambda s: ln_b[layer, s].reshape(1, d)
            x = _ffn_call(x, m, per_cond, w13, w2, layer, 0, g(0), bb(0), 0, alpha)
            i = layer // 2
            if layer % 2 == 0:
                outs = _na_proj_call(x, m, per_cond, w_in, i, na_heads, na_dh, pool_groups, pool_ch,
                                     float(na_dh ** -0.5) * LOG2E, caches is None)
                qkv = outs[0].reshape(3 * na_heads, bsz, l, na_dh)
                u = outs[1].reshape(pool_groups, bsz, l, pool_ch)
                if caches is None:
                    states.append((outs[2], outs[3]))
                    a = _attn_call(qkv, 0, [(qkv, na_heads, qkv, 2 * na_heads)], na_heads, na_dh,
                                   256, 512, "na_dense_attn", heads=4)
                else:
                    kc = caches[0][:, i].reshape(bsz, -1, na_width).astype(BF16)
                    vc = caches[1][:, i].reshape(bsz, -1, na_width).astype(BF16)
                    bias = _na_bias_table(na_rpb[i], l // GRID_W)
                    a = _na_call(qkv, kc, vc, bias, (na_rpb.shape[2] + 1) // 2)
                pooled = _pool_call(u, w_pool, p_scale, i)
                x = _out_call(x, m, per_cond,
                              [(a.reshape(bsz * l, na_width), w_out0, (None, na_width, d), (i, 0, 0)),
                               (pooled.reshape(bsz * l, na_width), w_out0, (None, na_width, d), (i, 1, 0))],
                              g(1), bb(1), alpha)
            else:
                tab = ident_tab if caches is None else rope_tab
                outs = _mla_proj_call(x, m, per_cond, wd, qn, kvn, wq, wk, wv, tab, i, mla_dims,
                                      caches is None)
                q, k, v = (o.reshape(mla_heads, bsz, l, 2 * nope) for o in outs[:3])
                segments = [(k, 0, v, 0)]
                if caches is None:
                    states.append((outs[3], outs[4]))
                else:
                    ckv_c = caches[2][:, i].reshape(-1, kv_lora)
                    kpe_c = caches[3][:, i].reshape(-1, rope)
                    kpe_c = jnp.concatenate([kpe_c, jnp.zeros_like(kpe_c)], axis=-1)
                    k_c, v_c = _mla_expand_call(ckv_c, kpe_c, wk, wv, i, mla_heads, nope)
                    segments.append((k_c.reshape(mla_heads, bsz, -1, 2 * nope), 0,
                                     v_c.reshape(mla_heads, bsz, -1, 2 * nope), 0))
                o = _attn_call(q, 0, segments, mla_heads, v_dim, 4096, 256, "mla_attn",
                               heads=1 if l > 1024 else 4)
                x = _out_call(x, m, per_cond,
                              [(o.reshape(bsz * l, -1), w_out1, (None,) + w_out1.shape[1:], (i, 0, 0))],
                              g(1), bb(1), alpha)
            x = _ffn_call(x, m, per_cond, w13, w2, layer, 1, g(2), bb(2), 2, alpha)
        return x.reshape(bsz, l, d), states

    y_prompt, st = trunk(x_prompt, mod[:, 0:1], batch * seq, None)
    y_sample, _ = trunk(x_sample, mod[:, 1:n_cond], dec_seq,
                        (cache_na_k, cache_na_v, cache_mla_ckv, cache_mla_kpe))
    even = [s for layer, s in enumerate(st) if layer % 2 == 0]
    odd = [s for layer, s in enumerate(st) if layer % 2 == 1]
    new_na_k = jnp.stack([s[0].reshape(batch, seq, na_heads, na_dh) for s in even], axis=1)
    new_na_v = jnp.stack([s[1].reshape(batch, seq, na_heads, na_dh) for s in even], axis=1)
    new_mla_ckv = jnp.stack([s[0].reshape(batch, seq, kv_lora) for s in odd], axis=1)
    new_mla_kpe = jnp.stack([s[1].reshape(batch, seq, rope) for s in odd], axis=1)
    return (y_prompt, y_sample, new_na_k, new_na_v, new_mla_ckv, new_mla_kpe)
```
